```python
import math
import jax, jax.numpy as jnp
from jax import lax
import numpy as np

D_MODEL = 2048
BATCH = 1
SEQ = 8192
DEPTH = 2
DEC_BATCH = 32
DEC_SEQ = 1
PAST_LEN = 8192
PAGE_SIZE = 128

CONV_CH = 1024
CONV_K = 31
ATT_PATTERNS = ((128, 1), (512, 4), (2048, 16))
N_ATT_GROUPS = 3
H_PER_GROUP = 8
HEAD_DIM = 128
ATT_QKV = N_ATT_GROUPS * H_PER_GROUP * HEAD_DIM
ATT_OUT = H_PER_GROUP * HEAD_DIM
ATT_BLOCK = 128
DN_HEADS = 8
DN_DK = 128
DN_DV = 128
DN_QK = DN_HEADS * DN_DK
DN_QKV = DN_HEADS * (2 * DN_DK + DN_DV)
DN_OUT = DN_HEADS * DN_DV
DN_SHORT_K = 4
DN_CHUNK = 64
N_BRANCH = 3
IN_SPLITS = (CONV_CH, CONV_CH, CONV_CH, ATT_QKV, ATT_QKV, ATT_QKV, ATT_OUT,
             DN_QKV, DN_OUT, DN_HEADS, DN_HEADS, N_BRANCH * D_MODEL)
D_IN = sum(IN_SPLITS)
D_BRANCH = CONV_CH + ATT_OUT + DN_OUT
EPS = 1e-6
NEG_INF = -1e30
F32 = jnp.float32

kernel_name = 'hybrid_conv_dilated_attn_gated_delta_step'


def rms_norm(x, g):
    xf = x.astype(F32)
    y = xf * lax.rsqrt(jnp.mean(xf * xf, axis=-1, keepdims=True) + EPS)
    return (y * g.astype(F32)).astype(x.dtype)


def layer_norm(x, g, b):
    xf = x.astype(F32)
    xc = xf - jnp.mean(xf, axis=-1, keepdims=True)
    y = xc * lax.rsqrt(jnp.mean(xc * xc, axis=-1, keepdims=True) + EPS)
    return (y * g.astype(F32) + b.astype(F32)).astype(x.dtype)


def l2_norm(x):
    return x * lax.rsqrt(jnp.sum(x * x, axis=-1, keepdims=True) + EPS)


def causal_depthwise_conv(x, buf, w):
    xc = jnp.concatenate([buf.astype(x.dtype), x], axis=1)
    y = lax.conv_general_dilated(xc, w[:, None, :].astype(x.dtype), window_strides=(1,), padding='VALID',
                                 dimension_numbers=('NWC', 'WIO', 'NWC'), feature_group_count=x.shape[-1])
    return y, xc[:, xc.shape[1] - (w.shape[0] - 1):]


def dilated_attn_prompt(q, k, v, window, dil):
    B, S, H, Dh = q.shape
    span = window // dil
    L = S // dil
    nb = -(-L // ATT_BLOCK)
    Lp = nb * ATT_BLOCK

    def by_residue(t):
        t = t.reshape(B, L, dil, H, Dh).transpose(0, 2, 1, 3, 4)
        t = jnp.pad(t, ((0, 0), (0, 0), (0, Lp - L), (0, 0), (0, 0)))
        return t.reshape(B, dil, nb, ATT_BLOCK, H, Dh)

    def with_prev_block(t):
        prev = jnp.pad(t, ((0, 0), (0, 0), (1, 0), (0, 0), (0, 0), (0, 0)))[:, :, :nb]
        return jnp.concatenate([prev, t], axis=3)

    qb = by_residue(q)
    kb = with_prev_block(by_residue(k))
    vb = with_prev_block(by_residue(v))
    s = jnp.einsum('brnqhd,brnkhd->brnhqk', qb, kb, preferred_element_type=F32) * (Dh ** -0.5)
    qi = jnp.arange(ATT_BLOCK)[:, None]
    kj = jnp.arange(2 * ATT_BLOCK)[None, :]
    dist = ATT_BLOCK + qi - kj
    kpos = (jnp.arange(nb)[:, None, None] - 1) * ATT_BLOCK + kj
    mask = (dist >= 0) & (dist <= span) & (kpos >= 0)
    s = jnp.where(mask[:, None], s, NEG_INF)
    m = jnp.max(s, axis=-1, keepdims=True)
    p = jnp.exp(s - m)
    l = jnp.sum(p, axis=-1, keepdims=True)
    o = jnp.einsum('brnhqk,brnkhd->brnqhd', p, vb.astype(F32)) / jnp.swapaxes(l, 3, 4)
    lse = jnp.swapaxes((m + jnp.log(l))[..., 0], 3, 4)
    o = o.reshape(B, dil, Lp, H, Dh)[:, :, :L].transpose(0, 2, 1, 3, 4).reshape(B, S, H, Dh)
    lse = lse.reshape(B, dil, Lp, H)[:, :, :L].transpose(0, 2, 1, 3).reshape(B, S, H)
    return o, lse


def dilated_attn_cached(q, k, v, kv_buf, window, dil):
    B, T, H, Dh = q.shape
    Lb = kv_buf.shape[2]
    span = window // dil
    kc = jnp.concatenate([kv_buf[:, 0].astype(k.dtype), k], axis=1)
    vc = jnp.concatenate([kv_buf[:, 1].astype(v.dtype), v], axis=1)
    idx = Lb + jnp.arange(T)[:, None] - dil * jnp.arange(span + 1)[None, :]
    valid = idx >= 0
    idx = jnp.maximum(idx, 0)
    kg = kc[:, idx]
    vg = vc[:, idx]
    s = jnp.einsum('bthd,btmhd->bthm', q, kg, preferred_element_type=F32) * (Dh ** -0.5)
    s = jnp.where(valid[:, None, :], s, NEG_INF)
    m = jnp.max(s, axis=-1, keepdims=True)
    p = jnp.exp(s - m)
    l = jnp.sum(p, axis=-1, keepdims=True)
    o = jnp.einsum('bthm,btmhd->bthd', p, vg.astype(F32)) / l
    lse = (m + jnp.log(l))[..., 0]
    start = kc.shape[1] - Lb
    new_buf = jnp.stack([kc[:, start:], vc[:, start:]], axis=1)
    return o, lse, new_buf


def gated_delta_rule(q, k, v, beta, g, s0):
    B, T, H, dk = q.shape
    dv = v.shape[-1]
    C = DN_CHUNK
    n = -(-T // C)
    pad = n * C - T

    def chunked(t):
        t = jnp.pad(t, ((0, 0), (0, pad)) + ((0, 0),) * (t.ndim - 2))
        t = t.reshape((B, n, C) + t.shape[2:])
        return t.transpose((1, 0, 3, 2) + tuple(range(4, t.ndim)))

    qc, kc, vc, bc, gc = (chunked(t) for t in (q, k, v, beta, g))
    G = jnp.cumsum(gc, axis=-1)
    causal = jnp.tril(jnp.ones((C, C), dtype=bool))
    decay = jnp.exp(jnp.where(causal, G[..., :, None] - G[..., None, :], -jnp.inf))
    kb = kc * bc[..., None]
    strict = causal & ~jnp.eye(C, dtype=bool)
    a = jnp.where(strict, jnp.einsum('nbhid,nbhjd->nbhij', kb, kc) * decay, 0.0)
    rhs = jnp.concatenate([vc * bc[..., None], kb * jnp.exp(G)[..., None]], axis=-1)
    sol = lax.linalg.triangular_solve(a, rhs, left_side=True, lower=True, unit_diagonal=True)
    u0, w = sol[..., :dv], sol[..., dv:]
    qk = jnp.einsum('nbhid,nbhjd->nbhij', qc, kc) * decay
    q_dec = qc * jnp.exp(G)[..., None]
    g_last = G[..., -1]
    k_dec = kc * jnp.exp(g_last[..., None] - G)[..., None]

    def step(S, xs):
        u0_i, w_i, qk_i, qd_i, kd_i, gl_i = xs
        u = u0_i - jnp.einsum('bhck,bhkv->bhcv', w_i, S)
        o = jnp.einsum('bhck,bhkv->bhcv', qd_i, S) + jnp.einsum('bhij,bhjv->bhiv', qk_i, u)
        S = jnp.exp(gl_i)[..., None, None] * S + jnp.einsum('bhck,bhcv->bhkv', kd_i, u)
        return S, o

    s_fin, o = lax.scan(step, s0, (u0, w, qk, q_dec, k_dec, g_last))
    o = o.transpose(1, 0, 3, 2, 4).reshape(B, n * C, H, dv)[:, :T]
    return o, s_fin


def gated_deltanet(c_qkv, c_gate, c_beta, c_alpha, sc_buf, s0, dconv_w, a_log, dt_bias, o_norm_g):
    B, T, _ = c_qkv.shape
    y, sc_new = causal_depthwise_conv(c_qkv, sc_buf, dconv_w)
    y = jax.nn.silu(y).astype(F32)
    q, k, v = jnp.split(y, [DN_QK, 2 * DN_QK], axis=-1)
    q = l2_norm(q.reshape(B, T, DN_HEADS, DN_DK)) * (DN_DK ** -0.5)
    k = l2_norm(k.reshape(B, T, DN_HEADS, DN_DK))
    v = v.reshape(B, T, DN_HEADS, DN_DV)
    beta = jax.nn.sigmoid(c_beta.astype(F32))
    g = -jnp.exp(a_log.astype(F32)) * jax.nn.softplus(c_alpha.astype(F32) + dt_bias.astype(F32))
    o, s_new = gated_delta_rule(q, k, v, beta, g, s0.astype(F32))
    o = rms_norm(o, o_norm_g) * jax.nn.silu(c_gate.astype(F32).reshape(B, T, DN_HEADS, DN_DV))
    return o.reshape(B, T, DN_OUT).astype(c_qkv.dtype), sc_new, s_new.astype(s0.dtype)


def mixer_layer(x, conv_buf, kv_bufs, sc_buf, s0, norm_g, w_in, conv_w, conv_b, ln_g, ln_b,
                q_norm_g, k_norm_g, dconv_w, a_log, dt_bias, o_norm_g, w_branch, w_out):
    B, T, _ = x.shape
    h = rms_norm(x, norm_g)
    split_points = [int(c) for c in np.cumsum(IN_SPLITS)[:-1]]
    (a_val, a_glu, a_gate, b_q, b_k, b_v, b_gate, c_qkv, c_gate, c_beta, c_alpha,
     m_logits) = jnp.split(h @ w_in.astype(h.dtype), split_points, axis=-1)

    glu = a_val * jax.nn.sigmoid(a_glu)
    ya, conv_new = causal_depthwise_conv(glu, conv_buf, conv_w)
    ya = jax.nn.silu(layer_norm(ya + conv_b.astype(ya.dtype), ln_g, ln_b)) * jax.nn.silu(a_gate)

    shp = (B, T, N_ATT_GROUPS, H_PER_GROUP, HEAD_DIM)
    q = rms_norm(b_q.reshape(shp), q_norm_g)
    k = rms_norm(b_k.reshape(shp), k_norm_g)
    v = b_v.reshape(shp)
    outs, lses, kv_new = [], [], []
    for gi, (window, dil) in enumerate(ATT_PATTERNS):
        if kv_bufs is None:
            o, lse = dilated_attn_prompt(q[:, :, gi], k[:, :, gi], v[:, :, gi], window, dil)
            keep = min(window, T)
            kv_new.append(jnp.stack([k[:, T - keep:, gi], v[:, T - keep:, gi]], axis=1))
        else:
            o, lse, buf = dilated_attn_cached(q[:, :, gi], k[:, :, gi], v[:, :, gi], kv_bufs[gi], window, dil)
            kv_new.append(buf)
        outs.append(o)
        lses.append(lse)
    alpha = jax.nn.softmax(jnp.stack(lses), axis=0)
    yb = jnp.einsum('gbth,gbthd->bthd', alpha, jnp.stack(outs)).reshape(B, T, ATT_OUT)
    yb = yb.astype(x.dtype) * jax.nn.silu(b_gate)

    yc, sc_new, s_new = gated_deltanet(c_qkv, c_gate, c_beta, c_alpha, sc_buf, s0,
                                       dconv_w, a_log, dt_bias, o_norm_g)

    wa, wb, wc = jnp.split(w_branch.astype(x.dtype), [CONV_CH, CONV_CH + ATT_OUT], axis=0)
    ga, gb, gc = jnp.split(jax.nn.sigmoid(m_logits), N_BRANCH, axis=-1)
    y = ga * (ya @ wa) + gb * (yb @ wb) + gc * (yc @ wc)
    out = x + y @ w_out.astype(x.dtype)
    return out, conv_new, kv_new, sc_new, s_new


def setup_inputs(seed: int = 0) -> dict:
    key = jax.random.key(seed)
    ks = jax.random.split(key, 24)

    def nrm(k, shape, scale):
        return scale * jax.random.normal(k, shape, F32)

    lb = [min(w, PAST_LEN) for w, _ in ATT_PATTERNS]

    def kv_shape(n):
        return (DEPTH, DEC_BATCH, 2, n, H_PER_GROUP, HEAD_DIM)

    dt = jnp.exp(jax.random.uniform(ks[21], (DEPTH, DN_HEADS), F32, math.log(1e-3), math.log(1e-1)))
    return {
        'x_prompt': nrm(ks[0], (BATCH, SEQ, D_MODEL), 1.0),
        'x_sample': nrm(ks[1], (DEC_BATCH, DEC_SEQ, D_MODEL), 1.0),
        'state_conv': nrm(ks[2], (DEPTH, DEC_BATCH, CONV_K - 1, CONV_CH), 0.5),
        'cache_kv_w128': nrm(ks[3], kv_shape(lb[0]), 1.0),
        'cache_kv_w512': nrm(ks[4], kv_shape(lb[1]), 1.0),
        'cache_kv_w2048': nrm(ks[5], kv_shape(lb[2]), 1.0),
        'state_short_conv': nrm(ks[6], (DEPTH, DEC_BATCH, DN_SHORT_K - 1, DN_QKV), 1.0),
        'state_delta': nrm(ks[7], (DEPTH, DEC_BATCH, DN_HEADS, DN_DK, DN_DV), DN_DK ** -0.5),
        'norm_g': 1.0 + nrm(ks[8], (DEPTH, D_MODEL), 0.01),
        'w_in': nrm(ks[9], (DEPTH, D_MODEL, D_IN), D_MODEL ** -0.5),
        'conv_w': nrm(ks[10], (DEPTH, CONV_K, CONV_CH), CONV_K ** -0.5),
        'conv_b': nrm(ks[11], (DEPTH, CONV_CH), 0.01),
        'ln_g': 1.0 + nrm(ks[12], (DEPTH, CONV_CH), 0.01),
        'ln_b': nrm(ks[13], (DEPTH, CONV_CH), 0.01),
        'q_norm_g': 1.0 + nrm(ks[14], (DEPTH, HEAD_DIM), 0.01),
        'k_norm_g': 1.0 + nrm(ks[15], (DEPTH, HEAD_DIM), 0.01),
        'dconv_w': nrm(ks[16], (DEPTH, DN_SHORT_K, DN_QKV), DN_SHORT_K ** -0.5),
        'a_log': jnp.log(jax.random.uniform(ks[17], (DEPTH, DN_HEADS), F32, 1.0, 16.0)),
        'dt_bias': dt + jnp.log(-jnp.expm1(-dt)),
        'o_norm_g': 1.0 + nrm(ks[18], (DEPTH, DN_DV), 0.01),
        'w_branch': nrm(ks[19], (DEPTH, D_BRANCH, D_MODEL), CONV_CH ** -0.5),
        'w_out': nrm(ks[20], (DEPTH, D_MODEL, D_MODEL), D_MODEL ** -0.5),
    }


def reference(x_prompt, x_sample, state_conv, cache_kv_w128, cache_kv_w512, cache_kv_w2048,
              state_short_conv, state_delta, norm_g, w_in, conv_w, conv_b, ln_g, ln_b,
              q_norm_g, k_norm_g, dconv_w, a_log, dt_bias, o_norm_g, w_branch, w_out):
    xp, xs = x_prompt, x_sample
    bp = x_prompt.shape[0]
    conv_p, conv_s, sc_p, sc_s, d_p, d_s = [], [], [], [], [], []
    kv_p = [[], [], []]
    kv_s = [[], [], []]
    for l in range(DEPTH):
        wts = (norm_g[l], w_in[l], conv_w[l], conv_b[l], ln_g[l], ln_b[l], q_norm_g[l], k_norm_g[l],
               dconv_w[l], a_log[l], dt_bias[l], o_norm_g[l], w_branch[l], w_out[l])
        xp, c_new, kv_new, sc_new, s_new = mixer_layer(
            xp, jnp.zeros((bp, CONV_K - 1, CONV_CH), xp.dtype), None,
            jnp.zeros((bp, DN_SHORT_K - 1, DN_QKV), xp.dtype),
            jnp.zeros((bp, DN_HEADS, DN_DK, DN_DV), xp.dtype), *wts)
        conv_p.append(c_new)
        sc_p.append(sc_new)
        d_p.append(s_new)
        for gi in range(N_ATT_GROUPS):
            kv_p[gi].append(kv_new[gi])
        xs, c_new, kv_new, sc_new, s_new = mixer_layer(
            xs, state_conv[l], (cache_kv_w128[l], cache_kv_w512[l], cache_kv_w2048[l]),
            state_short_conv[l], state_delta[l], *wts)
        conv_s.append(c_new)
        sc_s.append(sc_new)
        d_s.append(s_new)
        for gi in range(N_ATT_GROUPS):
            kv_s[gi].append(kv_new[gi])
    return (xp, xs,
            jnp.stack(conv_p), jnp.stack(conv_s),
            jnp.stack(kv_p[0]), jnp.stack(kv_s[0]),
            jnp.stack(kv_p[1]), jnp.stack(kv_s[1]),
            jnp.stack(kv_p[2]), jnp.stack(kv_s[2]),
            jnp.stack(sc_p), jnp.stack(sc_s),
            jnp.stack(d_p), jnp.stack(d_s))
```

```python
import functools
import math

import jax
import jax.numpy as jnp
import numpy as np
from jax import lax
from jax.experimental import pallas as pl
from jax.experimental.pallas import tpu as pltpu

D_MODEL = 2048
DEPTH = 2
CONV_CH = 1024
CONV_K = 31
ATT_PATTERNS = ((128, 1), (512, 4), (2048, 16))
N_ATT_GROUPS = 3
H_PER_GROUP = 8
HEAD_DIM = 128
ATT_QKV = N_ATT_GROUPS * H_PER_GROUP * HEAD_DIM
ATT_OUT = H_PER_GROUP * HEAD_DIM
ATT_BLOCK = 128
DN_HEADS = 8
DN_DK = 128
DN_DV = 128
DN_QK = DN_HEADS * DN_DK
DN_QKV = DN_HEADS * (2 * DN_DK + DN_DV)
DN_OUT = DN_HEADS * DN_DV
DN_SHORT_K = 4
DN_CHUNK = 64
N_BRANCH = 3
IN_SPLITS = (CONV_CH, CONV_CH, CONV_CH, ATT_QKV, ATT_QKV, ATT_QKV, ATT_OUT,
             DN_QKV, DN_OUT, DN_HEADS, DN_HEADS, N_BRANCH * D_MODEL)
D_IN = sum(IN_SPLITS)
EPS = 1e-6
NEG_INF = -1e30
F32 = jnp.float32
BF16 = jnp.bfloat16

V7X_VMEM_LIMIT_BYTES = 56 * 1024 * 1024


def _cdiv(a, b):
    return -(-a // b)


def _proj_body(*refs, normalize, residual):
    a_ref, g_ref, w_ref = refs[:3]
    r_ref = refs[3] if residual else None
    o_ref, lhs_ref = refs[-2:]

    @pl.when(pl.program_id(1) == 0)
    def _():
        a = a_ref[...]
        if normalize:
            a = a * lax.rsqrt(jnp.mean(a * a, axis=-1, keepdims=True) + EPS) * g_ref[...]
        lhs_ref[...] = a.astype(BF16)

    acc = jnp.dot(lhs_ref[...], w_ref[...].astype(BF16), preferred_element_type=F32)
    if residual:
        acc = acc + r_ref[...]
    o_ref[...] = acc


def _proj(a, w, gain=None, resid=None, *, tm, tn):
    m, k = a.shape
    n = w.shape[1]
    tm = min(tm, m)
    tn = min(tn, n)
    normalize = gain is not None
    if gain is None:
        gain = jnp.ones((k,), F32)
    in_specs = [
        pl.BlockSpec((tm, k), lambda i, j: (i, 0)),
        pl.BlockSpec((1, k), lambda i, j: (0, 0)),
        pl.BlockSpec((k, tn), lambda i, j: (0, j)),
    ]
    args = [a, gain.reshape(1, k), w]
    if resid is not None:
        in_specs.append(pl.BlockSpec((tm, tn), lambda i, j: (i, j)))
        args.append(resid)
    return pl.pallas_call(
        functools.partial(_proj_body, normalize=normalize, residual=resid is not None),
        grid=(_cdiv(m, tm), _cdiv(n, tn)),
        in_specs=in_specs,
        out_specs=pl.BlockSpec((tm, tn), lambda i, j: (i, j)),
        out_shape=jax.ShapeDtypeStruct((m, n), F32),
        scratch_shapes=[pltpu.VMEM((tm, k), BF16)],
        compiler_params=pltpu.CompilerParams(
            dimension_semantics=("parallel", "arbitrary"),
            vmem_limit_bytes=V7X_VMEM_LIMIT_BYTES),
        name="proj",
    )(*args)


def _rms_norm(x, g):
    y = x * lax.rsqrt(jnp.mean(x * x, axis=-1, keepdims=True) + EPS)
    return y * g


def _layer_norm(x, g, b):
    xc = x - jnp.mean(x, axis=-1, keepdims=True)
    y = xc * lax.rsqrt(jnp.mean(xc * xc, axis=-1, keepdims=True) + EPS)
    return y * g + b


def _l2_norm(x):
    return x * lax.rsqrt(jnp.sum(x * x, axis=-1, keepdims=True) + EPS)


def _causal_depthwise_conv(x, buf, w):
    xc = jnp.concatenate([buf, x], axis=1)
    y = lax.conv_general_dilated(xc, w[:, None, :], window_strides=(1,), padding='VALID',
                                 dimension_numbers=('NWC', 'WIO', 'NWC'), feature_group_count=x.shape[-1])
    return y, xc[:, xc.shape[1] - (w.shape[0] - 1):]


def _dilated_attn_prompt(q, k, v, window, dil):
    B, S, H, Dh = q.shape
    span = window // dil
    L = S // dil
    nb = -(-L // ATT_BLOCK)
    Lp = nb * ATT_BLOCK

    def by_residue(t):
        t = t.reshape(B, L, dil, H, Dh).transpose(0, 2, 1, 3, 4)
        t = jnp.pad(t, ((0, 0), (0, 0), (0, Lp - L), (0, 0), (0, 0)))
        return t.reshape(B, dil, nb, ATT_BLOCK, H, Dh)

    def with_prev_block(t):
        prev = jnp.pad(t, ((0, 0), (0, 0), (1, 0), (0, 0), (0, 0), (0, 0)))[:, :, :nb]
        return jnp.concatenate([prev, t], axis=3)

    qb = by_residue(q)
    kb = with_prev_block(by_residue(k))
    vb = with_prev_block(by_residue(v))
    s = jnp.einsum('brnqhd,brnkhd->brnhqk', qb, kb, preferred_element_type=F32) * (Dh ** -0.5)
    qi = jnp.arange(ATT_BLOCK)[:, None]
    kj = jnp.arange(2 * ATT_BLOCK)[None, :]
    dist = ATT_BLOCK + qi - kj
    kpos = (jnp.arange(nb)[:, None, None] - 1) * ATT_BLOCK + kj
    mask = (dist >= 0) & (dist <= span) & (kpos >= 0)
    s = jnp.where(mask[:, None], s, NEG_INF)
    m = jnp.max(s, axis=-1, keepdims=True)
    p = jnp.exp(s - m)
    l = jnp.sum(p, axis=-1, keepdims=True)
    o = jnp.einsum('brnhqk,brnkhd->brnqhd', p, vb) / jnp.swapaxes(l, 3, 4)
    lse = jnp.swapaxes((m + jnp.log(l))[..., 0], 3, 4)
    o = o.reshape(B, dil, Lp, H, Dh)[:, :, :L].transpose(0, 2, 1, 3, 4).reshape(B, S, H, Dh)
    lse = lse.reshape(B, dil, Lp, H)[:, :, :L].transpose(0, 2, 1, 3).reshape(B, S, H)
    return o, lse


def _dilated_attn_cached(q, k, v, kv_buf, window, dil):
    B, T, H, Dh = q.shape
    Lb = kv_buf.shape[2]
    span = window // dil
    kc = jnp.concatenate([kv_buf[:, 0], k], axis=1)
    vc = jnp.concatenate([kv_buf[:, 1], v], axis=1)
    idx = Lb + jnp.arange(T)[:, None] - dil * jnp.arange(span + 1)[None, :]
    valid = idx >= 0
    idx = jnp.maximum(idx, 0)
    kg = kc[:, idx]
    vg = vc[:, idx]
    s = jnp.einsum('bthd,btmhd->bthm', q, kg, preferred_element_type=F32) * (Dh ** -0.5)
    s = jnp.where(valid[:, None, :], s, NEG_INF)
    m = jnp.max(s, axis=-1, keepdims=True)
    p = jnp.exp(s - m)
    l = jnp.sum(p, axis=-1, keepdims=True)
    o = jnp.einsum('bthm,btmhd->bthd', p, vg) / l
    lse = (m + jnp.log(l))[..., 0]
    start = kc.shape[1] - Lb
    new_buf = jnp.stack([kc[:, start:], vc[:, start:]], axis=1)
    return o, lse, new_buf


def _gated_delta_rule(q, k, v, beta, g, s0):
    B, T, H, dk = q.shape
    dv = v.shape[-1]
    C = DN_CHUNK
    n = -(-T // C)
    pad = n * C - T

    def chunked(t):
        t = jnp.pad(t, ((0, 0), (0, pad)) + ((0, 0),) * (t.ndim - 2))
        t = t.reshape((B, n, C) + t.shape[2:])
        return t.transpose((1, 0, 3, 2) + tuple(range(4, t.ndim)))

    qc, kc, vc, bc, gc = (chunked(t) for t in (q, k, v, beta, g))
    G = jnp.cumsum(gc, axis=-1)
    causal = jnp.tril(jnp.ones((C, C), dtype=bool))
    decay = jnp.exp(jnp.where(causal, G[..., :, None] - G[..., None, :], -jnp.inf))
    kb = kc * bc[..., None]
    strict = causal & ~jnp.eye(C, dtype=bool)
    a = jnp.where(strict, jnp.einsum('nbhid,nbhjd->nbhij', kb, kc) * decay, 0.0)
    rhs = jnp.concatenate([vc * bc[..., None], kb * jnp.exp(G)[..., None]], axis=-1)
    sol = lax.linalg.triangular_solve(a, rhs, left_side=True, lower=True, unit_diagonal=True)
    u0, w = sol[..., :dv], sol[..., dv:]
    qk = jnp.einsum('nbhid,nbhjd->nbhij', qc, kc) * decay
    q_dec = qc * jnp.exp(G)[..., None]
    g_last = G[..., -1]
    k_dec = kc * jnp.exp(g_last[..., None] - G)[..., None]

    def step(S, xs):
        u0_i, w_i, qk_i, qd_i, kd_i, gl_i = xs
        u = u0_i - jnp.einsum('bhck,bhkv->bhcv', w_i, S)
        o = jnp.einsum('bhck,bhkv->bhcv', qd_i, S) + jnp.einsum('bhij,bhjv->bhiv', qk_i, u)
        S = jnp.exp(gl_i)[..., None, None] * S + jnp.einsum('bhck,bhcv->bhkv', kd_i, u)
        return S, o

    s_fin, o = lax.scan(step, s0, (u0, w, qk, q_dec, k_dec, g_last))
    o = o.transpose(1, 0, 3, 2, 4).reshape(B, n * C, H, dv)[:, :T]
    return o, s_fin


def _gated_deltanet(c_qkv, c_gate, c_beta, c_alpha, sc_buf, s0, dconv_w, a_log, dt_bias, o_norm_g):
    B, T, _ = c_qkv.shape
    y, sc_new = _causal_depthwise_conv(c_qkv, sc_buf, dconv_w)
    y = jax.nn.silu(y)
    q, k, v = jnp.split(y, [DN_QK, 2 * DN_QK], axis=-1)
    q = _l2_norm(q.reshape(B, T, DN_HEADS, DN_DK)) * (DN_DK ** -0.5)
    k = _l2_norm(k.reshape(B, T, DN_HEADS, DN_DK))
    v = v.reshape(B, T, DN_HEADS, DN_DV)
    beta = jax.nn.sigmoid(c_beta)
    g = -jnp.exp(a_log) * jax.nn.softplus(c_alpha + dt_bias)
    o, s_new = _gated_delta_rule(q, k, v, beta, g, s0)
    o = _rms_norm(o, o_norm_g) * jax.nn.silu(c_gate.reshape(B, T, DN_HEADS, DN_DV))
    return o.reshape(B, T, DN_OUT), sc_new, s_new


def _mixer_layer(x, conv_buf, kv_bufs, sc_buf, s0, norm_g, w_in, conv_w, conv_b, ln_g, ln_b,
                 q_norm_g, k_norm_g, dconv_w, a_log, dt_bias, o_norm_g, w_branch, w_out, *, tm):
    B, T, _ = x.shape
    x2 = x.reshape(B * T, D_MODEL)
    proj = _proj(x2, w_in, gain=norm_g, tm=tm, tn=512).reshape(B, T, D_IN)
    split_points = [int(c) for c in np.cumsum(IN_SPLITS)[:-1]]
    (a_val, a_glu, a_gate, b_q, b_k, b_v, b_gate, c_qkv, c_gate, c_beta, c_alpha,
     m_logits) = jnp.split(proj, split_points, axis=-1)

    glu = a_val * jax.nn.sigmoid(a_glu)
    ya, conv_new = _causal_depthwise_conv(glu, conv_buf, conv_w)
    ya = jax.nn.silu(_layer_norm(ya + conv_b, ln_g, ln_b)) * jax.nn.silu(a_gate)

    shp = (B, T, N_ATT_GROUPS, H_PER_GROUP, HEAD_DIM)
    q = _rms_norm(b_q.reshape(shp), q_norm_g)
    k = _rms_norm(b_k.reshape(shp), k_norm_g)
    v = b_v.reshape(shp)
    outs, lses, kv_new = [], [], []
    for gi, (window, dil) in enumerate(ATT_PATTERNS):
        if kv_bufs is None:
            o, lse = _dilated_attn_prompt(q[:, :, gi], k[:, :, gi], v[:, :, gi], window, dil)
            keep = min(window, T)
            kv_new.append(jnp.stack([k[:, T - keep:, gi], v[:, T - keep:, gi]], axis=1))
        else:
            o, lse, buf = _dilated_attn_cached(q[:, :, gi], k[:, :, gi], v[:, :, gi], kv_bufs[gi], window, dil)
            kv_new.append(buf)
        outs.append(o)
        lses.append(lse)
    alpha = jax.nn.softmax(jnp.stack(lses), axis=0)
    yb = jnp.einsum('gbth,gbthd->bthd', alpha, jnp.stack(outs)).reshape(B, T, ATT_OUT)
    yb = yb * jax.nn.silu(b_gate)

    yc, sc_new, s_new = _gated_deltanet(c_qkv, c_gate, c_beta, c_alpha, sc_buf, s0,
                                        dconv_w, a_log, dt_bias, o_norm_g)

    wa, wb, wc = jnp.split(w_branch, [CONV_CH, CONV_CH + ATT_OUT], axis=0)
    ga, gb, gc = jnp.split(jax.nn.sigmoid(m_logits), N_BRANCH, axis=-1)
    m = B * T
    pa = _proj(ya.reshape(m, CONV_CH), wa, tm=tm, tn=512).reshape(B, T, D_MODEL)
    pb = _proj(yb.reshape(m, ATT_OUT), wb, tm=tm, tn=512).reshape(B, T, D_MODEL)
    pc = _proj(yc.reshape(m, DN_OUT), wc, tm=tm, tn=512).reshape(B, T, D_MODEL)
    y = ga * pa + gb * pb + gc * pc
    out = _proj(y.reshape(m, D_MODEL), w_out, resid=x2, tm=tm, tn=512).reshape(B, T, D_MODEL)
    return out, conv_new, kv_new, sc_new, s_new


def kernel(x_prompt, x_sample, state_conv, cache_kv_w128, cache_kv_w512, cache_kv_w2048, state_short_conv, state_delta, norm_g, w_in, conv_w, conv_b, ln_g, ln_b, q_norm_g, k_norm_g, dconv_w, a_log, dt_bias, o_norm_g, w_branch, w_out):
    xp, xs = x_prompt, x_sample
    bp = x_prompt.shape[0]
    conv_p, conv_s, sc_p, sc_s, d_p, d_s = [], [], [], [], [], []
    kv_p = [[], [], []]
    kv_s = [[], [], []]
    for l in range(DEPTH):
        wts = (norm_g[l], w_in[l], conv_w[l], conv_b[l], ln_g[l], ln_b[l], q_norm_g[l], k_norm_g[l],
               dconv_w[l], a_log[l], dt_bias[l], o_norm_g[l], w_branch[l], w_out[l])
        xp, c_new, kv_new, sc_new, s_new = _mixer_layer(
            xp, jnp.zeros((bp, CONV_K - 1, CONV_CH), F32), None,
            jnp.zeros((bp, DN_SHORT_K - 1, DN_QKV), F32),
            jnp.zeros((bp, DN_HEADS, DN_DK, DN_DV), F32), *wts, tm=1024)
        conv_p.append(c_new)
        sc_p.append(sc_new)
        d_p.append(s_new)
        for gi in range(N_ATT_GROUPS):
            kv_p[gi].append(kv_new[gi])
        xs, c_new, kv_new, sc_new, s_new = _mixer_layer(
            xs, state_conv[l], (cache_kv_w128[l], cache_kv_w512[l], cache_kv_w2048[l]),
            state_short_conv[l], state_delta[l], *wts, tm=32)
        conv_s.append(c_new)
        sc_s.append(sc_new)
        d_s.append(s_new)
        for gi in range(N_ATT_GROUPS):
            kv_s[gi].append(kv_new[gi])
    return (xp, xs,
            jnp.stack(conv_p), jnp.stack(conv_s),
            jnp.stack(kv_p[0]), jnp.stack(kv_s[0]),
            jnp.stack(kv_p[1]), jnp.stack(kv_s[1]),
            jnp.stack(kv_p[2]), jnp.stack(kv_s[2]),
            jnp.stack(sc_p), jnp.stack(sc_s),
            jnp.stack(d_p), jnp.stack(d_s))
```

```python
import functools
import math

import jax
import jax.numpy as jnp
import numpy as np
from jax import lax
from jax.experimental import pallas as pl
from jax.experimental.pallas import tpu as pltpu

D_MODEL = 2048
DEPTH = 2
CONV_CH = 1024
CONV_K = 31
ATT_PATTERNS = ((128, 1), (512, 4), (2048, 16))
N_ATT_GROUPS = 3
H_PER_GROUP = 8
HEAD_DIM = 128
ATT_QKV = N_ATT_GROUPS * H_PER_GROUP * HEAD_DIM
ATT_OUT = H_PER_GROUP * HEAD_DIM
ATT_BLOCK = 128
DN_HEADS = 8
DN_DK = 128
DN_DV = 128
DN_QK = DN_HEADS * DN_DK
DN_QKV = DN_HEADS * (2 * DN_DK + DN_DV)
DN_OUT = DN_HEADS * DN_DV
DN_SHORT_K = 4
DN_CHUNK = 64
N_BRANCH = 3
IN_SPLITS = (CONV_CH, CONV_CH, CONV_CH, ATT_QKV, ATT_QKV, ATT_QKV, ATT_OUT,
             DN_QKV, DN_OUT, DN_HEADS, DN_HEADS, N_BRANCH * D_MODEL)
D_IN = sum(IN_SPLITS)
EPS = 1e-6
NEG_INF = -1e30
F32 = jnp.float32
BF16 = jnp.bfloat16

V7X_VMEM_LIMIT_BYTES = 56 * 1024 * 1024


def _cdiv(a, b):
    return -(-a // b)


def _proj_body(*refs, normalize, residual):
    a_ref, g_ref, w_ref = refs[:3]
    r_ref = refs[3] if residual else None
    o_ref, lhs_ref = refs[-2:]

    @pl.when(pl.program_id(1) == 0)
    def _():
        a = a_ref[...]
        if normalize:
            a = a * lax.rsqrt(jnp.mean(a * a, axis=-1, keepdims=True) + EPS) * g_ref[...]
        lhs_ref[...] = a.astype(BF16)

    acc = jnp.dot(lhs_ref[...], w_ref[...], preferred_element_type=F32)
    if residual:
        acc = acc + r_ref[...]
    o_ref[...] = acc


def _proj(a, w, gain=None, resid=None, *, tm, tn):
    m, k = a.shape
    n = w.shape[1]
    tm = min(tm, m)
    tn = min(tn, n)
    normalize = gain is not None
    if gain is None:
        gain = jnp.ones((k,), F32)
    in_specs = [
        pl.BlockSpec((tm, k), lambda i, j: (i, 0)),
        pl.BlockSpec((1, k), lambda i, j: (0, 0)),
        pl.BlockSpec((k, tn), lambda i, j: (0, j)),
    ]
    args = [a, gain.reshape(1, k), w]
    if resid is not None:
        in_specs.append(pl.BlockSpec((tm, tn), lambda i, j: (i, j)))
        args.append(resid)
    return pl.pallas_call(
        functools.partial(_proj_body, normalize=normalize, residual=resid is not None),
        grid=(_cdiv(m, tm), _cdiv(n, tn)),
        in_specs=in_specs,
        out_specs=pl.BlockSpec((tm, tn), lambda i, j: (i, j)),
        out_shape=jax.ShapeDtypeStruct((m, n), F32),
        scratch_shapes=[pltpu.VMEM((tm, k), BF16)],
        compiler_params=pltpu.CompilerParams(
            dimension_semantics=("parallel", "arbitrary"),
            vmem_limit_bytes=V7X_VMEM_LIMIT_BYTES),
        name="proj",
    )(*args)


LANES = 128
COL_A_VAL = 0
COL_A_GLU = CONV_CH
COL_A_GATE = 2 * CONV_CH
COL_Q = 3 * CONV_CH
COL_K = COL_Q + ATT_QKV
COL_V = COL_K + ATT_QKV
COL_B_GATE = COL_V + ATT_QKV
COL_C_QKV = COL_B_GATE + ATT_OUT
COL_C_GATE = COL_C_QKV + DN_QKV
COL_MLOG = COL_C_GATE + DN_OUT
COL_BETA = COL_MLOG + N_BRANCH * D_MODEL
COL_ALPHA = COL_BETA + DN_HEADS
MXU_WIDTH = 256
N_PROJ = _cdiv(D_IN, MXU_WIDTH) * MXU_WIDTH
PROJ_TN = 768
_SRC_BETA = COL_MLOG


def _prep_w_in(w_in):
    pad = jnp.zeros((w_in.shape[0], N_PROJ - D_IN), w_in.dtype)
    w = jnp.concatenate([w_in[:, :_SRC_BETA], w_in[:, _SRC_BETA + 2 * DN_HEADS:],
                         w_in[:, _SRC_BETA:_SRC_BETA + 2 * DN_HEADS], pad], axis=1)
    return w.astype(BF16)


ATT_SB = ATT_BLOCK * max(d for _, d in ATT_PATTERNS)
ATT_SPAN = ATT_BLOCK
assert all(w // d == ATT_SPAN for w, d in ATT_PATTERNS)


def _rows(start, size, stride):
    return pl.ds(start, size) if stride == 1 else pl.ds(start, size, stride=stride)


def _rms_rows(x, gain):
    return x * lax.rsqrt(jnp.mean(x * x, axis=-1, keepdims=True) + EPS) * gain


def _attn_prompt_body(q0, q1, q2, k0, k1, k2, v0, v1, v2, gate_ref, qg_ref, kg_ref,
                      yb_ref, kv0, kv1, kv2,
                      kr0, kr1, kr2, vr0, vr1, vr2, on0, on1, on2, ln0, ln1, ln2):
    q_refs, k_refs, v_refs = (q0, q1, q2), (k0, k1, k2), (v0, v1, v2)
    kv_refs = (kv0, kv1, kv2)
    kres, vres = (kr0, kr1, kr2), (vr0, vr1, vr2)
    onat, lnat = (on0, on1, on2), (ln0, ln1, ln2)
    i = pl.program_id(1)
    last = pl.num_programs(1) - 1
    qg = qg_ref[...]
    kg = kg_ref[...]
    scale = HEAD_DIM ** -0.5
    qi = lax.broadcasted_iota(jnp.int32, (ATT_BLOCK, 2 * ATT_BLOCK), 0)
    kj = lax.broadcasted_iota(jnp.int32, (ATT_BLOCK, 2 * ATT_BLOCK), 1)
    delta = kj - qi
    band = (delta >= 0) & (delta <= ATT_SPAN)

    for g, (_, d) in enumerate(ATT_PATTERNS):
        nblk = ATT_SB // (ATT_BLOCK * d)
        units = ATT_BLOCK * nblk

        @pl.when(i == 0)
        def _():
            kres[g][:, 0:ATT_BLOCK, :] = jnp.zeros((d, ATT_BLOCK, HEAD_DIM), BF16)
            vres[g][:, 0:ATT_BLOCK, :] = jnp.zeros((d, ATT_BLOCK, HEAD_DIM), BF16)

        @pl.when(i > 0)
        def _():
            kres[g][:, 0:ATT_BLOCK, :] = kres[g][:, units:units + ATT_BLOCK, :]
            vres[g][:, 0:ATT_BLOCK, :] = vres[g][:, units:units + ATT_BLOCK, :]

        for r in range(d):
            for c in range(nblk):
                rows = _rows(c * ATT_BLOCK * d + r, ATT_BLOCK, d)
                dst = slice(ATT_BLOCK + c * ATT_BLOCK, ATT_BLOCK + (c + 1) * ATT_BLOCK)
                kres[g][r, dst, :] = _rms_rows(k_refs[g][rows, :], kg).astype(BF16)
                vres[g][r, dst, :] = v_refs[g][rows, :].astype(BF16)

        for r in range(d):
            for c in range(nblk):
                rows = _rows(c * ATT_BLOCK * d + r, ATT_BLOCK, d)
                q = _rms_rows(q_refs[g][rows, :], qg).astype(BF16)
                kb = kres[g][r, c * ATT_BLOCK:(c + 2) * ATT_BLOCK, :]
                vb = vres[g][r, c * ATT_BLOCK:(c + 2) * ATT_BLOCK, :]
                s = lax.dot_general(q, kb, (((1,), (1,)), ((), ())), preferred_element_type=F32) * scale
                if c == 0:
                    first = jnp.where(i == 0, ATT_BLOCK, 0)
                    mask = band & (kj >= first)
                else:
                    mask = band
                s = jnp.where(mask, s, NEG_INF)
                m = jnp.max(s, axis=-1, keepdims=True)
                p = jnp.exp(s - m)
                l = jnp.sum(p, axis=-1, keepdims=True)
                o = jnp.dot(p.astype(BF16), vb, preferred_element_type=F32) / l
                onat[g][rows, :] = o
                lnat[g][rows, :] = jnp.broadcast_to(m + jnp.log(l), (ATT_BLOCK, HEAD_DIM))

    chunk = 256
    for c in range(ATT_SB // chunk):
        rows = slice(c * chunk, (c + 1) * chunk)
        l0, l1, l2 = lnat[0][rows, :], lnat[1][rows, :], lnat[2][rows, :]
        m = jnp.maximum(jnp.maximum(l0, l1), l2)
        e0, e1, e2 = jnp.exp(l0 - m), jnp.exp(l1 - m), jnp.exp(l2 - m)
        y = (e0 * onat[0][rows, :] + e1 * onat[1][rows, :] + e2 * onat[2][rows, :]) / (e0 + e1 + e2)
        gate = gate_ref[rows, :]
        yb_ref[rows, :] = y * (gate * jax.nn.sigmoid(gate))

    @pl.when(i == last)
    def _():
        for g, (window, _) in enumerate(ATT_PATTERNS):
            tail = slice(ATT_SB - window, ATT_SB)
            kv_refs[g][0, :, :] = _rms_rows(k_refs[g][tail, :], kg)
            kv_refs[g][1, :, :] = v_refs[g][tail, :]


def _attn_prompt(proj, q_norm_g, k_norm_g):
    t = proj.shape[0]
    assert t % ATT_SB == 0
    n_sb = t // ATT_SB

    def col_spec(col0, g):
        blk0 = col0 // LANES + g * H_PER_GROUP
        return pl.BlockSpec((ATT_SB, HEAD_DIM), lambda h, i: (i, blk0 + h))

    in_specs = ([col_spec(COL_Q, g) for g in range(N_ATT_GROUPS)]
                + [col_spec(COL_K, g) for g in range(N_ATT_GROUPS)]
                + [col_spec(COL_V, g) for g in range(N_ATT_GROUPS)]
                + [col_spec(COL_B_GATE, 0),
                   pl.BlockSpec((1, HEAD_DIM), lambda h, i: (0, 0)),
                   pl.BlockSpec((1, HEAD_DIM), lambda h, i: (0, 0))])
    out_specs = [pl.BlockSpec((ATT_SB, HEAD_DIM), lambda h, i: (i, h))]
    out_shape = [jax.ShapeDtypeStruct((t, ATT_OUT), F32)]
    scratch = []
    for window, _ in ATT_PATTERNS:
        out_specs.append(pl.BlockSpec((2, window, HEAD_DIM), lambda h, i: (0, 0, h)))
        out_shape.append(jax.ShapeDtypeStruct((2, window, ATT_OUT), F32))
    for _ in range(2):
        for _, d in ATT_PATTERNS:
            scratch.append(pltpu.VMEM((d, ATT_BLOCK + ATT_SB // d, HEAD_DIM), BF16))
    for _ in range(2 * N_ATT_GROUPS):
        scratch.append(pltpu.VMEM((ATT_SB, HEAD_DIM), F32))
    return pl.pallas_call(
        _attn_prompt_body,
        grid=(H_PER_GROUP, n_sb),
        in_specs=in_specs,
        out_specs=out_specs,
        out_shape=out_shape,
        scratch_shapes=scratch,
        compiler_params=pltpu.CompilerParams(
            dimension_semantics=("parallel", "arbitrary"),
            vmem_limit_bytes=V7X_VMEM_LIMIT_BYTES),
        name="attn_prompt",
    )(*([proj] * 10), q_norm_g.reshape(1, HEAD_DIM), k_norm_g.reshape(1, HEAD_DIM))


CONV_TT = 256
CONV_HALO = 32
CONV_RC = 64
CONV_LC = 256


def _silu(x):
    return x * jax.nn.sigmoid(x)


def _conv_a_body(aval_ref, aglu_ref, agate_ref, w_ref, b_ref, lg_ref, lb_ref,
                 ya_ref, tail_ref, ext_ref, y_ref):
    i = pl.program_id(0)

    @pl.when(i == 0)
    def _():
        ext_ref[0:CONV_HALO, :] = jnp.zeros((CONV_HALO, CONV_CH), F32)

    @pl.when(i > 0)
    def _():
        ext_ref[0:CONV_HALO, :] = ext_ref[CONV_TT:CONV_TT + CONV_HALO, :]

    for rc in range(CONV_TT // CONV_RC):
        rows = slice(rc * CONV_RC, (rc + 1) * CONV_RC)
        ext_ref[CONV_HALO + rc * CONV_RC:CONV_HALO + (rc + 1) * CONV_RC, :] = (
            aval_ref[rows, :] * jax.nn.sigmoid(aglu_ref[rows, :]))

    first = CONV_HALO - (CONV_K - 1)
    for lc in range(CONV_CH // CONV_LC):
        lanes = slice(lc * CONV_LC, (lc + 1) * CONV_LC)
        for rc in range(CONV_TT // CONV_RC):
            acc = jnp.broadcast_to(b_ref[:, lanes], (CONV_RC, CONV_LC))
            for k in range(CONV_K):
                r0 = first + rc * CONV_RC + k
                acc = acc + w_ref[k:k + 1, lanes] * ext_ref[r0:r0 + CONV_RC, lanes]
            y_ref[rc * CONV_RC:(rc + 1) * CONV_RC, lanes] = acc

    for rc in range(CONV_TT // CONV_RC):
        rows = slice(rc * CONV_RC, (rc + 1) * CONV_RC)
        y = y_ref[rows, :]
        yc = y - jnp.mean(y, axis=-1, keepdims=True)
        yn = yc * lax.rsqrt(jnp.mean(yc * yc, axis=-1, keepdims=True) + EPS)
        ya_ref[rows, :] = _silu(yn * lg_ref[...] + lb_ref[...]) * _silu(agate_ref[rows, :])

    @pl.when(i == pl.num_programs(0) - 1)
    def _():
        tail_ref[...] = ext_ref[CONV_HALO + CONV_TT - (CONV_K - 1):CONV_HALO + CONV_TT, :]


def _conv_a_prompt(proj, conv_w, conv_b, ln_g, ln_b):
    t = proj.shape[0]
    assert t % CONV_TT == 0
    nb = CONV_CH // LANES

    def col_spec(col0):
        return pl.BlockSpec((CONV_TT, CONV_CH), lambda i: (i, col0 // CONV_CH))

    def full(shape):
        return pl.BlockSpec(shape, lambda i: (0,) * len(shape))

    del nb
    return pl.pallas_call(
        _conv_a_body,
        grid=(t // CONV_TT,),
        in_specs=[col_spec(COL_A_VAL), col_spec(COL_A_GLU), col_spec(COL_A_GATE),
                  full((CONV_K, CONV_CH)), full((1, CONV_CH)), full((1, CONV_CH)), full((1, CONV_CH))],
        out_specs=[pl.BlockSpec((CONV_TT, CONV_CH), lambda i: (i, 0)), full((CONV_K - 1, CONV_CH))],
        out_shape=[jax.ShapeDtypeStruct((t, CONV_CH), F32),
                   jax.ShapeDtypeStruct((CONV_K - 1, CONV_CH), F32)],
        scratch_shapes=[pltpu.VMEM((CONV_HALO + CONV_TT, CONV_CH), F32),
                        pltpu.VMEM((CONV_TT, CONV_CH), F32)],
        compiler_params=pltpu.CompilerParams(
            dimension_semantics=("arbitrary",), vmem_limit_bytes=V7X_VMEM_LIMIT_BYTES),
        name="conv_a",
    )(proj, proj, proj, conv_w, conv_b.reshape(1, CONV_CH), ln_g.reshape(1, CONV_CH), ln_b.reshape(1, CONV_CH))


DN_TB = 256
DN_NC = DN_TB // DN_CHUNK
DN_HALO = 8
assert DN_DK == DN_DV == LANES


def _split_bf16(x, parts):
    out = []
    for _ in range(parts):
        hi = x.astype(BF16)
        out.append(hi)
        x = x - hi.astype(F32)
    return out


def _dot3(a, b):
    ah, al = _split_bf16(a, 2)
    bh, bl = _split_bf16(b, 2)
    return (jnp.dot(ah, bh, preferred_element_type=F32) + jnp.dot(ah, bl, preferred_element_type=F32)
            + jnp.dot(al, bh, preferred_element_type=F32))


def _dot_nt(a, b):
    return lax.dot_general(a, b, (((1,), (1,)), ((), ())), preferred_element_type=F32)


def _dn_prep_body(xq_ref, xk_ref, xv_ref, hq_ref, hk_ref, hv_ref, ba_ref, wq_ref, wk_ref, wv_ref,
                  pa_ref, pd_ref,
                  u0_ref, w_ref, qd_ref, kdt_ref, qk_ref, e_ref,
                  eq_ref, ek_ref, ev_ref):
    i = pl.program_id(0)
    for ext, halo, x in ((eq_ref, hq_ref, xq_ref), (ek_ref, hk_ref, xk_ref), (ev_ref, hv_ref, xv_ref)):
        ext[0:DN_HALO, :] = jnp.where(i == 0, 0.0, halo[...])
        ext[DN_HALO:DN_HALO + DN_TB, :] = x[...]

    ba = ba_ref[...]
    beta_all = jax.nn.sigmoid(ba)
    z = ba + pd_ref[...]
    g_all = -jnp.exp(pa_ref[...]) * (jnp.maximum(z, 0.0) + jnp.log1p(jnp.exp(-jnp.abs(z))))

    ri = lax.broadcasted_iota(jnp.int32, (DN_CHUNK, DN_CHUNK), 0)
    ci = lax.broadcasted_iota(jnp.int32, (DN_CHUNK, DN_CHUNK), 1)
    causal = ri >= ci
    strict = ri > ci
    tril = causal.astype(BF16)
    eye = (ri == ci).astype(F32)
    first = DN_HALO - (DN_SHORT_K - 1)

    for c in range(DN_NC):
        rows = slice(c * DN_CHUNK, (c + 1) * DN_CHUNK)
        g1, g2, g3 = _split_bf16(g_all[rows, :], 3)
        gsum = (jnp.dot(tril, g1, preferred_element_type=F32) + jnp.dot(tril, g2, preferred_element_type=F32)
                + jnp.dot(tril, g3, preferred_element_type=F32))
        for h in range(DN_HEADS):
            lanes = slice(h * LANES, (h + 1) * LANES)

            def short_conv(ext, wref):
                acc = None
                for k in range(DN_SHORT_K):
                    r0 = first + c * DN_CHUNK + k
                    term = wref[k:k + 1, lanes] * ext[r0:r0 + DN_CHUNK, lanes]
                    acc = term if acc is None else acc + term
                return _silu(acc)

            q = short_conv(eq_ref, wq_ref)
            k = short_conv(ek_ref, wk_ref)
            v = short_conv(ev_ref, wv_ref)
            q = q * lax.rsqrt(jnp.sum(q * q, axis=-1, keepdims=True) + EPS) * (DN_DK ** -0.5)
            k = k * lax.rsqrt(jnp.sum(k * k, axis=-1, keepdims=True) + EPS)
            bcol = jnp.broadcast_to(beta_all[rows, h:h + 1], (DN_CHUNK, LANES))
            gcol = jnp.broadcast_to(gsum[:, DN_HEADS + h:DN_HEADS + h + 1], (DN_CHUNK, LANES))
            grow = jnp.transpose(gcol)[0:DN_CHUNK, :]
            decay = jnp.where(causal, jnp.exp(gcol[:, 0:DN_CHUNK] - grow), 0.0)
            kb = k * bcol
            kbf = k.astype(BF16)
            a = jnp.where(strict, _dot_nt(kb.astype(BF16), kbf) * decay, 0.0)
            qk = _dot_nt(q.astype(BF16), kbf) * decay
            eg = jnp.exp(gcol)
            rhs = jnp.concatenate([v * bcol, kb * eg], axis=1)
            tinv = eye - a
            p = a
            for _ in range(5):
                p = _dot3(p, p)
                tinv = tinv + _dot3(p, tinv)
            sol = _dot3(tinv, rhs)
            glast = gcol[DN_CHUNK - 1:DN_CHUNK, :]
            u0_ref[rows, lanes] = sol[:, 0:DN_DV]
            w_ref[rows, lanes] = sol[:, DN_DV:].astype(BF16)
            qd_ref[rows, lanes] = (q * eg).astype(BF16)
            kdt_ref[c, lanes, :] = jnp.transpose(k * jnp.exp(glast - gcol)).astype(BF16)
            qk_ref[rows, h * LANES:h * LANES + DN_CHUNK] = qk.astype(BF16)
            qk_ref[rows, h * LANES + DN_CHUNK:(h + 1) * LANES] = jnp.zeros((DN_CHUNK, LANES - DN_CHUNK), BF16)
            e_ref[c, h:h + 1, :] = jnp.exp(glast)


def _dn_rec_body(u0_ref, w_ref, qd_ref, kdt_ref, qk_ref, e_ref, gate_ref, og_ref,
                 yc_ref, sfin_ref, s_ref):
    i = pl.program_id(0)

    @pl.when(i == 0)
    def _():
        s_ref[...] = jnp.zeros(s_ref.shape, F32)

    og = og_ref[...]
    for c in range(DN_NC):
        rows = slice(c * DN_CHUNK, (c + 1) * DN_CHUNK)
        for h in range(DN_HEADS):
            lanes = slice(h * LANES, (h + 1) * LANES)
            s = s_ref[h]
            sb = s.astype(BF16)
            u = u0_ref[rows, lanes] - jnp.dot(w_ref[rows, lanes], sb, preferred_element_type=F32)
            ub = u.astype(BF16)
            o = (jnp.dot(qd_ref[rows, lanes], sb, preferred_element_type=F32)
                 + jnp.dot(qk_ref[rows, h * LANES:h * LANES + DN_CHUNK], ub, preferred_element_type=F32))
            s_ref[h] = e_ref[c, h:h + 1, :] * s + jnp.dot(kdt_ref[c, lanes, :], ub, preferred_element_type=F32)
            yc_ref[rows, lanes] = _rms_rows(o, og) * _silu(gate_ref[rows, lanes])

    @pl.when(i == pl.num_programs(0) - 1)
    def _():
        sfin_ref[...] = s_ref[...]


def _deltanet_prompt(proj, dconv_w, a_log, dt_bias, o_norm_g):
    t = proj.shape[0]
    assert t % DN_TB == 0
    nt = t // DN_TB
    qkv_blk = COL_C_QKV // DN_QK
    halo_per_tb = DN_TB // DN_HALO

    def x_spec(j):
        return pl.BlockSpec((DN_TB, DN_QK), lambda i: (i, qkv_blk + j))

    def halo_spec(j):
        return pl.BlockSpec((DN_HALO, DN_QK), lambda i: (jnp.maximum(i * halo_per_tb - 1, 0), qkv_blk + j))

    def w_spec(j):
        return pl.BlockSpec((DN_SHORT_K, DN_QK), lambda i: (0, j))

    row = pl.BlockSpec((1, LANES), lambda i: (0, 0))
    pad_a = jnp.zeros((1, LANES), F32).at[0, DN_HEADS:2 * DN_HEADS].set(a_log)
    pad_d = jnp.zeros((1, LANES), F32).at[0, DN_HEADS:2 * DN_HEADS].set(dt_bias)
    wide = pl.BlockSpec((DN_TB, DN_OUT), lambda i: (i, 0))
    kdt_spec = pl.BlockSpec((DN_NC, DN_OUT, DN_CHUNK), lambda i: (i, 0, 0))
    e_spec = pl.BlockSpec((DN_NC, DN_HEADS, LANES), lambda i: (i, 0, 0))
    u0, w, qd, kdt, qk, e = pl.pallas_call(
        _dn_prep_body,
        grid=(nt,),
        in_specs=[x_spec(0), x_spec(1), x_spec(2), halo_spec(0), halo_spec(1), halo_spec(2),
                  pl.BlockSpec((DN_TB, LANES), lambda i: (i, COL_BETA // LANES)),
                  w_spec(0), w_spec(1), w_spec(2), row, row],
        out_specs=[wide, wide, wide, kdt_spec, wide, e_spec],
        out_shape=[jax.ShapeDtypeStruct((t, DN_OUT), F32),
                   jax.ShapeDtypeStruct((t, DN_OUT), BF16),
                   jax.ShapeDtypeStruct((t, DN_OUT), BF16),
                   jax.ShapeDtypeStruct((t // DN_CHUNK, DN_OUT, DN_CHUNK), BF16),
                   jax.ShapeDtypeStruct((t, DN_OUT), BF16),
                   jax.ShapeDtypeStruct((t // DN_CHUNK, DN_HEADS, LANES), F32)],
        scratch_shapes=[pltpu.VMEM((DN_HALO + DN_TB, DN_QK), F32) for _ in range(3)],
        compiler_params=pltpu.CompilerParams(
            dimension_semantics=("parallel",), vmem_limit_bytes=V7X_VMEM_LIMIT_BYTES),
        name="dn_prep",
    )(proj, proj, proj, proj, proj, proj, proj, dconv_w, dconv_w, dconv_w, pad_a, pad_d)

    state = pl.BlockSpec((DN_HEADS, DN_DK, DN_DV), lambda i: (0, 0, 0))
    yc, s_fin = pl.pallas_call(
        _dn_rec_body,
        grid=(nt,),
        in_specs=[wide, wide, wide, kdt_spec, wide, e_spec,
                  pl.BlockSpec((DN_TB, DN_OUT), lambda i: (i, COL_C_GATE // DN_OUT)), row],
        out_specs=[wide, state],
        out_shape=[jax.ShapeDtypeStruct((t, DN_OUT), F32),
                   jax.ShapeDtypeStruct((DN_HEADS, DN_DK, DN_DV), F32)],
        scratch_shapes=[pltpu.VMEM((DN_HEADS, DN_DK, DN_DV), F32)],
        compiler_params=pltpu.CompilerParams(
            dimension_semantics=("arbitrary",), vmem_limit_bytes=V7X_VMEM_LIMIT_BYTES),
        name="dn_rec",
    )(u0, w, qd, kdt, qk, e, proj, o_norm_g.reshape(1, LANES))
    return yc, s_fin


def _rms_norm(x, g):
    y = x * lax.rsqrt(jnp.mean(x * x, axis=-1, keepdims=True) + EPS)
    return y * g


def _layer_norm(x, g, b):
    xc = x - jnp.mean(x, axis=-1, keepdims=True)
    y = xc * lax.rsqrt(jnp.mean(xc * xc, axis=-1, keepdims=True) + EPS)
    return y * g + b


def _l2_norm(x):
    return x * lax.rsqrt(jnp.sum(x * x, axis=-1, keepdims=True) + EPS)


def _causal_depthwise_conv(x, buf, w):
    xc = jnp.concatenate([buf, x], axis=1)
    y = lax.conv_general_dilated(xc, w[:, None, :], window_strides=(1,), padding='VALID',
                                 dimension_numbers=('NWC', 'WIO', 'NWC'), feature_group_count=x.shape[-1])
    return y, xc[:, xc.shape[1] - (w.shape[0] - 1):]


def _dilated_attn_prompt(q, k, v, window, dil):
    B, S, H, Dh = q.shape
    span = window // dil
    L = S // dil
    nb = -(-L // ATT_BLOCK)
    Lp = nb * ATT_BLOCK

    def by_residue(t):
        t = t.reshape(B, L, dil, H, Dh).transpose(0, 2, 1, 3, 4)
        t = jnp.pad(t, ((0, 0), (0, 0), (0, Lp - L), (0, 0), (0, 0)))
        return t.reshape(B, dil, nb, ATT_BLOCK, H, Dh)

    def with_prev_block(t):
        prev = jnp.pad(t, ((0, 0), (0, 0), (1, 0), (0, 0), (0, 0), (0, 0)))[:, :, :nb]
        return jnp.concatenate([prev, t], axis=3)

    qb = by_residue(q)
    kb = with_prev_block(by_residue(k))
    vb = with_prev_block(by_residue(v))
    s = jnp.einsum('brnqhd,brnkhd->brnhqk', qb, kb, preferred_element_type=F32) * (Dh ** -0.5)
    qi = jnp.arange(ATT_BLOCK)[:, None]
    kj = jnp.arange(2 * ATT_BLOCK)[None, :]
    dist = ATT_BLOCK + qi - kj
    kpos = (jnp.arange(nb)[:, None, None] - 1) * ATT_BLOCK + kj
    mask = (dist >= 0) & (dist <= span) & (kpos >= 0)
    s = jnp.where(mask[:, None], s, NEG_INF)
    m = jnp.max(s, axis=-1, keepdims=True)
    p = jnp.exp(s - m)
    l = jnp.sum(p, axis=-1, keepdims=True)
    o = jnp.einsum('brnhqk,brnkhd->brnqhd', p, vb) / jnp.swapaxes(l, 3, 4)
    lse = jnp.swapaxes((m + jnp.log(l))[..., 0], 3, 4)
    o = o.reshape(B, dil, Lp, H, Dh)[:, :, :L].transpose(0, 2, 1, 3, 4).reshape(B, S, H, Dh)
    lse = lse.reshape(B, dil, Lp, H)[:, :, :L].transpose(0, 2, 1, 3).reshape(B, S, H)
    return o, lse


def _dilated_attn_cached(q, k, v, kv_buf, window, dil):
    B, T, H, Dh = q.shape
    Lb = kv_buf.shape[2]
    span = window // dil
    kc = jnp.concatenate([kv_buf[:, 0], k], axis=1)
    vc = jnp.concatenate([kv_buf[:, 1], v], axis=1)
    idx = Lb + jnp.arange(T)[:, None] - dil * jnp.arange(span + 1)[None, :]
    valid = idx >= 0
    idx = jnp.maximum(idx, 0)
    kg = kc[:, idx]
    vg = vc[:, idx]
    s = jnp.einsum('bthd,btmhd->bthm', q, kg, preferred_element_type=F32) * (Dh ** -0.5)
    s = jnp.where(valid[:, None, :], s, NEG_INF)
    m = jnp.max(s, axis=-1, keepdims=True)
    p = jnp.exp(s - m)
    l = jnp.sum(p, axis=-1, keepdims=True)
    o = jnp.einsum('bthm,btmhd->bthd', p, vg) / l
    lse = (m + jnp.log(l))[..., 0]
    start = kc.shape[1] - Lb
    new_buf = jnp.stack([kc[:, start:], vc[:, start:]], axis=1)
    return o, lse, new_buf


def _gated_delta_rule(q, k, v, beta, g, s0):
    B, T, H, dk = q.shape
    dv = v.shape[-1]
    C = DN_CHUNK
    n = -(-T // C)
    pad = n * C - T

    def chunked(t):
        t = jnp.pad(t, ((0, 0), (0, pad)) + ((0, 0),) * (t.ndim - 2))
        t = t.reshape((B, n, C) + t.shape[2:])
        return t.transpose((1, 0, 3, 2) + tuple(range(4, t.ndim)))

    qc, kc, vc, bc, gc = (chunked(t) for t in (q, k, v, beta, g))
    G = jnp.cumsum(gc, axis=-1)
    causal = jnp.tril(jnp.ones((C, C), dtype=bool))
    decay = jnp.exp(jnp.where(causal, G[..., :, None] - G[..., None, :], -jnp.inf))
    kb = kc * bc[..., None]
    strict = causal & ~jnp.eye(C, dtype=bool)
    a = jnp.where(strict, jnp.einsum('nbhid,nbhjd->nbhij', kb, kc) * decay, 0.0)
    rhs = jnp.concatenate([vc * bc[..., None], kb * jnp.exp(G)[..., None]], axis=-1)
    sol = lax.linalg.triangular_solve(a, rhs, left_side=True, lower=True, unit_diagonal=True)
    u0, w = sol[..., :dv], sol[..., dv:]
    qk = jnp.einsum('nbhid,nbhjd->nbhij', qc, kc) * decay
    q_dec = qc * jnp.exp(G)[..., None]
    g_last = G[..., -1]
    k_dec = kc * jnp.exp(g_last[..., None] - G)[..., None]

    def step(S, xs):
        u0_i, w_i, qk_i, qd_i, kd_i, gl_i = xs
        u = u0_i - jnp.einsum('bhck,bhkv->bhcv', w_i, S)
        o = jnp.einsum('bhck,bhkv->bhcv', qd_i, S) + jnp.einsum('bhij,bhjv->bhiv', qk_i, u)
        S = jnp.exp(gl_i)[..., None, None] * S + jnp.einsum('bhck,bhcv->bhkv', kd_i, u)
        return S, o

    s_fin, o = lax.scan(step, s0, (u0, w, qk, q_dec, k_dec, g_last))
    o = o.transpose(1, 0, 3, 2, 4).reshape(B, n * C, H, dv)[:, :T]
    return o, s_fin


def _gated_deltanet(c_qkv, c_gate, c_beta, c_alpha, sc_buf, s0, dconv_w, a_log, dt_bias, o_norm_g):
    B, T, _ = c_qkv.shape
    y, sc_new = _causal_depthwise_conv(c_qkv, sc_buf, dconv_w)
    y = jax.nn.silu(y)
    q, k, v = jnp.split(y, [DN_QK, 2 * DN_QK], axis=-1)
    q = _l2_norm(q.reshape(B, T, DN_HEADS, DN_DK)) * (DN_DK ** -0.5)
    k = _l2_norm(k.reshape(B, T, DN_HEADS, DN_DK))
    v = v.reshape(B, T, DN_HEADS, DN_DV)
    beta = jax.nn.sigmoid(c_beta)
    g = -jnp.exp(a_log) * jax.nn.softplus(c_alpha + dt_bias)
    o, s_new = _gated_delta_rule(q, k, v, beta, g, s0)
    o = _rms_norm(o, o_norm_g) * jax.nn.silu(c_gate.reshape(B, T, DN_HEADS, DN_DV))
    return o.reshape(B, T, DN_OUT), sc_new, s_new


MERGE_TM = 512
MERGE_TN = 512
assert COL_MLOG % MERGE_TN == 0 and D_MODEL % MERGE_TN == 0


def _merge_body(ya_ref, yb_ref, yc_ref, ga_ref, gb_ref, gc_ref, w_ref, o_ref, lhs_ref):
    @pl.when(pl.program_id(1) == 0)
    def _():
        for b, y_ref in enumerate((ya_ref, yb_ref, yc_ref)):
            lhs_ref[b] = y_ref[...].astype(BF16)

    acc = None
    for b, g_ref in enumerate((ga_ref, gb_ref, gc_ref)):
        part = jax.nn.sigmoid(g_ref[...]) * jnp.dot(
            lhs_ref[b], w_ref[b * CONV_CH:(b + 1) * CONV_CH, :], preferred_element_type=F32)
        acc = part if acc is None else acc + part
    o_ref[...] = acc


def _branch_merge(ya, yb, yc, proj, w_branch):
    assert CONV_CH == ATT_OUT == DN_OUT
    m = ya.shape[0]
    tm = min(MERGE_TM, m)
    y_spec = pl.BlockSpec((tm, CONV_CH), lambda i, j: (i, 0))

    def gate_spec(b):
        blk0 = (COL_MLOG + b * D_MODEL) // MERGE_TN
        return pl.BlockSpec((tm, MERGE_TN), lambda i, j: (i, blk0 + j))

    return pl.pallas_call(
        _merge_body,
        grid=(_cdiv(m, tm), D_MODEL // MERGE_TN),
        in_specs=[y_spec, y_spec, y_spec, gate_spec(0), gate_spec(1), gate_spec(2),
                  pl.BlockSpec((N_BRANCH * CONV_CH, MERGE_TN), lambda i, j: (0, j))],
        out_specs=pl.BlockSpec((tm, MERGE_TN), lambda i, j: (i, j)),
        out_shape=jax.ShapeDtypeStruct((m, D_MODEL), F32),
        scratch_shapes=[pltpu.VMEM((N_BRANCH, tm, CONV_CH), BF16)],
        compiler_params=pltpu.CompilerParams(
            dimension_semantics=("parallel", "arbitrary"), vmem_limit_bytes=V7X_VMEM_LIMIT_BYTES),
        name="branch_merge",
    )(ya, yb, yc, proj, proj, proj, w_branch)


def _prompt_layer(x2, norm_g, w_in_r, conv_w, conv_b, ln_g, ln_b, q_norm_g, k_norm_g,
                  dconv_w, a_log, dt_bias, o_norm_g, w_branch, w_out):
    t = x2.shape[0]
    proj = _proj(x2, w_in_r, gain=norm_g, tm=1024, tn=PROJ_TN)
    ya, conv_new = _conv_a_prompt(proj, conv_w, conv_b, ln_g, ln_b)
    yb, kv0, kv1, kv2 = _attn_prompt(proj, q_norm_g, k_norm_g)
    yc, s_new = _deltanet_prompt(proj, dconv_w, a_log, dt_bias, o_norm_g)
    y = _branch_merge(ya, yb, yc, proj, w_branch)
    out = _proj(y, w_out, resid=x2, tm=1024, tn=512)
    sc_new = proj[t - (DN_SHORT_K - 1):, COL_C_QKV:COL_C_QKV + DN_QKV]
    kv_new = [kv.reshape(1, 2, w, H_PER_GROUP, HEAD_DIM) for kv, (w, _) in zip((kv0, kv1, kv2), ATT_PATTERNS)]
    return out, conv_new[None], kv_new, sc_new[None], s_new[None]


def _sample_layer(x, conv_buf, kv_bufs, sc_buf, s0, norm_g, w_in_r, conv_w, conv_b, ln_g, ln_b,
                  q_norm_g, k_norm_g, dconv_w, a_log, dt_bias, o_norm_g, w_branch, w_out):
    B, T, _ = x.shape
    m = B * T
    x2 = x.reshape(m, D_MODEL)
    proj2 = _proj(x2, w_in_r, gain=norm_g, tm=m, tn=PROJ_TN)
    proj = proj2.reshape(B, T, N_PROJ)

    def seg(col0, width):
        return proj[..., col0:col0 + width]

    a_val, a_glu, a_gate = seg(COL_A_VAL, CONV_CH), seg(COL_A_GLU, CONV_CH), seg(COL_A_GATE, CONV_CH)
    b_q, b_k, b_v = seg(COL_Q, ATT_QKV), seg(COL_K, ATT_QKV), seg(COL_V, ATT_QKV)
    b_gate = seg(COL_B_GATE, ATT_OUT)
    c_qkv, c_gate = seg(COL_C_QKV, DN_QKV), seg(COL_C_GATE, DN_OUT)
    c_beta, c_alpha = seg(COL_BETA, DN_HEADS), seg(COL_ALPHA, DN_HEADS)

    glu = a_val * jax.nn.sigmoid(a_glu)
    ya, conv_new = _causal_depthwise_conv(glu, conv_buf, conv_w)
    ya = jax.nn.silu(_layer_norm(ya + conv_b, ln_g, ln_b)) * jax.nn.silu(a_gate)

    shp = (B, T, N_ATT_GROUPS, H_PER_GROUP, HEAD_DIM)
    q = _rms_norm(b_q.reshape(shp), q_norm_g)
    k = _rms_norm(b_k.reshape(shp), k_norm_g)
    v = b_v.reshape(shp)
    outs, lses, kv_new = [], [], []
    for gi, (window, dil) in enumerate(ATT_PATTERNS):
        o, lse, buf = _dilated_attn_cached(q[:, :, gi], k[:, :, gi], v[:, :, gi], kv_bufs[gi], window, dil)
        kv_new.append(buf)
        outs.append(o)
        lses.append(lse)
    alpha = jax.nn.softmax(jnp.stack(lses), axis=0)
    yb = jnp.einsum('gbth,gbthd->bthd', alpha, jnp.stack(outs)).reshape(B, T, ATT_OUT)
    yb = yb * jax.nn.silu(b_gate)

    yc, sc_new, s_new = _gated_deltanet(c_qkv, c_gate, c_beta, c_alpha, sc_buf, s0,
                                        dconv_w, a_log, dt_bias, o_norm_g)

    y = _branch_merge(ya.reshape(m, CONV_CH), yb.reshape(m, ATT_OUT), yc.reshape(m, DN_OUT), proj2, w_branch)
    out = _proj(y, w_out, resid=x2, tm=m, tn=512).reshape(B, T, D_MODEL)
    return out, conv_new, kv_new, sc_new, s_new


def kernel(x_prompt, x_sample, state_conv, cache_kv_w128, cache_kv_w512, cache_kv_w2048, state_short_conv, state_delta, norm_g, w_in, conv_w, conv_b, ln_g, ln_b, q_norm_g, k_norm_g, dconv_w, a_log, dt_bias, o_norm_g, w_branch, w_out):
    bp, t, _ = x_prompt.shape
    assert bp == 1
    xp, xs = x_prompt.reshape(t, D_MODEL), x_sample
    conv_p, conv_s, sc_p, sc_s, d_p, d_s = [], [], [], [], [], []
    kv_p = [[], [], []]
    kv_s = [[], [], []]
    for l in range(DEPTH):
        wts = (norm_g[l], _prep_w_in(w_in[l]), conv_w[l], conv_b[l], ln_g[l], ln_b[l], q_norm_g[l], k_norm_g[l],
               dconv_w[l], a_log[l], dt_bias[l], o_norm_g[l], w_branch[l].astype(BF16), w_out[l].astype(BF16))
        xp, c_new, kv_new, sc_new, s_new = _prompt_layer(xp, *wts)
        conv_p.append(c_new)
        sc_p.append(sc_new)
        d_p.append(s_new)
        for gi in range(N_ATT_GROUPS):
            kv_p[gi].append(kv_new[gi])
        xs, c_new, kv_new, sc_new, s_new = _sample_layer(
            xs, state_conv[l], (cache_kv_w128[l], cache_kv_w512[l], cache_kv_w2048[l]),
            state_short_conv[l], state_delta[l], *wts)
        conv_s.append(c_new)
        sc_s.append(sc_new)
        d_s.append(s_new)
        for gi in range(N_ATT_GROUPS):
            kv_s[gi].append(kv_new[gi])
    return (xp.reshape(bp, t, D_MODEL), xs,
            jnp.stack(conv_p), jnp.stack(conv_s),
            jnp.stack(kv_p[0]), jnp.stack(kv_s[0]),
            jnp.stack(kv_p[1]), jnp.stack(kv_s[1]),
            jnp.stack(kv_p[2]), jnp.stack(kv_s[2]),
            jnp.stack(sc_p), jnp.stack(sc_s),
            jnp.stack(d_p), jnp.stack(d_s))
```

```python
import functools
import math

import jax
import jax.numpy as jnp
import numpy as np
from jax import lax
from jax.experimental import pallas as pl
from jax.experimental.pallas import tpu as pltpu

D_MODEL = 2048
DEPTH = 2
CONV_CH = 1024
CONV_K = 31
ATT_PATTERNS = ((128, 1), (512, 4), (2048, 16))
N_ATT_GROUPS = 3
H_PER_GROUP = 8
HEAD_DIM = 128
ATT_QKV = N_ATT_GROUPS * H_PER_GROUP * HEAD_DIM
ATT_OUT = H_PER_GROUP * HEAD_DIM
ATT_BLOCK = 128
DN_HEADS = 8
DN_DK = 128
DN_DV = 128
DN_QK = DN_HEADS * DN_DK
DN_QKV = DN_HEADS * (2 * DN_DK + DN_DV)
DN_OUT = DN_HEADS * DN_DV
DN_SHORT_K = 4
DN_CHUNK = 64
N_BRANCH = 3
IN_SPLITS = (CONV_CH, CONV_CH, CONV_CH, ATT_QKV, ATT_QKV, ATT_QKV, ATT_OUT,
             DN_QKV, DN_OUT, DN_HEADS, DN_HEADS, N_BRANCH * D_MODEL)
D_IN = sum(IN_SPLITS)
EPS = 1e-6
NEG_INF = -1e30
F32 = jnp.float32
BF16 = jnp.bfloat16

V7X_VMEM_LIMIT_BYTES = 56 * 1024 * 1024


def _cdiv(a, b):
    return -(-a // b)


def _proj_body(*refs, normalize, residual):
    a_ref, g_ref, w_ref = refs[:3]
    r_ref = refs[3] if residual else None
    o_ref, lhs_ref = refs[-2:]

    @pl.when(pl.program_id(1) == 0)
    def _():
        a = a_ref[...]
        if normalize:
            a = a * lax.rsqrt(jnp.mean(a * a, axis=-1, keepdims=True) + EPS) * g_ref[...]
        lhs_ref[...] = a.astype(BF16)

    acc = jnp.dot(lhs_ref[...], w_ref[...], preferred_element_type=F32)
    if residual:
        acc = acc + r_ref[...]
    o_ref[...] = acc


def _proj(a, w, gain=None, resid=None, *, tm, tn):
    m, k = a.shape
    n = w.shape[1]
    tm = min(tm, m)
    tn = min(tn, n)
    normalize = gain is not None
    if gain is None:
        gain = jnp.ones((k,), F32)
    in_specs = [
        pl.BlockSpec((tm, k), lambda i, j: (i, 0)),
        pl.BlockSpec((1, k), lambda i, j: (0, 0)),
        pl.BlockSpec((k, tn), lambda i, j: (0, j)),
    ]
    args = [a, gain.reshape(1, k), w]
    if resid is not None:
        in_specs.append(pl.BlockSpec((tm, tn), lambda i, j: (i, j)))
        args.append(resid)
    return pl.pallas_call(
        functools.partial(_proj_body, normalize=normalize, residual=resid is not None),
        grid=(_cdiv(m, tm), _cdiv(n, tn)),
        in_specs=in_specs,
        out_specs=pl.BlockSpec((tm, tn), lambda i, j: (i, j)),
        out_shape=jax.ShapeDtypeStruct((m, n), F32),
        scratch_shapes=[pltpu.VMEM((tm, k), BF16)],
        compiler_params=pltpu.CompilerParams(
            dimension_semantics=("parallel", "arbitrary"),
            vmem_limit_bytes=V7X_VMEM_LIMIT_BYTES),
        name="proj",
    )(*args)


LANES = 128
COL_A_VAL = 0
COL_A_GLU = CONV_CH
COL_A_GATE = 2 * CONV_CH
COL_Q = 3 * CONV_CH
COL_K = COL_Q + ATT_QKV
COL_V = COL_K + ATT_QKV
COL_B_GATE = COL_V + ATT_QKV
COL_C_QKV = COL_B_GATE + ATT_OUT
COL_C_GATE = COL_C_QKV + DN_QKV
COL_MLOG = COL_C_GATE + DN_OUT
COL_BETA = COL_MLOG + N_BRANCH * D_MODEL
COL_ALPHA = COL_BETA + DN_HEADS
MXU_WIDTH = 256
N_PROJ = _cdiv(D_IN, MXU_WIDTH) * MXU_WIDTH
PROJ_TN = 768
_SRC_BETA = COL_MLOG


def _prep_w_in(w_in):
    pad = jnp.zeros((w_in.shape[0], N_PROJ - D_IN), w_in.dtype)
    w = jnp.concatenate([w_in[:, :_SRC_BETA], w_in[:, _SRC_BETA + 2 * DN_HEADS:],
                         w_in[:, _SRC_BETA:_SRC_BETA + 2 * DN_HEADS], pad], axis=1)
    return w.astype(BF16)


ATT_SB = ATT_BLOCK * max(d for _, d in ATT_PATTERNS)
ATT_SPAN = ATT_BLOCK
assert all(w // d == ATT_SPAN for w, d in ATT_PATTERNS)


def _rows(start, size, stride):
    return pl.ds(start, size) if stride == 1 else pl.ds(start, size, stride=stride)


def _rms_rows(x, gain):
    return x * lax.rsqrt(jnp.mean(x * x, axis=-1, keepdims=True) + EPS) * gain


def _attn_prompt_body(q0, q1, q2, k0, k1, k2, v0, v1, v2, gate_ref, qg_ref, kg_ref,
                      yb_ref, kv0, kv1, kv2,
                      kr0, kr1, kr2, vr0, vr1, vr2, on0, on1, on2, ln0, ln1, ln2):
    q_refs, k_refs, v_refs = (q0, q1, q2), (k0, k1, k2), (v0, v1, v2)
    kv_refs = (kv0, kv1, kv2)
    kres, vres = (kr0, kr1, kr2), (vr0, vr1, vr2)
    onat, lnat = (on0, on1, on2), (ln0, ln1, ln2)
    i = pl.program_id(1)
    last = pl.num_programs(1) - 1
    qg = qg_ref[...]
    kg = kg_ref[...]
    scale = HEAD_DIM ** -0.5
    qi = lax.broadcasted_iota(jnp.int32, (ATT_BLOCK, 2 * ATT_BLOCK), 0)
    kj = lax.broadcasted_iota(jnp.int32, (ATT_BLOCK, 2 * ATT_BLOCK), 1)
    delta = kj - qi
    band = (delta >= 0) & (delta <= ATT_SPAN)

    for g, (_, d) in enumerate(ATT_PATTERNS):
        nblk = ATT_SB // (ATT_BLOCK * d)
        units = ATT_BLOCK * nblk

        @pl.when(i == 0)
        def _():
            kres[g][:, 0:ATT_BLOCK, :] = jnp.zeros((d, ATT_BLOCK, HEAD_DIM), BF16)
            vres[g][:, 0:ATT_BLOCK, :] = jnp.zeros((d, ATT_BLOCK, HEAD_DIM), BF16)

        @pl.when(i > 0)
        def _():
            kres[g][:, 0:ATT_BLOCK, :] = kres[g][:, units:units + ATT_BLOCK, :]
            vres[g][:, 0:ATT_BLOCK, :] = vres[g][:, units:units + ATT_BLOCK, :]

        for r in range(d):
            for c in range(nblk):
                rows = _rows(c * ATT_BLOCK * d + r, ATT_BLOCK, d)
                dst = slice(ATT_BLOCK + c * ATT_BLOCK, ATT_BLOCK + (c + 1) * ATT_BLOCK)
                kres[g][r, dst, :] = _rms_rows(k_refs[g][rows, :], kg).astype(BF16)
                vres[g][r, dst, :] = v_refs[g][rows, :].astype(BF16)

        for r in range(d):
            for c in range(nblk):
                rows = _rows(c * ATT_BLOCK * d + r, ATT_BLOCK, d)
                q = _rms_rows(q_refs[g][rows, :], qg).astype(BF16)
                kb = kres[g][r, c * ATT_BLOCK:(c + 2) * ATT_BLOCK, :]
                vb = vres[g][r, c * ATT_BLOCK:(c + 2) * ATT_BLOCK, :]
                s = lax.dot_general(q, kb, (((1,), (1,)), ((), ())), preferred_element_type=F32) * scale
                if c == 0:
                    first = jnp.where(i == 0, ATT_BLOCK, 0)
                    mask = band & (kj >= first)
                else:
                    mask = band
                s = jnp.where(mask, s, NEG_INF)
                m = jnp.max(s, axis=-1, keepdims=True)
                p = jnp.exp(s - m)
                l = jnp.sum(p, axis=-1, keepdims=True)
                o = jnp.dot(p.astype(BF16), vb, preferred_element_type=F32) / l
                onat[g][rows, :] = o
                lnat[g][rows, :] = jnp.broadcast_to(m + jnp.log(l), (ATT_BLOCK, HEAD_DIM))

    chunk = 256
    for c in range(ATT_SB // chunk):
        rows = slice(c * chunk, (c + 1) * chunk)
        l0, l1, l2 = lnat[0][rows, :], lnat[1][rows, :], lnat[2][rows, :]
        m = jnp.maximum(jnp.maximum(l0, l1), l2)
        e0, e1, e2 = jnp.exp(l0 - m), jnp.exp(l1 - m), jnp.exp(l2 - m)
        y = (e0 * onat[0][rows, :] + e1 * onat[1][rows, :] + e2 * onat[2][rows, :]) / (e0 + e1 + e2)
        gate = gate_ref[rows, :]
        yb_ref[rows, :] = y * (gate * jax.nn.sigmoid(gate))

    @pl.when(i == last)
    def _():
        for g, (window, _) in enumerate(ATT_PATTERNS):
            tail = slice(ATT_SB - window, ATT_SB)
            kv_refs[g][0, :, :] = _rms_rows(k_refs[g][tail, :], kg)
            kv_refs[g][1, :, :] = v_refs[g][tail, :]


def _attn_prompt(proj, q_norm_g, k_norm_g):
    t = proj.shape[0]
    assert t % ATT_SB == 0
    n_sb = t // ATT_SB

    def col_spec(col0, g):
        blk0 = col0 // LANES + g * H_PER_GROUP
        return pl.BlockSpec((ATT_SB, HEAD_DIM), lambda h, i: (i, blk0 + h))

    in_specs = ([col_spec(COL_Q, g) for g in range(N_ATT_GROUPS)]
                + [col_spec(COL_K, g) for g in range(N_ATT_GROUPS)]
                + [col_spec(COL_V, g) for g in range(N_ATT_GROUPS)]
                + [col_spec(COL_B_GATE, 0),
                   pl.BlockSpec((1, HEAD_DIM), lambda h, i: (0, 0)),
                   pl.BlockSpec((1, HEAD_DIM), lambda h, i: (0, 0))])
    out_specs = [pl.BlockSpec((ATT_SB, HEAD_DIM), lambda h, i: (i, h))]
    out_shape = [jax.ShapeDtypeStruct((t, ATT_OUT), F32)]
    scratch = []
    for window, _ in ATT_PATTERNS:
        out_specs.append(pl.BlockSpec((2, window, HEAD_DIM), lambda h, i: (0, 0, h)))
        out_shape.append(jax.ShapeDtypeStruct((2, window, ATT_OUT), F32))
    for _ in range(2):
        for _, d in ATT_PATTERNS:
            scratch.append(pltpu.VMEM((d, ATT_BLOCK + ATT_SB // d, HEAD_DIM), BF16))
    for _ in range(2 * N_ATT_GROUPS):
        scratch.append(pltpu.VMEM((ATT_SB, HEAD_DIM), F32))
    return pl.pallas_call(
        _attn_prompt_body,
        grid=(H_PER_GROUP, n_sb),
        in_specs=in_specs,
        out_specs=out_specs,
        out_shape=out_shape,
        scratch_shapes=scratch,
        compiler_params=pltpu.CompilerParams(
            dimension_semantics=("parallel", "arbitrary"),
            vmem_limit_bytes=V7X_VMEM_LIMIT_BYTES),
        name="attn_prompt",
    )(*([proj] * 10), q_norm_g.reshape(1, HEAD_DIM), k_norm_g.reshape(1, HEAD_DIM))


CONV_TT = 256
CONV_HALO = 32
CONV_RC = 64
CONV_LC = 256


def _silu(x):
    return x * jax.nn.sigmoid(x)


def _conv_a_body(aval_ref, aglu_ref, agate_ref, w_ref, b_ref, lg_ref, lb_ref,
                 ya_ref, tail_ref, ext_ref, y_ref):
    i = pl.program_id(0)

    @pl.when(i == 0)
    def _():
        ext_ref[0:CONV_HALO, :] = jnp.zeros((CONV_HALO, CONV_CH), F32)

    @pl.when(i > 0)
    def _():
        ext_ref[0:CONV_HALO, :] = ext_ref[CONV_TT:CONV_TT + CONV_HALO, :]

    for rc in range(CONV_TT // CONV_RC):
        rows = slice(rc * CONV_RC, (rc + 1) * CONV_RC)
        ext_ref[CONV_HALO + rc * CONV_RC:CONV_HALO + (rc + 1) * CONV_RC, :] = (
            aval_ref[rows, :] * jax.nn.sigmoid(aglu_ref[rows, :]))

    first = CONV_HALO - (CONV_K - 1)
    for lc in range(CONV_CH // CONV_LC):
        lanes = slice(lc * CONV_LC, (lc + 1) * CONV_LC)
        for rc in range(CONV_TT // CONV_RC):
            acc = jnp.broadcast_to(b_ref[:, lanes], (CONV_RC, CONV_LC))
            for k in range(CONV_K):
                r0 = first + rc * CONV_RC + k
                acc = acc + w_ref[k:k + 1, lanes] * ext_ref[r0:r0 + CONV_RC, lanes]
            y_ref[rc * CONV_RC:(rc + 1) * CONV_RC, lanes] = acc

    for rc in range(CONV_TT // CONV_RC):
        rows = slice(rc * CONV_RC, (rc + 1) * CONV_RC)
        y = y_ref[rows, :]
        yc = y - jnp.mean(y, axis=-1, keepdims=True)
        yn = yc * lax.rsqrt(jnp.mean(yc * yc, axis=-1, keepdims=True) + EPS)
        ya_ref[rows, :] = _silu(yn * lg_ref[...] + lb_ref[...]) * _silu(agate_ref[rows, :])

    @pl.when(i == pl.num_programs(0) - 1)
    def _():
        tail_ref[...] = ext_ref[CONV_HALO + CONV_TT - (CONV_K - 1):CONV_HALO + CONV_TT, :]


def _conv_a_prompt(proj, conv_w, conv_b, ln_g, ln_b):
    t = proj.shape[0]
    assert t % CONV_TT == 0
    nb = CONV_CH // LANES

    def col_spec(col0):
        return pl.BlockSpec((CONV_TT, CONV_CH), lambda i: (i, col0 // CONV_CH))

    def full(shape):
        return pl.BlockSpec(shape, lambda i: (0,) * len(shape))

    del nb
    return pl.pallas_call(
        _conv_a_body,
        grid=(t // CONV_TT,),
        in_specs=[col_spec(COL_A_VAL), col_spec(COL_A_GLU), col_spec(COL_A_GATE),
                  full((CONV_K, CONV_CH)), full((1, CONV_CH)), full((1, CONV_CH)), full((1, CONV_CH))],
        out_specs=[pl.BlockSpec((CONV_TT, CONV_CH), lambda i: (i, 0)), full((CONV_K - 1, CONV_CH))],
        out_shape=[jax.ShapeDtypeStruct((t, CONV_CH), F32),
                   jax.ShapeDtypeStruct((CONV_K - 1, CONV_CH), F32)],
        scratch_shapes=[pltpu.VMEM((CONV_HALO + CONV_TT, CONV_CH), F32),
                        pltpu.VMEM((CONV_TT, CONV_CH), F32)],
        compiler_params=pltpu.CompilerParams(
            dimension_semantics=("arbitrary",), vmem_limit_bytes=V7X_VMEM_LIMIT_BYTES),
        name="conv_a",
    )(proj, proj, proj, conv_w, conv_b.reshape(1, CONV_CH), ln_g.reshape(1, CONV_CH), ln_b.reshape(1, CONV_CH))


DN_TB = 256
DN_NC = DN_TB // DN_CHUNK
DN_HALO = 8
assert DN_DK == DN_DV == LANES


def _split_bf16(x, parts):
    out = []
    for _ in range(parts):
        hi = x.astype(BF16)
        out.append(hi)
        x = x - hi.astype(F32)
    return out


def _dot_nt(a, b):
    return lax.dot_general(a, b, (((1,), (1,)), ((), ())), preferred_element_type=F32)


def _dn_prep_body(xq_ref, xk_ref, xv_ref, hq_ref, hk_ref, hv_ref, ba_ref, wq_ref, wk_ref, wv_ref,
                  pa_ref, pd_ref,
                  u0_ref, w_ref, qd_ref, kdt_ref, qk_ref, e_ref,
                  eq_ref, ek_ref, ev_ref):
    i = pl.program_id(0)
    for ext, halo, x in ((eq_ref, hq_ref, xq_ref), (ek_ref, hk_ref, xk_ref), (ev_ref, hv_ref, xv_ref)):
        ext[0:DN_HALO, :] = jnp.where(i == 0, 0.0, halo[...])
        ext[DN_HALO:DN_HALO + DN_TB, :] = x[...]

    ba = ba_ref[...]
    beta_all = jax.nn.sigmoid(ba)
    z = ba + pd_ref[...]
    g_all = -jnp.exp(pa_ref[...]) * (jnp.maximum(z, 0.0) + jnp.log1p(jnp.exp(-jnp.abs(z))))

    ri = lax.broadcasted_iota(jnp.int32, (DN_CHUNK, DN_CHUNK), 0)
    ci = lax.broadcasted_iota(jnp.int32, (DN_CHUNK, DN_CHUNK), 1)
    causal = ri >= ci
    strict = ri > ci
    tril = causal.astype(BF16)
    eye = (ri == ci).astype(F32)
    first = DN_HALO - (DN_SHORT_K - 1)

    for c in range(DN_NC):
        rows = slice(c * DN_CHUNK, (c + 1) * DN_CHUNK)
        g1, g2, g3 = _split_bf16(g_all[rows, :], 3)
        gsum = (jnp.dot(tril, g1, preferred_element_type=F32) + jnp.dot(tril, g2, preferred_element_type=F32)
                + jnp.dot(tril, g3, preferred_element_type=F32))
        a_all, rhs_all = [], []
        for h in range(DN_HEADS):
            lanes = slice(h * LANES, (h + 1) * LANES)

            def short_conv(ext, wref):
                acc = None
                for k in range(DN_SHORT_K):
                    r0 = first + c * DN_CHUNK + k
                    term = wref[k:k + 1, lanes] * ext[r0:r0 + DN_CHUNK, lanes]
                    acc = term if acc is None else acc + term
                return _silu(acc)

            q = short_conv(eq_ref, wq_ref)
            k = short_conv(ek_ref, wk_ref)
            v = short_conv(ev_ref, wv_ref)
            q = q * lax.rsqrt(jnp.sum(q * q, axis=-1, keepdims=True) + EPS) * (DN_DK ** -0.5)
            k = k * lax.rsqrt(jnp.sum(k * k, axis=-1, keepdims=True) + EPS)
            bcol = jnp.broadcast_to(beta_all[rows, h:h + 1], (DN_CHUNK, LANES))
            gcol = jnp.broadcast_to(gsum[:, DN_HEADS + h:DN_HEADS + h + 1], (DN_CHUNK, LANES))
            grow = jnp.transpose(gcol)[0:DN_CHUNK, :]
            decay = jnp.where(causal, jnp.exp(gcol[:, 0:DN_CHUNK] - grow), 0.0)
            kb = k * bcol
            kbf = k.astype(BF16)
            a = jnp.where(strict, _dot_nt(kb.astype(BF16), kbf) * decay, 0.0)
            qk = _dot_nt(q.astype(BF16), kbf) * decay
            eg = jnp.exp(gcol)
            a_all.append(a)
            rhs_all.append(jnp.concatenate([v * bcol, kb * eg], axis=1).astype(BF16))
            glast = gcol[DN_CHUNK - 1:DN_CHUNK, :]
            qd_ref[rows, lanes] = (q * eg).astype(BF16)
            kdt_ref[c, lanes, :] = jnp.transpose(k * jnp.exp(glast - gcol)).astype(BF16)
            qk_ref[rows, h * LANES:h * LANES + DN_CHUNK] = qk.astype(BF16)
            qk_ref[rows, h * LANES + DN_CHUNK:(h + 1) * LANES] = jnp.zeros((DN_CHUNK, LANES - DN_CHUNK), BF16)
            e_ref[c, h:h + 1, :] = jnp.exp(glast)

        tinv = [eye - a for a in a_all]
        pb = [a.astype(BF16) for a in a_all]
        for _ in range(5):
            pb = [jnp.dot(x, x, preferred_element_type=F32).astype(BF16) for x in pb]
            tinv = [t + jnp.dot(x, t.astype(BF16), preferred_element_type=F32) for x, t in zip(pb, tinv)]
        for h in range(DN_HEADS):
            lanes = slice(h * LANES, (h + 1) * LANES)
            sol = jnp.dot(tinv[h].astype(BF16), rhs_all[h], preferred_element_type=F32)
            u0_ref[rows, lanes] = sol[:, 0:DN_DV]
            w_ref[rows, lanes] = sol[:, DN_DV:].astype(BF16)


def _dn_rec_body(u0_ref, w_ref, qd_ref, kdt_ref, qk_ref, e_ref, gate_ref, og_ref,
                 yc_ref, sfin_ref, s_ref):
    i = pl.program_id(0)

    @pl.when(i == 0)
    def _():
        s_ref[...] = jnp.zeros(s_ref.shape, F32)

    og = og_ref[...]
    for c in range(DN_NC):
        rows = slice(c * DN_CHUNK, (c + 1) * DN_CHUNK)
        heads = range(DN_HEADS)
        lanes = [slice(h * LANES, (h + 1) * LANES) for h in heads]
        s = [s_ref[h] for h in heads]
        sb = [x.astype(BF16) for x in s]
        ub = [(u0_ref[rows, lanes[h]] - jnp.dot(w_ref[rows, lanes[h]], sb[h], preferred_element_type=F32)
               ).astype(BF16) for h in heads]
        for h in heads:
            s_ref[h] = (e_ref[c, h:h + 1, :] * s[h]
                        + jnp.dot(kdt_ref[c, lanes[h], :], ub[h], preferred_element_type=F32))
        for h in heads:
            o = (jnp.dot(qd_ref[rows, lanes[h]], sb[h], preferred_element_type=F32)
                 + jnp.dot(qk_ref[rows, h * LANES:h * LANES + DN_CHUNK], ub[h], preferred_element_type=F32))
            yc_ref[rows, lanes[h]] = _rms_rows(o, og) * _silu(gate_ref[rows, lanes[h]])

    @pl.when(i == pl.num_programs(0) - 1)
    def _():
        sfin_ref[...] = s_ref[...]


def _deltanet_prompt(proj, dconv_w, a_log, dt_bias, o_norm_g):
    t = proj.shape[0]
    assert t % DN_TB == 0
    nt = t // DN_TB
    qkv_blk = COL_C_QKV // DN_QK
    halo_per_tb = DN_TB // DN_HALO

    def x_spec(j):
        return pl.BlockSpec((DN_TB, DN_QK), lambda i: (i, qkv_blk + j))

    def halo_spec(j):
        return pl.BlockSpec((DN_HALO, DN_QK), lambda i: (jnp.maximum(i * halo_per_tb - 1, 0), qkv_blk + j))

    def w_spec(j):
        return pl.BlockSpec((DN_SHORT_K, DN_QK), lambda i: (0, j))

    row = pl.BlockSpec((1, LANES), lambda i: (0, 0))
    pad_a = jnp.zeros((1, LANES), F32).at[0, DN_HEADS:2 * DN_HEADS].set(a_log)
    pad_d = jnp.zeros((1, LANES), F32).at[0, DN_HEADS:2 * DN_HEADS].set(dt_bias)
    wide = pl.BlockSpec((DN_TB, DN_OUT), lambda i: (i, 0))
    kdt_spec = pl.BlockSpec((DN_NC, DN_OUT, DN_CHUNK), lambda i: (i, 0, 0))
    e_spec = pl.BlockSpec((DN_NC, DN_HEADS, LANES), lambda i: (i, 0, 0))
    u0, w, qd, kdt, qk, e = pl.pallas_call(
        _dn_prep_body,
        grid=(nt,),
        in_specs=[x_spec(0), x_spec(1), x_spec(2), halo_spec(0), halo_spec(1), halo_spec(2),
                  pl.BlockSpec((DN_TB, LANES), lambda i: (i, COL_BETA // LANES)),
                  w_spec(0), w_spec(1), w_spec(2), row, row],
        out_specs=[wide, wide, wide, kdt_spec, wide, e_spec],
        out_shape=[jax.ShapeDtypeStruct((t, DN_OUT), F32),
                   jax.ShapeDtypeStruct((t, DN_OUT), BF16),
                   jax.ShapeDtypeStruct((t, DN_OUT), BF16),
                   jax.ShapeDtypeStruct((t // DN_CHUNK, DN_OUT, DN_CHUNK), BF16),
                   jax.ShapeDtypeStruct((t, DN_OUT), BF16),
                   jax.ShapeDtypeStruct((t // DN_CHUNK, DN_HEADS, LANES), F32)],
        scratch_shapes=[pltpu.VMEM((DN_HALO + DN_TB, DN_QK), F32) for _ in range(3)],
        compiler_params=pltpu.CompilerParams(
            dimension_semantics=("parallel",), vmem_limit_bytes=V7X_VMEM_LIMIT_BYTES),
        name="dn_prep",
    )(proj, proj, proj, proj, proj, proj, proj, dconv_w, dconv_w, dconv_w, pad_a, pad_d)

    state = pl.BlockSpec((DN_HEADS, DN_DK, DN_DV), lambda i: (0, 0, 0))
    yc, s_fin = pl.pallas_call(
        _dn_rec_body,
        grid=(nt,),
        in_specs=[wide, wide, wide, kdt_spec, wide, e_spec,
                  pl.BlockSpec((DN_TB, DN_OUT), lambda i: (i, COL_C_GATE // DN_OUT)), row],
        out_specs=[wide, state],
        out_shape=[jax.ShapeDtypeStruct((t, DN_OUT), F32),
                   jax.ShapeDtypeStruct((DN_HEADS, DN_DK, DN_DV), F32)],
        scratch_shapes=[pltpu.VMEM((DN_HEADS, DN_DK, DN_DV), F32)],
        compiler_params=pltpu.CompilerParams(
            dimension_semantics=("arbitrary",), vmem_limit_bytes=V7X_VMEM_LIMIT_BYTES),
        name="dn_rec",
    )(u0, w, qd, kdt, qk, e, proj, o_norm_g.reshape(1, LANES))
    return yc, s_fin


BLK = CONV_CH // LANES
assert CONV_CH == ATT_OUT == DN_OUT == DN_QK == BLK * LANES and H_PER_GROUP == DN_HEADS == BLK


def _sample_mix_body(a_ref, p3_ref, ba_ref, cst_ref, cw_ref, cb_ref, lg_ref, lb_ref,
                     kv0_ref, kv1_ref, kv2_ref, qg_ref, kg_ref,
                     qkt_ref, sct_ref, wt_ref, scv_ref, wv_ref, s0_ref, al_ref, dtb_ref, og_ref,
                     ya_ref, glu_ref, yb_ref, yc_ref, n0_ref, n1_ref, n2_ref, s_ref):
    def seg(col0):
        return p3_ref[0, col0 // LANES:col0 // LANES + BLK, :]

    glu = a_ref[0, :, COL_A_VAL:COL_A_VAL + CONV_CH] * jax.nn.sigmoid(a_ref[0, :, COL_A_GLU:COL_A_GLU + CONV_CH])
    y = (jnp.sum(cst_ref[0] * cw_ref[0:CONV_K - 1, :], axis=0, keepdims=True)
         + glu * cw_ref[CONV_K - 1:CONV_K, :] + cb_ref[...])
    yc = y - jnp.mean(y, axis=-1, keepdims=True)
    yn = yc * lax.rsqrt(jnp.mean(yc * yc, axis=-1, keepdims=True) + EPS)
    ya_ref[0] = _silu(yn * lg_ref[...] + lb_ref[...]) * _silu(a_ref[0, :, COL_A_GATE:COL_A_GATE + CONV_CH])
    glu_ref[0] = glu

    scale = HEAD_DIM ** -0.5
    outs, lses = [], []
    for g, (kv_ref, new_ref) in enumerate(((kv0_ref, n0_ref), (kv1_ref, n1_ref), (kv2_ref, n2_ref))):
        q = _rms_rows(seg(COL_Q + g * ATT_OUT), qg_ref[...])
        k_new = _rms_rows(seg(COL_K + g * ATT_OUT), kg_ref[...])
        v_new = seg(COL_V + g * ATT_OUT)
        new_ref[0, 0] = k_new
        new_ref[0, 1] = v_new
        kc = kv_ref[0, 0]
        vc = kv_ref[0, 1]
        s = jnp.sum(kc * q[None], axis=-1, keepdims=True) * scale
        s_new = jnp.sum(k_new * q, axis=-1, keepdims=True) * scale
        m = jnp.maximum(jnp.max(s, axis=0), s_new)
        p = jnp.exp(s - m[None])
        p_new = jnp.exp(s_new - m)
        l = jnp.sum(p, axis=0) + p_new
        outs.append((jnp.sum(p * vc, axis=0) + p_new * v_new) / l)
        lses.append(m + jnp.log(l))
    m = jnp.maximum(jnp.maximum(lses[0], lses[1]), lses[2])
    e = [jnp.exp(x - m) for x in lses]
    yb = (e[0] * outs[0] + e[1] * outs[1] + e[2] * outs[2]) / (e[0] + e[1] + e[2])
    yb_ref[0] = yb * _silu(seg(COL_B_GATE))

    first = DN_SHORT_K - 1
    qk = sct_ref[0, 0] * wt_ref[0]
    for j in range(1, first):
        qk = qk + sct_ref[0, j] * wt_ref[j]
    qk = _silu(qk + qkt_ref[0] * wt_ref[first])
    qk = qk * lax.rsqrt(jnp.sum(qk * qk, axis=0, keepdims=True) + EPS)
    v = scv_ref[0, 0] * wv_ref[0]
    for j in range(1, first):
        v = v + scv_ref[0, j] * wv_ref[j]
    v = _silu(v + seg(COL_C_QKV + 2 * DN_QK) * wv_ref[first])
    hi = lax.broadcasted_iota(jnp.int32, (DN_HEADS, LANES), 0)
    li = lax.broadcasted_iota(jnp.int32, (DN_HEADS, LANES), 1)
    ba = jnp.broadcast_to(ba_ref[0], (DN_HEADS, LANES))
    beta = jax.nn.sigmoid(jnp.sum(jnp.where(li == hi, ba, 0.0), axis=-1, keepdims=True))
    z = jnp.sum(jnp.where(li == hi + DN_HEADS, ba, 0.0), axis=-1, keepdims=True) + dtb_ref[...]
    gdec = jnp.exp(-jnp.exp(al_ref[...]) * (jnp.maximum(z, 0.0) + jnp.log1p(jnp.exp(-jnp.abs(z)))))
    gate = seg(COL_C_GATE)
    for h in range(DN_HEADS):
        qc = qk[:, h:h + 1] * (DN_DK ** -0.5)
        kc = qk[:, DN_HEADS + h:DN_HEADS + h + 1]
        eg = gdec[h:h + 1, :]
        s0 = s0_ref[0, h]
        u = beta[h:h + 1, :] * (v[h:h + 1, :] - eg * jnp.sum(kc * s0, axis=0, keepdims=True))
        s1 = eg * s0 + kc * u
        s_ref[0, h] = s1
        o = jnp.sum(qc * s1, axis=0, keepdims=True)
        yc_ref[0, h:h + 1, :] = _rms_rows(o, og_ref[...]) * _silu(gate[h:h + 1, :])


def _sample_mix(proj2, conv_buf, kv_bufs, sc_buf, s0, conv_w, conv_b, ln_g, ln_b, q_norm_g, k_norm_g,
                dconv_w, a_log, dt_bias, o_norm_g):
    b = proj2.shape[0]
    rows = proj2.reshape(b, 1, N_PROJ)
    p3 = proj2.reshape(b, N_PROJ // LANES, LANES)
    nq = 2 * DN_QK
    qkt = proj2[:, COL_C_QKV:COL_C_QKV + nq].reshape(b, 2 * DN_HEADS, DN_DK).transpose(0, 2, 1)
    sct = sc_buf[:, :, :nq].reshape(b, DN_SHORT_K - 1, 2 * DN_HEADS, DN_DK).transpose(0, 1, 3, 2)
    wt = dconv_w[:, :nq].reshape(DN_SHORT_K, 2 * DN_HEADS, DN_DK).transpose(0, 2, 1)
    scv = sc_buf[:, :, nq:].reshape(b, DN_SHORT_K - 1, DN_HEADS, DN_DV)
    wv = dconv_w[:, nq:].reshape(DN_SHORT_K, DN_HEADS, DN_DV)
    kv_views, kv_specs = [], []
    for kv, (window, d) in zip(kv_bufs, ATT_PATTERNS):
        assert kv.shape == (b, 2, window, H_PER_GROUP, HEAD_DIM) and window == ATT_SPAN * d
        kv_views.append(kv.reshape(b, 2, ATT_SPAN, d, H_PER_GROUP, HEAD_DIM))
        kv_specs.append(pl.BlockSpec((1, 2, ATT_SPAN, None, H_PER_GROUP, HEAD_DIM),
                                     lambda i: (i, 0, 0, 0, 0, 0)))

    def full(shape):
        return pl.BlockSpec(shape, lambda i: (0,) * len(shape))

    def per_b(shape):
        return pl.BlockSpec((1,) + shape, lambda i: (i,) + (0,) * len(shape))

    col = lambda x: x.reshape(-1, 1)
    row = lambda x: x.reshape(1, -1)
    tile_out = jax.ShapeDtypeStruct((b, BLK, LANES), F32)
    new_out = jax.ShapeDtypeStruct((b, 2, H_PER_GROUP, HEAD_DIM), F32)
    outs = pl.pallas_call(
        _sample_mix_body,
        grid=(b,),
        in_specs=[per_b((1, N_PROJ)), per_b((N_PROJ // LANES, LANES)),
                  pl.BlockSpec((1, 1, LANES), lambda i: (i, 0, COL_BETA // LANES)),
                  per_b((CONV_K - 1, CONV_CH)), full((CONV_K, CONV_CH)),
                  full((1, CONV_CH)), full((1, CONV_CH)), full((1, CONV_CH)),
                  *kv_specs, full((1, HEAD_DIM)), full((1, HEAD_DIM)),
                  per_b((DN_DK, 2 * DN_HEADS)), per_b((DN_SHORT_K - 1, DN_DK, 2 * DN_HEADS)),
                  full((DN_SHORT_K, DN_DK, 2 * DN_HEADS)),
                  per_b((DN_SHORT_K - 1, DN_HEADS, DN_DV)), full((DN_SHORT_K, DN_HEADS, DN_DV)),
                  per_b((DN_HEADS, DN_DK, DN_DV)), full((DN_HEADS, 1)), full((DN_HEADS, 1)), full((1, DN_DV))],
        out_specs=[per_b((1, CONV_CH)), per_b((1, CONV_CH)), per_b((BLK, LANES)), per_b((BLK, LANES)),
                   per_b((2, H_PER_GROUP, HEAD_DIM)), per_b((2, H_PER_GROUP, HEAD_DIM)),
                   per_b((2, H_PER_GROUP, HEAD_DIM)), per_b((DN_HEADS, DN_DK, DN_DV))],
        out_shape=[jax.ShapeDtypeStruct((b, 1, CONV_CH), F32), jax.ShapeDtypeStruct((b, 1, CONV_CH), F32),
                   tile_out, tile_out, new_out, new_out, new_out,
                   jax.ShapeDtypeStruct((b, DN_HEADS, DN_DK, DN_DV), F32)],
        compiler_params=pltpu.CompilerParams(
            dimension_semantics=("parallel",), vmem_limit_bytes=V7X_VMEM_LIMIT_BYTES),
        name="sample_mix",
    )(rows, p3, rows, conv_buf, conv_w, row(conv_b), row(ln_g), row(ln_b),
      *kv_views, row(q_norm_g), row(k_norm_g),
      qkt, sct, wt, scv, wv, s0, col(a_log), col(dt_bias), row(o_norm_g))
    ya, glu, yb, yc, n0, n1, n2, s_new = outs
    return (ya.reshape(b, CONV_CH), glu, yb.reshape(b, ATT_OUT), yc.reshape(b, DN_OUT), (n0, n1, n2), s_new)


SHIFT_ROWS = 512


def _cache_shift_body(cur_ref, nxt_ref, new_ref, o_ref):
    r = cur_ref.shape[3]
    o_ref[0, 0, 0, 0:r - 1] = cur_ref[0, 0, 0, 1:r]
    is_last = pl.program_id(3) == pl.num_programs(3) - 1
    o_ref[0, 0, 0, r - 1] = jnp.where(is_last, new_ref[0, 0, 0], nxt_ref[0, 0, 0, 0])


def _cache_shift(cache, new_rows):
    nl, b, _, lb, h, dh = cache.shape
    r = min(SHIFT_ROWS, lb)
    assert lb % r == 0
    nchunk = lb // r
    return pl.pallas_call(
        _cache_shift_body,
        grid=(nl, b, 2, nchunk),
        in_specs=[pl.BlockSpec((1, 1, 1, r, h, dh), lambda l, i, k, c: (l, i, k, c, 0, 0)),
                  pl.BlockSpec((1, 1, 1, 1, h, dh),
                               lambda l, i, k, c: (l, i, k, jnp.minimum((c + 1) * r, lb - 1), 0, 0)),
                  pl.BlockSpec((1, 1, 1, h, dh), lambda l, i, k, c: (l, i, k, 0, 0))],
        out_specs=pl.BlockSpec((1, 1, 1, r, h, dh), lambda l, i, k, c: (l, i, k, c, 0, 0)),
        out_shape=jax.ShapeDtypeStruct(cache.shape, cache.dtype),
        compiler_params=pltpu.CompilerParams(
            dimension_semantics=("parallel", "parallel", "parallel", "arbitrary"),
            vmem_limit_bytes=V7X_VMEM_LIMIT_BYTES),
        name="cache_shift",
    )(cache, cache, new_rows)


def _rms_norm(x, g):
    y = x * lax.rsqrt(jnp.mean(x * x, axis=-1, keepdims=True) + EPS)
    return y * g


def _layer_norm(x, g, b):
    xc = x - jnp.mean(x, axis=-1, keepdims=True)
    y = xc * lax.rsqrt(jnp.mean(xc * xc, axis=-1, keepdims=True) + EPS)
    return y * g + b


def _l2_norm(x):
    return x * lax.rsqrt(jnp.sum(x * x, axis=-1, keepdims=True) + EPS)


def _causal_depthwise_conv(x, buf, w):
    xc = jnp.concatenate([buf, x], axis=1)
    y = lax.conv_general_dilated(xc, w[:, None, :], window_strides=(1,), padding='VALID',
                                 dimension_numbers=('NWC', 'WIO', 'NWC'), feature_group_count=x.shape[-1])
    return y, xc[:, xc.shape[1] - (w.shape[0] - 1):]


def _dilated_attn_prompt(q, k, v, window, dil):
    B, S, H, Dh = q.shape
    span = window // dil
    L = S // dil
    nb = -(-L // ATT_BLOCK)
    Lp = nb * ATT_BLOCK

    def by_residue(t):
        t = t.reshape(B, L, dil, H, Dh).transpose(0, 2, 1, 3, 4)
        t = jnp.pad(t, ((0, 0), (0, 0), (0, Lp - L), (0, 0), (0, 0)))
        return t.reshape(B, dil, nb, ATT_BLOCK, H, Dh)

    def with_prev_block(t):
        prev = jnp.pad(t, ((0, 0), (0, 0), (1, 0), (0, 0), (0, 0), (0, 0)))[:, :, :nb]
        return jnp.concatenate([prev, t], axis=3)

    qb = by_residue(q)
    kb = with_prev_block(by_residue(k))
    vb = with_prev_block(by_residue(v))
    s = jnp.einsum('brnqhd,brnkhd->brnhqk', qb, kb, preferred_element_type=F32) * (Dh ** -0.5)
    qi = jnp.arange(ATT_BLOCK)[:, None]
    kj = jnp.arange(2 * ATT_BLOCK)[None, :]
    dist = ATT_BLOCK + qi - kj
    kpos = (jnp.arange(nb)[:, None, None] - 1) * ATT_BLOCK + kj
    mask = (dist >= 0) & (dist <= span) & (kpos >= 0)
    s = jnp.where(mask[:, None], s, NEG_INF)
    m = jnp.max(s, axis=-1, keepdims=True)
    p = jnp.exp(s - m)
    l = jnp.sum(p, axis=-1, keepdims=True)
    o = jnp.einsum('brnhqk,brnkhd->brnqhd', p, vb) / jnp.swapaxes(l, 3, 4)
    lse = jnp.swapaxes((m + jnp.log(l))[..., 0], 3, 4)
    o = o.reshape(B, dil, Lp, H, Dh)[:, :, :L].transpose(0, 2, 1, 3, 4).reshape(B, S, H, Dh)
    lse = lse.reshape(B, dil, Lp, H)[:, :, :L].transpose(0, 2, 1, 3).reshape(B, S, H)
    return o, lse


def _dilated_attn_cached(q, k, v, kv_buf, window, dil):
    B, T, H, Dh = q.shape
    Lb = kv_buf.shape[2]
    span = window // dil
    kc = jnp.concatenate([kv_buf[:, 0], k], axis=1)
    vc = jnp.concatenate([kv_buf[:, 1], v], axis=1)
    idx = Lb + jnp.arange(T)[:, None] - dil * jnp.arange(span + 1)[None, :]
    valid = idx >= 0
    idx = jnp.maximum(idx, 0)
    kg = kc[:, idx]
    vg = vc[:, idx]
    s = jnp.einsum('bthd,btmhd->bthm', q, kg, preferred_element_type=F32) * (Dh ** -0.5)
    s = jnp.where(valid[:, None, :], s, NEG_INF)
    m = jnp.max(s, axis=-1, keepdims=True)
    p = jnp.exp(s - m)
    l = jnp.sum(p, axis=-1, keepdims=True)
    o = jnp.einsum('bthm,btmhd->bthd', p, vg) / l
    lse = (m + jnp.log(l))[..., 0]
    start = kc.shape[1] - Lb
    new_buf = jnp.stack([kc[:, start:], vc[:, start:]], axis=1)
    return o, lse, new_buf


def _gated_delta_rule(q, k, v, beta, g, s0):
    B, T, H, dk = q.shape
    dv = v.shape[-1]
    C = DN_CHUNK
    n = -(-T // C)
    pad = n * C - T

    def chunked(t):
        t = jnp.pad(t, ((0, 0), (0, pad)) + ((0, 0),) * (t.ndim - 2))
        t = t.reshape((B, n, C) + t.shape[2:])
        return t.transpose((1, 0, 3, 2) + tuple(range(4, t.ndim)))

    qc, kc, vc, bc, gc = (chunked(t) for t in (q, k, v, beta, g))
    G = jnp.cumsum(gc, axis=-1)
    causal = jnp.tril(jnp.ones((C, C), dtype=bool))
    decay = jnp.exp(jnp.where(causal, G[..., :, None] - G[..., None, :], -jnp.inf))
    kb = kc * bc[..., None]
    strict = causal & ~jnp.eye(C, dtype=bool)
    a = jnp.where(strict, jnp.einsum('nbhid,nbhjd->nbhij', kb, kc) * decay, 0.0)
    rhs = jnp.concatenate([vc * bc[..., None], kb * jnp.exp(G)[..., None]], axis=-1)
    sol = lax.linalg.triangular_solve(a, rhs, left_side=True, lower=True, unit_diagonal=True)
    u0, w = sol[..., :dv], sol[..., dv:]
    qk = jnp.einsum('nbhid,nbhjd->nbhij', qc, kc) * decay
    q_dec = qc * jnp.exp(G)[..., None]
    g_last = G[..., -1]
    k_dec = kc * jnp.exp(g_last[..., None] - G)[..., None]

    def step(S, xs):
        u0_i, w_i, qk_i, qd_i, kd_i, gl_i = xs
        u = u0_i - jnp.einsum('bhck,bhkv->bhcv', w_i, S)
        o = jnp.einsum('bhck,bhkv->bhcv', qd_i, S) + jnp.einsum('bhij,bhjv->bhiv', qk_i, u)
        S = jnp.exp(gl_i)[..., None, None] * S + jnp.einsum('bhck,bhcv->bhkv', kd_i, u)
        return S, o

    s_fin, o = lax.scan(step, s0, (u0, w, qk, q_dec, k_dec, g_last))
    o = o.transpose(1, 0, 3, 2, 4).reshape(B, n * C, H, dv)[:, :T]
    return o, s_fin


def _gated_deltanet(c_qkv, c_gate, c_beta, c_alpha, sc_buf, s0, dconv_w, a_log, dt_bias, o_norm_g):
    B, T, _ = c_qkv.shape
    y, sc_new = _causal_depthwise_conv(c_qkv, sc_buf, dconv_w)
    y = jax.nn.silu(y)
    q, k, v = jnp.split(y, [DN_QK, 2 * DN_QK], axis=-1)
    q = _l2_norm(q.reshape(B, T, DN_HEADS, DN_DK)) * (DN_DK ** -0.5)
    k = _l2_norm(k.reshape(B, T, DN_HEADS, DN_DK))
    v = v.reshape(B, T, DN_HEADS, DN_DV)
    beta = jax.nn.sigmoid(c_beta)
    g = -jnp.exp(a_log) * jax.nn.softplus(c_alpha + dt_bias)
    o, s_new = _gated_delta_rule(q, k, v, beta, g, s0)
    o = _rms_norm(o, o_norm_g) * jax.nn.silu(c_gate.reshape(B, T, DN_HEADS, DN_DV))
    return o.reshape(B, T, DN_OUT), sc_new, s_new


MERGE_TM = 512
MERGE_TN = 512
assert COL_MLOG % MERGE_TN == 0 and D_MODEL % MERGE_TN == 0


def _merge_body(ya_ref, yb_ref, yc_ref, ga_ref, gb_ref, gc_ref, w_ref, o_ref, lhs_ref):
    @pl.when(pl.program_id(1) == 0)
    def _():
        for b, y_ref in enumerate((ya_ref, yb_ref, yc_ref)):
            lhs_ref[b] = y_ref[...].astype(BF16)

    acc = None
    for b, g_ref in enumerate((ga_ref, gb_ref, gc_ref)):
        part = jax.nn.sigmoid(g_ref[...]) * jnp.dot(
            lhs_ref[b], w_ref[b * CONV_CH:(b + 1) * CONV_CH, :], preferred_element_type=F32)
        acc = part if acc is None else acc + part
    o_ref[...] = acc


def _branch_merge(ya, yb, yc, proj, w_branch):
    assert CONV_CH == ATT_OUT == DN_OUT
    m = ya.shape[0]
    tm = min(MERGE_TM, m)
    y_spec = pl.BlockSpec((tm, CONV_CH), lambda i, j: (i, 0))

    def gate_spec(b):
        blk0 = (COL_MLOG + b * D_MODEL) // MERGE_TN
        return pl.BlockSpec((tm, MERGE_TN), lambda i, j: (i, blk0 + j))

    return pl.pallas_call(
        _merge_body,
        grid=(_cdiv(m, tm), D_MODEL // MERGE_TN),
        in_specs=[y_spec, y_spec, y_spec, gate_spec(0), gate_spec(1), gate_spec(2),
                  pl.BlockSpec((N_BRANCH * CONV_CH, MERGE_TN), lambda i, j: (0, j))],
        out_specs=pl.BlockSpec((tm, MERGE_TN), lambda i, j: (i, j)),
        out_shape=jax.ShapeDtypeStruct((m, D_MODEL), F32),
        scratch_shapes=[pltpu.VMEM((N_BRANCH, tm, CONV_CH), BF16)],
        compiler_params=pltpu.CompilerParams(
            dimension_semantics=("parallel", "arbitrary"), vmem_limit_bytes=V7X_VMEM_LIMIT_BYTES),
        name="branch_merge",
    )(ya, yb, yc, proj, proj, proj, w_branch)


def _prompt_layer(x2, norm_g, w_in_r, conv_w, conv_b, ln_g, ln_b, q_norm_g, k_norm_g,
                  dconv_w, a_log, dt_bias, o_norm_g, w_branch, w_out):
    t = x2.shape[0]
    proj = _proj(x2, w_in_r, gain=norm_g, tm=1024, tn=PROJ_TN)
    ya, conv_new = _conv_a_prompt(proj, conv_w, conv_b, ln_g, ln_b)
    yb, kv0, kv1, kv2 = _attn_prompt(proj, q_norm_g, k_norm_g)
    yc, s_new = _deltanet_prompt(proj, dconv_w, a_log, dt_bias, o_norm_g)
    y = _branch_merge(ya, yb, yc, proj, w_branch)
    out = _proj(y, w_out, resid=x2, tm=1024, tn=512)
    sc_new = proj[t - (DN_SHORT_K - 1):, COL_C_QKV:COL_C_QKV + DN_QKV]
    kv_new = [kv.reshape(1, 2, w, H_PER_GROUP, HEAD_DIM) for kv, (w, _) in zip((kv0, kv1, kv2), ATT_PATTERNS)]
    return out, conv_new[None], kv_new, sc_new[None], s_new[None]


def _sample_layer(x, conv_buf, kv_bufs, sc_buf, s0, norm_g, w_in_r, conv_w, conv_b, ln_g, ln_b,
                  q_norm_g, k_norm_g, dconv_w, a_log, dt_bias, o_norm_g, w_branch, w_out):
    B, T, _ = x.shape
    assert T == 1
    x2 = x.reshape(B, D_MODEL)
    proj2 = _proj(x2, w_in_r, gain=norm_g, tm=B, tn=PROJ_TN)
    ya, glu, yb, yc, kv_new, s_new = _sample_mix(proj2, conv_buf, kv_bufs, sc_buf, s0, conv_w, conv_b, ln_g, ln_b,
                                                 q_norm_g, k_norm_g, dconv_w, a_log, dt_bias, o_norm_g)
    y = _branch_merge(ya, yb, yc, proj2, w_branch)
    out = _proj(y, w_out, resid=x2, tm=B, tn=512).reshape(B, T, D_MODEL)
    conv_new = jnp.concatenate([conv_buf[:, 1:], glu], axis=1)
    sc_new = jnp.concatenate([sc_buf[:, 1:], proj2[:, None, COL_C_QKV:COL_C_QKV + DN_QKV]], axis=1)
    return out, conv_new, kv_new, sc_new, s_new


def kernel(x_prompt, x_sample, state_conv, cache_kv_w128, cache_kv_w512, cache_kv_w2048, state_short_conv, state_delta, norm_g, w_in, conv_w, conv_b, ln_g, ln_b, q_norm_g, k_norm_g, dconv_w, a_log, dt_bias, o_norm_g, w_branch, w_out):
    bp, t, _ = x_prompt.shape
    assert bp == 1
    xp, xs = x_prompt.reshape(t, D_MODEL), x_sample
    conv_p, conv_s, sc_p, sc_s, d_p, d_s = [], [], [], [], [], []
    kv_p = [[], [], []]
    kv_s = [[], [], []]
    for l in range(DEPTH):
        wts = (norm_g[l], _prep_w_in(w_in[l]), conv_w[l], conv_b[l], ln_g[l], ln_b[l], q_norm_g[l], k_norm_g[l],
               dconv_w[l], a_log[l], dt_bias[l], o_norm_g[l], w_branch[l].astype(BF16), w_out[l].astype(BF16))
        xp, c_new, kv_new, sc_new, s_new = _prompt_layer(xp, *wts)
        conv_p.append(c_new)
        sc_p.append(sc_new)
        d_p.append(s_new)
        for gi in range(N_ATT_GROUPS):
            kv_p[gi].append(kv_new[gi])
        xs, c_new, kv_new, sc_new, s_new = _sample_layer(
            xs, state_conv[l], (cache_kv_w128[l], cache_kv_w512[l], cache_kv_w2048[l]),
            state_short_conv[l], state_delta[l], *wts)
        conv_s.append(c_new)
        sc_s.append(sc_new)
        d_s.append(s_new)
        for gi in range(N_ATT_GROUPS):
            kv_s[gi].append(kv_new[gi])
    caches = (cache_kv_w128, cache_kv_w512, cache_kv_w2048)
    kv_s = [[_cache_shift(caches[gi], jnp.stack(kv_s[gi]))] for gi in range(N_ATT_GROUPS)]
    return (xp.reshape(bp, t, D_MODEL), xs,
            jnp.stack(conv_p), jnp.stack(conv_s),
            jnp.stack(kv_p[0]), kv_s[0][0],
            jnp.stack(kv_p[1]), kv_s[1][0],
            jnp.stack(kv_p[2]), kv_s[2][0],
            jnp.stack(sc_p), jnp.stack(sc_s),
            jnp.stack(d_p), jnp.stack(d_s))
```

```python
import functools
import math

import jax
import jax.numpy as jnp
import numpy as np
from jax import lax
from jax.experimental import pallas as pl
from jax.experimental.pallas import tpu as pltpu

D_MODEL = 2048
DEPTH = 2
CONV_CH = 1024
CONV_K = 31
ATT_PATTERNS = ((128, 1), (512, 4), (2048, 16))
N_ATT_GROUPS = 3
H_PER_GROUP = 8
HEAD_DIM = 128
ATT_QKV = N_ATT_GROUPS * H_PER_GROUP * HEAD_DIM
ATT_OUT = H_PER_GROUP * HEAD_DIM
ATT_BLOCK = 128
DN_HEADS = 8
DN_DK = 128
DN_DV = 128
DN_QK = DN_HEADS * DN_DK
DN_QKV = DN_HEADS * (2 * DN_DK + DN_DV)
DN_OUT = DN_HEADS * DN_DV
DN_SHORT_K = 4
DN_CHUNK = 64
N_BRANCH = 3
IN_SPLITS = (CONV_CH, CONV_CH, CONV_CH, ATT_QKV, ATT_QKV, ATT_QKV, ATT_OUT,
             DN_QKV, DN_OUT, DN_HEADS, DN_HEADS, N_BRANCH * D_MODEL)
D_IN = sum(IN_SPLITS)
EPS = 1e-6
NEG_INF = -1e30
F32 = jnp.float32
BF16 = jnp.bfloat16

V7X_VMEM_LIMIT_BYTES = 56 * 1024 * 1024


def _cdiv(a, b):
    return -(-a // b)


def _proj_body(*refs, normalize, residual):
    a_ref, g_ref, w_ref = refs[:3]
    r_ref = refs[3] if residual else None
    o_ref, lhs_ref = refs[-2:]

    @pl.when(pl.program_id(1) == 0)
    def _():
        a = a_ref[...]
        if normalize:
            a = a * lax.rsqrt(jnp.mean(a * a, axis=-1, keepdims=True) + EPS) * g_ref[...]
        lhs_ref[...] = a.astype(BF16)

    acc = jnp.dot(lhs_ref[...], w_ref[...], preferred_element_type=F32)
    if residual:
        acc = acc + r_ref[...]
    o_ref[...] = acc


def _proj(a, w, gain=None, resid=None, *, tm, tn):
    m, k = a.shape
    n = w.shape[1]
    tm = min(tm, m)
    tn = min(tn, n)
    normalize = gain is not None
    if gain is None:
        gain = jnp.ones((k,), F32)
    in_specs = [
        pl.BlockSpec((tm, k), lambda i, j: (i, 0)),
        pl.BlockSpec((1, k), lambda i, j: (0, 0)),
        pl.BlockSpec((k, tn), lambda i, j: (0, j)),
    ]
    args = [a, gain.reshape(1, k), w]
    if resid is not None:
        in_specs.append(pl.BlockSpec((tm, tn), lambda i, j: (i, j)))
        args.append(resid)
    return pl.pallas_call(
        functools.partial(_proj_body, normalize=normalize, residual=resid is not None),
        grid=(_cdiv(m, tm), _cdiv(n, tn)),
        in_specs=in_specs,
        out_specs=pl.BlockSpec((tm, tn), lambda i, j: (i, j)),
        out_shape=jax.ShapeDtypeStruct((m, n), F32),
        scratch_shapes=[pltpu.VMEM((tm, k), BF16)],
        compiler_params=pltpu.CompilerParams(
            dimension_semantics=("parallel", "arbitrary"),
            vmem_limit_bytes=V7X_VMEM_LIMIT_BYTES),
        name="proj",
    )(*args)


COPY_PARTS = 3
N_COPY_SEMS = 4


def _shift_copies(layer, caches, outs, sem, pair, part):
    b, kv = pair // 2, pair % 2
    w0, w1, w2 = (c.shape[3] for c in caches)
    half = w2 // 2

    def mk(g, src0, rows, dst0, si):
        return pltpu.make_async_copy(caches[g].at[layer, b, kv, pl.ds(src0, rows)],
                                     outs[g].at[layer, b, kv, pl.ds(dst0, rows)], sem.at[si])

    if part == 0:
        return [mk(2, 1, half, 0, 0)]
    if part == 1:
        return [mk(2, 1 + half, w2 - 1 - half, half, 1)]
    return [mk(1, 1, w1 - 1, 0, 2), mk(0, 1, w0 - 1, 0, 3)]


def _proj_shift_body(*refs, layer, n_slabs):
    a_ref, g_ref, w_ref = refs[:3]
    caches = refs[3:6]
    o_ref = refs[-6]
    outs = refs[-5:-2]
    lhs_ref, sem = refs[-2:]
    step = pl.program_id(0) * pl.num_programs(1) + pl.program_id(1)

    for part in range(COPY_PARTS):
        @pl.when((step < n_slabs) & (step % COPY_PARTS == part))
        def _():
            for c in _shift_copies(layer, caches, outs, sem, step // COPY_PARTS, part):
                c.start()

    @pl.when(pl.program_id(1) == 0)
    def _():
        a = a_ref[...]
        lhs_ref[...] = (a * lax.rsqrt(jnp.mean(a * a, axis=-1, keepdims=True) + EPS) * g_ref[...]).astype(BF16)

    o_ref[...] = jnp.dot(lhs_ref[...], w_ref[...], preferred_element_type=F32)

    prev = step - 1
    for part in range(COPY_PARTS):
        @pl.when((step >= 1) & (prev < n_slabs) & (prev % COPY_PARTS == part))
        def _():
            for c in _shift_copies(layer, caches, outs, sem, prev // COPY_PARTS, part):
                c.wait()


def _proj_shift(a, w, gain, layer, caches, bufs, *, tm, tn):
    m, k = a.shape
    n = w.shape[1]
    assert m % tm == 0 and n % tn == 0
    grid = (m // tm, n // tn)
    nb = caches[0].shape[1]
    n_slabs = nb * 2 * COPY_PARTS
    assert n_slabs < grid[0] * grid[1]
    any_spec = pl.BlockSpec(memory_space=pl.ANY)
    in_specs = [pl.BlockSpec((tm, k), lambda i, j: (i, 0)),
                pl.BlockSpec((1, k), lambda i, j: (0, 0)),
                pl.BlockSpec((k, tn), lambda i, j: (0, j)),
                any_spec, any_spec, any_spec]
    args = [a, gain.reshape(1, k), w, *caches]
    aliases = {}
    if bufs is not None:
        in_specs += [any_spec] * 3
        aliases = {len(args) + g: 1 + g for g in range(3)}
        args += list(bufs)
    outs = pl.pallas_call(
        functools.partial(_proj_shift_body, layer=layer, n_slabs=n_slabs),
        grid=grid,
        in_specs=in_specs,
        out_specs=[pl.BlockSpec((tm, tn), lambda i, j: (i, j)), any_spec, any_spec, any_spec],
        out_shape=[jax.ShapeDtypeStruct((m, n), F32)] + [jax.ShapeDtypeStruct(c.shape, c.dtype) for c in caches],
        scratch_shapes=[pltpu.VMEM((tm, k), BF16), pltpu.SemaphoreType.DMA((N_COPY_SEMS,))],
        input_output_aliases=aliases,
        compiler_params=pltpu.CompilerParams(
            dimension_semantics=("arbitrary", "arbitrary"),
            vmem_limit_bytes=V7X_VMEM_LIMIT_BYTES),
        name="proj_shift",
    )(*args)
    return outs[0], tuple(outs[1:])


def _cache_append_body(buf_ref, new_ref, o_ref):
    del buf_ref
    o_ref[0, :, :, 0] = new_ref[0]


def _cache_append(buf, new_rows):
    nl, b, _, lb, h, dh = buf.shape
    return pl.pallas_call(
        _cache_append_body,
        grid=(nl,),
        in_specs=[pl.BlockSpec(memory_space=pl.ANY),
                  pl.BlockSpec((1, b, 2, h, dh), lambda l: (l, 0, 0, 0, 0))],
        out_specs=pl.BlockSpec((1, b, 2, 1, h, dh), lambda l: (l, 0, 0, lb - 1, 0, 0)),
        out_shape=jax.ShapeDtypeStruct(buf.shape, buf.dtype),
        input_output_aliases={0: 0},
        compiler_params=pltpu.CompilerParams(
            dimension_semantics=("arbitrary",), vmem_limit_bytes=V7X_VMEM_LIMIT_BYTES),
        name="cache_append",
    )(buf, new_rows)


LANES = 128
COL_A_VAL = 0
COL_A_GLU = CONV_CH
COL_A_GATE = 2 * CONV_CH
COL_Q = 3 * CONV_CH
COL_K = COL_Q + ATT_QKV
COL_V = COL_K + ATT_QKV
COL_B_GATE = COL_V + ATT_QKV
COL_C_QKV = COL_B_GATE + ATT_OUT
COL_C_GATE = COL_C_QKV + DN_QKV
COL_MLOG = COL_C_GATE + DN_OUT
COL_BETA = COL_MLOG + N_BRANCH * D_MODEL
COL_ALPHA = COL_BETA + DN_HEADS
MXU_WIDTH = 256
N_PROJ = _cdiv(D_IN, MXU_WIDTH) * MXU_WIDTH
PROJ_TN = 768
_SRC_BETA = COL_MLOG


def _prep_w_in(w_in):
    pad = jnp.zeros((w_in.shape[0], N_PROJ - D_IN), w_in.dtype)
    w = jnp.concatenate([w_in[:, :_SRC_BETA], w_in[:, _SRC_BETA + 2 * DN_HEADS:],
                         w_in[:, _SRC_BETA:_SRC_BETA + 2 * DN_HEADS], pad], axis=1)
    return w.astype(BF16)


ATT_SB = ATT_BLOCK * max(d for _, d in ATT_PATTERNS)
ATT_SPAN = ATT_BLOCK
assert all(w // d == ATT_SPAN for w, d in ATT_PATTERNS)


def _rows(start, size, stride):
    return pl.ds(start, size) if stride == 1 else pl.ds(start, size, stride=stride)


def _rms_rows(x, gain):
    return x * lax.rsqrt(jnp.mean(x * x, axis=-1, keepdims=True) + EPS) * gain


def _attn_prompt_body(q0, q1, q2, k0, k1, k2, v0, v1, v2, gate_ref, qg_ref, kg_ref,
                      yb_ref, kv0, kv1, kv2,
                      kr0, kr1, kr2, vr0, vr1, vr2, on0, on1, on2, ln0, ln1, ln2):
    q_refs, k_refs, v_refs = (q0, q1, q2), (k0, k1, k2), (v0, v1, v2)
    kv_refs = (kv0, kv1, kv2)
    kres, vres = (kr0, kr1, kr2), (vr0, vr1, vr2)
    onat, lnat = (on0, on1, on2), (ln0, ln1, ln2)
    i = pl.program_id(1)
    last = pl.num_programs(1) - 1
    qg = qg_ref[...]
    kg = kg_ref[...]
    scale = HEAD_DIM ** -0.5
    qi = lax.broadcasted_iota(jnp.int32, (ATT_BLOCK, 2 * ATT_BLOCK), 0)
    kj = lax.broadcasted_iota(jnp.int32, (ATT_BLOCK, 2 * ATT_BLOCK), 1)
    delta = kj - qi
    band = (delta >= 0) & (delta <= ATT_SPAN)

    for g, (_, d) in enumerate(ATT_PATTERNS):
        nblk = ATT_SB // (ATT_BLOCK * d)
        units = ATT_BLOCK * nblk

        @pl.when(i == 0)
        def _():
            kres[g][:, 0:ATT_BLOCK, :] = jnp.zeros((d, ATT_BLOCK, HEAD_DIM), BF16)
            vres[g][:, 0:ATT_BLOCK, :] = jnp.zeros((d, ATT_BLOCK, HEAD_DIM), BF16)

        @pl.when(i > 0)
        def _():
            kres[g][:, 0:ATT_BLOCK, :] = kres[g][:, units:units + ATT_BLOCK, :]
            vres[g][:, 0:ATT_BLOCK, :] = vres[g][:, units:units + ATT_BLOCK, :]

        for r in range(d):
            for c in range(nblk):
                rows = _rows(c * ATT_BLOCK * d + r, ATT_BLOCK, d)
                dst = slice(ATT_BLOCK + c * ATT_BLOCK, ATT_BLOCK + (c + 1) * ATT_BLOCK)
                kres[g][r, dst, :] = _rms_rows(k_refs[g][rows, :], kg).astype(BF16)
                vres[g][r, dst, :] = v_refs[g][rows, :].astype(BF16)

        for r in range(d):
            for c in range(nblk):
                rows = _rows(c * ATT_BLOCK * d + r, ATT_BLOCK, d)
                q = _rms_rows(q_refs[g][rows, :], qg).astype(BF16)
                kb = kres[g][r, c * ATT_BLOCK:(c + 2) * ATT_BLOCK, :]
                vb = vres[g][r, c * ATT_BLOCK:(c + 2) * ATT_BLOCK, :]
                s = lax.dot_general(q, kb, (((1,), (1,)), ((), ())), preferred_element_type=F32) * scale
                if c == 0:
                    first = jnp.where(i == 0, ATT_BLOCK, 0)
                    mask = band & (kj >= first)
                else:
                    mask = band
                s = jnp.where(mask, s, NEG_INF)
                m = jnp.max(s, axis=-1, keepdims=True)
                p = jnp.exp(s - m)
                l = jnp.sum(p, axis=-1, keepdims=True)
                o = jnp.dot(p.astype(BF16), vb, preferred_element_type=F32) / l
                onat[g][rows, :] = o
                lnat[g][rows, :] = jnp.broadcast_to(m + jnp.log(l), (ATT_BLOCK, HEAD_DIM))

    chunk = 256
    for c in range(ATT_SB // chunk):
        rows = slice(c * chunk, (c + 1) * chunk)
        l0, l1, l2 = lnat[0][rows, :], lnat[1][rows, :], lnat[2][rows, :]
        m = jnp.maximum(jnp.maximum(l0, l1), l2)
        e0, e1, e2 = jnp.exp(l0 - m), jnp.exp(l1 - m), jnp.exp(l2 - m)
        y = (e0 * onat[0][rows, :] + e1 * onat[1][rows, :] + e2 * onat[2][rows, :]) / (e0 + e1 + e2)
        gate = gate_ref[rows, :]
        yb_ref[rows, :] = y * (gate * jax.nn.sigmoid(gate))

    @pl.when(i == last)
    def _():
        for g, (window, _) in enumerate(ATT_PATTERNS):
            tail = slice(ATT_SB - window, ATT_SB)
            kv_refs[g][0, :, :] = _rms_rows(k_refs[g][tail, :], kg)
            kv_refs[g][1, :, :] = v_refs[g][tail, :]


def _attn_prompt(proj, q_norm_g, k_norm_g):
    t = proj.shape[0]
    assert t % ATT_SB == 0
    n_sb = t // ATT_SB

    def col_spec(col0, g):
        blk0 = col0 // LANES + g * H_PER_GROUP
        return pl.BlockSpec((ATT_SB, HEAD_DIM), lambda h, i: (i, blk0 + h))

    in_specs = ([col_spec(COL_Q, g) for g in range(N_ATT_GROUPS)]
                + [col_spec(COL_K, g) for g in range(N_ATT_GROUPS)]
                + [col_spec(COL_V, g) for g in range(N_ATT_GROUPS)]
                + [col_spec(COL_B_GATE, 0),
                   pl.BlockSpec((1, HEAD_DIM), lambda h, i: (0, 0)),
                   pl.BlockSpec((1, HEAD_DIM), lambda h, i: (0, 0))])
    out_specs = [pl.BlockSpec((ATT_SB, HEAD_DIM), lambda h, i: (i, h))]
    out_shape = [jax.ShapeDtypeStruct((t, ATT_OUT), F32)]
    scratch = []
    for window, _ in ATT_PATTERNS:
        out_specs.append(pl.BlockSpec((2, window, HEAD_DIM), lambda h, i: (0, 0, h)))
        out_shape.append(jax.ShapeDtypeStruct((2, window, ATT_OUT), F32))
    for _ in range(2):
        for _, d in ATT_PATTERNS:
            scratch.append(pltpu.VMEM((d, ATT_BLOCK + ATT_SB // d, HEAD_DIM), BF16))
    for _ in range(2 * N_ATT_GROUPS):
        scratch.append(pltpu.VMEM((ATT_SB, HEAD_DIM), F32))
    return pl.pallas_call(
        _attn_prompt_body,
        grid=(H_PER_GROUP, n_sb),
        in_specs=in_specs,
        out_specs=out_specs,
        out_shape=out_shape,
        scratch_shapes=scratch,
        compiler_params=pltpu.CompilerParams(
            dimension_semantics=("parallel", "arbitrary"),
            vmem_limit_bytes=V7X_VMEM_LIMIT_BYTES),
        name="attn_prompt",
    )(*([proj] * 10), q_norm_g.reshape(1, HEAD_DIM), k_norm_g.reshape(1, HEAD_DIM))


CONV_TT = 256
CONV_HALO = 32
CONV_RC = 64
CONV_LC = 256
SUBLANES = 8
CONV_SH_ROWS = 40
assert (CONV_HALO + CONV_TT - SUBLANES) % CONV_SH_ROWS == 0


def _silu(x):
    return x * jax.nn.sigmoid(x)


def _conv_a_body(aval_ref, aglu_ref, agate_ref, w_ref, b_ref, lg_ref, lb_ref,
                 ya_ref, tail_ref, ext_ref, y_ref, sh_ref):
    i = pl.program_id(0)

    @pl.when(i == 0)
    def _():
        ext_ref[0:CONV_HALO, :] = jnp.zeros((CONV_HALO, CONV_CH), F32)

    @pl.when(i > 0)
    def _():
        ext_ref[0:CONV_HALO, :] = ext_ref[CONV_TT:CONV_TT + CONV_HALO, :]

    for rc in range(CONV_TT // CONV_RC):
        rows = slice(rc * CONV_RC, (rc + 1) * CONV_RC)
        ext_ref[CONV_HALO + rc * CONV_RC:CONV_HALO + (rc + 1) * CONV_RC, :] = (
            aval_ref[rows, :] * jax.nn.sigmoid(aglu_ref[rows, :]))

    n_sh = CONV_HALO + CONV_TT - SUBLANES
    for s in range(1, SUBLANES):
        for r0 in range(0, n_sh, CONV_SH_ROWS):
            sh_ref[s, r0:r0 + CONV_SH_ROWS, :] = ext_ref[r0 + s:r0 + s + CONV_SH_ROWS, :]

    first = CONV_HALO - (CONV_K - 1)
    for lc in range(CONV_CH // CONV_LC):
        lanes = slice(lc * CONV_LC, (lc + 1) * CONV_LC)
        for rc in range(CONV_TT // CONV_RC):
            acc = jnp.broadcast_to(b_ref[:, lanes], (CONV_RC, CONV_LC))
            for k in range(CONV_K):
                s = (first + k) % SUBLANES
                r0 = first + k - s + rc * CONV_RC
                src = ext_ref[r0:r0 + CONV_RC, lanes] if s == 0 else sh_ref[s, r0:r0 + CONV_RC, lanes]
                acc = acc + w_ref[k:k + 1, lanes] * src
            y_ref[rc * CONV_RC:(rc + 1) * CONV_RC, lanes] = acc

    for rc in range(CONV_TT // CONV_RC):
        rows = slice(rc * CONV_RC, (rc + 1) * CONV_RC)
        y = y_ref[rows, :]
        yc = y - jnp.mean(y, axis=-1, keepdims=True)
        yn = yc * lax.rsqrt(jnp.mean(yc * yc, axis=-1, keepdims=True) + EPS)
        ya_ref[rows, :] = _silu(yn * lg_ref[...] + lb_ref[...]) * _silu(agate_ref[rows, :])

    @pl.when(i == pl.num_programs(0) - 1)
    def _():
        tail_ref[...] = ext_ref[CONV_HALO + CONV_TT - (CONV_K - 1):CONV_HALO + CONV_TT, :]


def _conv_a_prompt(proj, conv_w, conv_b, ln_g, ln_b):
    t = proj.shape[0]
    assert t % CONV_TT == 0
    nb = CONV_CH // LANES

    def col_spec(col0):
        return pl.BlockSpec((CONV_TT, CONV_CH), lambda i: (i, col0 // CONV_CH))

    def full(shape):
        return pl.BlockSpec(shape, lambda i: (0,) * len(shape))

    del nb
    return pl.pallas_call(
        _conv_a_body,
        grid=(t // CONV_TT,),
        in_specs=[col_spec(COL_A_VAL), col_spec(COL_A_GLU), col_spec(COL_A_GATE),
                  full((CONV_K, CONV_CH)), full((1, CONV_CH)), full((1, CONV_CH)), full((1, CONV_CH))],
        out_specs=[pl.BlockSpec((CONV_TT, CONV_CH), lambda i: (i, 0)), full((CONV_K - 1, CONV_CH))],
        out_shape=[jax.ShapeDtypeStruct((t, CONV_CH), F32),
                   jax.ShapeDtypeStruct((CONV_K - 1, CONV_CH), F32)],
        scratch_shapes=[pltpu.VMEM((CONV_HALO + CONV_TT, CONV_CH), F32),
                        pltpu.VMEM((CONV_TT, CONV_CH), F32),
                        pltpu.VMEM((SUBLANES, CONV_HALO + CONV_TT, CONV_CH), F32)],
        compiler_params=pltpu.CompilerParams(
            dimension_semantics=("arbitrary",), vmem_limit_bytes=V7X_VMEM_LIMIT_BYTES),
        name="conv_a",
    )(proj, proj, proj, conv_w, conv_b.reshape(1, CONV_CH), ln_g.reshape(1, CONV_CH), ln_b.reshape(1, CONV_CH))


DN_TB = 256
DN_NC = DN_TB // DN_CHUNK
DN_HALO = 8
assert DN_DK == DN_DV == LANES


def _split_bf16(x, parts):
    out = []
    for _ in range(parts):
        hi = x.astype(BF16)
        out.append(hi)
        x = x - hi.astype(F32)
    return out


def _dot_nt(a, b):
    return lax.dot_general(a, b, (((1,), (1,)), ((), ())), preferred_element_type=F32)


def _dn_prep_body(xq_ref, xk_ref, xv_ref, hq_ref, hk_ref, hv_ref, ba_ref, wq_ref, wk_ref, wv_ref,
                  pa_ref, pd_ref,
                  u0_ref, w_ref, qd_ref, kdt_ref, qk_ref, e_ref,
                  eq_ref, ek_ref, ev_ref):
    i = pl.program_id(0)
    for ext, halo, x in ((eq_ref, hq_ref, xq_ref), (ek_ref, hk_ref, xk_ref), (ev_ref, hv_ref, xv_ref)):
        ext[0:DN_HALO, :] = jnp.where(i == 0, 0.0, halo[...])
        ext[DN_HALO:DN_HALO + DN_TB, :] = x[...]

    ba = ba_ref[...]
    beta_all = jax.nn.sigmoid(ba)
    z = ba + pd_ref[...]
    g_all = -jnp.exp(pa_ref[...]) * (jnp.maximum(z, 0.0) + jnp.log1p(jnp.exp(-jnp.abs(z))))

    ri = lax.broadcasted_iota(jnp.int32, (DN_CHUNK, DN_CHUNK), 0)
    ci = lax.broadcasted_iota(jnp.int32, (DN_CHUNK, DN_CHUNK), 1)
    causal = ri >= ci
    strict = ri > ci
    tril = causal.astype(BF16)
    eye = (ri == ci).astype(F32)
    first = DN_HALO - (DN_SHORT_K - 1)

    for c in range(DN_NC):
        rows = slice(c * DN_CHUNK, (c + 1) * DN_CHUNK)
        g1, g2, g3 = _split_bf16(g_all[rows, :], 3)
        gsum = (jnp.dot(tril, g1, preferred_element_type=F32) + jnp.dot(tril, g2, preferred_element_type=F32)
                + jnp.dot(tril, g3, preferred_element_type=F32))
        a_all, rhs_all = [], []
        for h in range(DN_HEADS):
            lanes = slice(h * LANES, (h + 1) * LANES)

            def short_conv(ext, wref):
                acc = None
                for k in range(DN_SHORT_K):
                    r0 = first + c * DN_CHUNK + k
                    term = wref[k:k + 1, lanes] * ext[r0:r0 + DN_CHUNK, lanes]
                    acc = term if acc is None else acc + term
                return _silu(acc)

            q = short_conv(eq_ref, wq_ref)
            k = short_conv(ek_ref, wk_ref)
            v = short_conv(ev_ref, wv_ref)
            q = q * lax.rsqrt(jnp.sum(q * q, axis=-1, keepdims=True) + EPS) * (DN_DK ** -0.5)
            k = k * lax.rsqrt(jnp.sum(k * k, axis=-1, keepdims=True) + EPS)
            bcol = jnp.broadcast_to(beta_all[rows, h:h + 1], (DN_CHUNK, LANES))
            gcol = jnp.broadcast_to(gsum[:, DN_HEADS + h:DN_HEADS + h + 1], (DN_CHUNK, LANES))
            grow = jnp.transpose(gcol)[0:DN_CHUNK, :]
            decay = jnp.where(causal, jnp.exp(gcol[:, 0:DN_CHUNK] - grow), 0.0)
            kb = k * bcol
            kbf = k.astype(BF16)
            a = jnp.where(strict, _dot_nt(kb.astype(BF16), kbf) * decay, 0.0)
            qk = _dot_nt(q.astype(BF16), kbf) * decay
            eg = jnp.exp(gcol)
            a_all.append(a)
            rhs_all.append(jnp.concatenate([v * bcol, kb * eg], axis=1).astype(BF16))
            glast = gcol[DN_CHUNK - 1:DN_CHUNK, :]
            qd_ref[rows, lanes] = (q * eg).astype(BF16)
            kdt_ref[c, lanes, :] = jnp.transpose(k * jnp.exp(glast - gcol)).astype(BF16)
            qk_ref[rows, h * LANES:h * LANES + DN_CHUNK] = qk.astype(BF16)
            qk_ref[rows, h * LANES + DN_CHUNK:(h + 1) * LANES] = jnp.zeros((DN_CHUNK, LANES - DN_CHUNK), BF16)
            e_ref[c, h:h + 1, :] = jnp.exp(glast)

        tinv = [eye - a for a in a_all]
        pb = [a.astype(BF16) for a in a_all]
        for _ in range(5):
            pb = [jnp.dot(x, x, preferred_element_type=F32).astype(BF16) for x in pb]
            tinv = [t + jnp.dot(x, t.astype(BF16), preferred_element_type=F32) for x, t in zip(pb, tinv)]
        for h in range(DN_HEADS):
            lanes = slice(h * LANES, (h + 1) * LANES)
            sol = jnp.dot(tinv[h].astype(BF16), rhs_all[h], preferred_element_type=F32)
            u0_ref[rows, lanes] = sol[:, 0:DN_DV]
            w_ref[rows, lanes] = sol[:, DN_DV:].astype(BF16)


def _dn_rec_body(u0_ref, w_ref, qd_ref, kdt_ref, qk_ref, e_ref, gate_ref, og_ref,
                 yc_ref, sfin_ref, s_ref):
    i = pl.program_id(0)

    @pl.when(i == 0)
    def _():
        s_ref[...] = jnp.zeros(s_ref.shape, F32)

    og = og_ref[...]
    for c in range(DN_NC):
        rows = slice(c * DN_CHUNK, (c + 1) * DN_CHUNK)
        heads = range(DN_HEADS)
        lanes = [slice(h * LANES, (h + 1) * LANES) for h in heads]
        s = [s_ref[h] for h in heads]
        sb = [x.astype(BF16) for x in s]
        ub = [(u0_ref[rows, lanes[h]] - jnp.dot(w_ref[rows, lanes[h]], sb[h], preferred_element_type=F32)
               ).astype(BF16) for h in heads]
        for h in heads:
            s_ref[h] = (e_ref[c, h:h + 1, :] * s[h]
                        + jnp.dot(kdt_ref[c, lanes[h], :], ub[h], preferred_element_type=F32))
        for h in heads:
            o = (jnp.dot(qd_ref[rows, lanes[h]], sb[h], preferred_element_type=F32)
                 + jnp.dot(qk_ref[rows, h * LANES:h * LANES + DN_CHUNK], ub[h], preferred_element_type=F32))
            yc_ref[rows, lanes[h]] = _rms_rows(o, og) * _silu(gate_ref[rows, lanes[h]])

    @pl.when(i == pl.num_programs(0) - 1)
    def _():
        sfin_ref[...] = s_ref[...]


def _deltanet_prompt(proj, dconv_w, a_log, dt_bias, o_norm_g):
    t = proj.shape[0]
    assert t % DN_TB == 0
    nt = t // DN_TB
    qkv_blk = COL_C_QKV // DN_QK
    halo_per_tb = DN_TB // DN_HALO

    def x_spec(j):
        return pl.BlockSpec((DN_TB, DN_QK), lambda i: (i, qkv_blk + j))

    def halo_spec(j):
        return pl.BlockSpec((DN_HALO, DN_QK), lambda i: (jnp.maximum(i * halo_per_tb - 1, 0), qkv_blk + j))

    def w_spec(j):
        return pl.BlockSpec((DN_SHORT_K, DN_QK), lambda i: (0, j))

    row = pl.BlockSpec((1, LANES), lambda i: (0, 0))
    pad_a = jnp.zeros((1, LANES), F32).at[0, DN_HEADS:2 * DN_HEADS].set(a_log)
    pad_d = jnp.zeros((1, LANES), F32).at[0, DN_HEADS:2 * DN_HEADS].set(dt_bias)
    wide = pl.BlockSpec((DN_TB, DN_OUT), lambda i: (i, 0))
    kdt_spec = pl.BlockSpec((DN_NC, DN_OUT, DN_CHUNK), lambda i: (i, 0, 0))
    e_spec = pl.BlockSpec((DN_NC, DN_HEADS, LANES), lambda i: (i, 0, 0))
    u0, w, qd, kdt, qk, e = pl.pallas_call(
        _dn_prep_body,
        grid=(nt,),
        in_specs=[x_spec(0), x_spec(1), x_spec(2), halo_spec(0), halo_spec(1), halo_spec(2),
                  pl.BlockSpec((DN_TB, LANES), lambda i: (i, COL_BETA // LANES)),
                  w_spec(0), w_spec(1), w_spec(2), row, row],
        out_specs=[wide, wide, wide, kdt_spec, wide, e_spec],
        out_shape=[jax.ShapeDtypeStruct((t, DN_OUT), F32),
                   jax.ShapeDtypeStruct((t, DN_OUT), BF16),
                   jax.ShapeDtypeStruct((t, DN_OUT), BF16),
                   jax.ShapeDtypeStruct((t // DN_CHUNK, DN_OUT, DN_CHUNK), BF16),
                   jax.ShapeDtypeStruct((t, DN_OUT), BF16),
                   jax.ShapeDtypeStruct((t // DN_CHUNK, DN_HEADS, LANES), F32)],
        scratch_shapes=[pltpu.VMEM((DN_HALO + DN_TB, DN_QK), F32) for _ in range(3)],
        compiler_params=pltpu.CompilerParams(
            dimension_semantics=("parallel",), vmem_limit_bytes=V7X_VMEM_LIMIT_BYTES),
        name="dn_prep",
    )(proj, proj, proj, proj, proj, proj, proj, dconv_w, dconv_w, dconv_w, pad_a, pad_d)

    state = pl.BlockSpec((DN_HEADS, DN_DK, DN_DV), lambda i: (0, 0, 0))
    yc, s_fin = pl.pallas_call(
        _dn_rec_body,
        grid=(nt,),
        in_specs=[wide, wide, wide, kdt_spec, wide, e_spec,
                  pl.BlockSpec((DN_TB, DN_OUT), lambda i: (i, COL_C_GATE // DN_OUT)), row],
        out_specs=[wide, state],
        out_shape=[jax.ShapeDtypeStruct((t, DN_OUT), F32),
                   jax.ShapeDtypeStruct((DN_HEADS, DN_DK, DN_DV), F32)],
        scratch_shapes=[pltpu.VMEM((DN_HEADS, DN_DK, DN_DV), F32)],
        compiler_params=pltpu.CompilerParams(
            dimension_semantics=("arbitrary",), vmem_limit_bytes=V7X_VMEM_LIMIT_BYTES),
        name="dn_rec",
    )(u0, w, qd, kdt, qk, e, proj, o_norm_g.reshape(1, LANES))
    return yc, s_fin


BLK = CONV_CH // LANES
assert CONV_CH == ATT_OUT == DN_OUT == DN_QK == BLK * LANES and H_PER_GROUP == DN_HEADS == BLK


def _sample_mix_body(a_ref, p3_ref, ba_ref, cst_ref, cw_ref, cb_ref, lg_ref, lb_ref,
                     kv0_ref, kv1_ref, kv2_ref, qg_ref, kg_ref,
                     qkt_ref, sct_ref, wt_ref, scv_ref, wv_ref, s0_ref, al_ref, dtb_ref, og_ref,
                     ya_ref, glu_ref, yb_ref, yc_ref, n0_ref, n1_ref, n2_ref, s_ref):
    def seg(col0):
        return p3_ref[0, col0 // LANES:col0 // LANES + BLK, :]

    glu = a_ref[0, :, COL_A_VAL:COL_A_VAL + CONV_CH] * jax.nn.sigmoid(a_ref[0, :, COL_A_GLU:COL_A_GLU + CONV_CH])
    y = (jnp.sum(cst_ref[0] * cw_ref[0:CONV_K - 1, :], axis=0, keepdims=True)
         + glu * cw_ref[CONV_K - 1:CONV_K, :] + cb_ref[...])
    yc = y - jnp.mean(y, axis=-1, keepdims=True)
    yn = yc * lax.rsqrt(jnp.mean(yc * yc, axis=-1, keepdims=True) + EPS)
    ya_ref[0] = _silu(yn * lg_ref[...] + lb_ref[...]) * _silu(a_ref[0, :, COL_A_GATE:COL_A_GATE + CONV_CH])
    glu_ref[0] = glu

    scale = HEAD_DIM ** -0.5
    outs, lses = [], []
    for g, (kv_ref, new_ref) in enumerate(((kv0_ref, n0_ref), (kv1_ref, n1_ref), (kv2_ref, n2_ref))):
        q = _rms_rows(seg(COL_Q + g * ATT_OUT), qg_ref[...])
        k_new = _rms_rows(seg(COL_K + g * ATT_OUT), kg_ref[...])
        v_new = seg(COL_V + g * ATT_OUT)
        new_ref[0, 0] = k_new
        new_ref[0, 1] = v_new
        kc = kv_ref[0, 0]
        vc = kv_ref[0, 1]
        s = jnp.sum(kc * q[None], axis=-1, keepdims=True) * scale
        s_new = jnp.sum(k_new * q, axis=-1, keepdims=True) * scale
        m = jnp.maximum(jnp.max(s, axis=0), s_new)
        p = jnp.exp(s - m[None])
        p_new = jnp.exp(s_new - m)
        l = jnp.sum(p, axis=0) + p_new
        outs.append((jnp.sum(p * vc, axis=0) + p_new * v_new) / l)
        lses.append(m + jnp.log(l))
    m = jnp.maximum(jnp.maximum(lses[0], lses[1]), lses[2])
    e = [jnp.exp(x - m) for x in lses]
    yb = (e[0] * outs[0] + e[1] * outs[1] + e[2] * outs[2]) / (e[0] + e[1] + e[2])
    yb_ref[0] = yb * _silu(seg(COL_B_GATE))

    first = DN_SHORT_K - 1
    qk = sct_ref[0, 0] * wt_ref[0]
    for j in range(1, first):
        qk = qk + sct_ref[0, j] * wt_ref[j]
    qk = _silu(qk + qkt_ref[0] * wt_ref[first])
    qk = qk * lax.rsqrt(jnp.sum(qk * qk, axis=0, keepdims=True) + EPS)
    v = scv_ref[0, 0] * wv_ref[0]
    for j in range(1, first):
        v = v + scv_ref[0, j] * wv_ref[j]
    v = _silu(v + seg(COL_C_QKV + 2 * DN_QK) * wv_ref[first])
    hi = lax.broadcasted_iota(jnp.int32, (DN_HEADS, LANES), 0)
    li = lax.broadcasted_iota(jnp.int32, (DN_HEADS, LANES), 1)
    ba = jnp.broadcast_to(ba_ref[0], (DN_HEADS, LANES))
    beta = jax.nn.sigmoid(jnp.sum(jnp.where(li == hi, ba, 0.0), axis=-1, keepdims=True))
    z = jnp.sum(jnp.where(li == hi + DN_HEADS, ba, 0.0), axis=-1, keepdims=True) + dtb_ref[...]
    gdec = jnp.exp(-jnp.exp(al_ref[...]) * (jnp.maximum(z, 0.0) + jnp.log1p(jnp.exp(-jnp.abs(z)))))
    gate = seg(COL_C_GATE)
    for h in range(DN_HEADS):
        qc = qk[:, h:h + 1] * (DN_DK ** -0.5)
        kc = qk[:, DN_HEADS + h:DN_HEADS + h + 1]
        eg = gdec[h:h + 1, :]
        s0 = s0_ref[0, h]
        u = beta[h:h + 1, :] * (v[h:h + 1, :] - eg * jnp.sum(kc * s0, axis=0, keepdims=True))
        s1 = eg * s0 + kc * u
        s_ref[0, h] = s1
        o = jnp.sum(qc * s1, axis=0, keepdims=True)
        yc_ref[0, h:h + 1, :] = _rms_rows(o, og_ref[...]) * _silu(gate[h:h + 1, :])


def _sample_mix(proj2, layer, conv_bufs, kv_bufs, sc_buf, s0s, conv_w, conv_b, ln_g, ln_b, q_norm_g, k_norm_g,
                dconv_w, a_log, dt_bias, o_norm_g):
    b = proj2.shape[0]
    rows = proj2.reshape(b, 1, N_PROJ)
    p3 = proj2.reshape(b, N_PROJ // LANES, LANES)
    nq = 2 * DN_QK
    qkt = proj2[:, COL_C_QKV:COL_C_QKV + nq].reshape(b, 2 * DN_HEADS, DN_DK).transpose(0, 2, 1)
    sct = sc_buf[:, :, :nq].reshape(b, DN_SHORT_K - 1, 2 * DN_HEADS, DN_DK).transpose(0, 1, 3, 2)
    wt = dconv_w[:, :nq].reshape(DN_SHORT_K, 2 * DN_HEADS, DN_DK).transpose(0, 2, 1)
    scv = sc_buf[:, :, nq:].reshape(b, DN_SHORT_K - 1, DN_HEADS, DN_DV)
    wv = dconv_w[:, nq:].reshape(DN_SHORT_K, DN_HEADS, DN_DV)
    kv_views, kv_specs = [], []
    for kv, (window, d) in zip(kv_bufs, ATT_PATTERNS):
        assert kv.shape[1:] == (b, 2, window, H_PER_GROUP, HEAD_DIM) and window == ATT_SPAN * d
        kv_views.append(kv.reshape(kv.shape[0], b, 2, ATT_SPAN, d, H_PER_GROUP, HEAD_DIM))
        kv_specs.append(pl.BlockSpec((None, 1, 2, ATT_SPAN, None, H_PER_GROUP, HEAD_DIM),
                                     lambda i: (layer, i, 0, 0, 0, 0, 0)))

    def full(shape):
        return pl.BlockSpec(shape, lambda i: (0,) * len(shape))

    def per_b(shape):
        return pl.BlockSpec((1,) + shape, lambda i: (i,) + (0,) * len(shape))

    def per_lb(shape):
        return pl.BlockSpec((None, 1) + shape, lambda i: (layer, i) + (0,) * len(shape))

    col = lambda x: x.reshape(-1, 1)
    row = lambda x: x.reshape(1, -1)
    tile_out = jax.ShapeDtypeStruct((b, BLK, LANES), F32)
    new_out = jax.ShapeDtypeStruct((b, 2, H_PER_GROUP, HEAD_DIM), F32)
    outs = pl.pallas_call(
        _sample_mix_body,
        grid=(b,),
        in_specs=[per_b((1, N_PROJ)), per_b((N_PROJ // LANES, LANES)),
                  pl.BlockSpec((1, 1, LANES), lambda i: (i, 0, COL_BETA // LANES)),
                  per_lb((CONV_K - 1, CONV_CH)), full((CONV_K, CONV_CH)),
                  full((1, CONV_CH)), full((1, CONV_CH)), full((1, CONV_CH)),
                  *kv_specs, full((1, HEAD_DIM)), full((1, HEAD_DIM)),
                  per_b((DN_DK, 2 * DN_HEADS)), per_b((DN_SHORT_K - 1, DN_DK, 2 * DN_HEADS)),
                  full((DN_SHORT_K, DN_DK, 2 * DN_HEADS)),
                  per_b((DN_SHORT_K - 1, DN_HEADS, DN_DV)), full((DN_SHORT_K, DN_HEADS, DN_DV)),
                  per_lb((DN_HEADS, DN_DK, DN_DV)), full((DN_HEADS, 1)), full((DN_HEADS, 1)), full((1, DN_DV))],
        out_specs=[per_b((1, CONV_CH)), per_b((1, CONV_CH)), per_b((BLK, LANES)), per_b((BLK, LANES)),
                   per_b((2, H_PER_GROUP, HEAD_DIM)), per_b((2, H_PER_GROUP, HEAD_DIM)),
                   per_b((2, H_PER_GROUP, HEAD_DIM)), per_b((DN_HEADS, DN_DK, DN_DV))],
        out_shape=[jax.ShapeDtypeStruct((b, 1, CONV_CH), F32), jax.ShapeDtypeStruct((b, 1, CONV_CH), F32),
                   tile_out, tile_out, new_out, new_out, new_out,
                   jax.ShapeDtypeStruct((b, DN_HEADS, DN_DK, DN_DV), F32)],
        compiler_params=pltpu.CompilerParams(
            dimension_semantics=("parallel",), vmem_limit_bytes=V7X_VMEM_LIMIT_BYTES),
        name="sample_mix",
    )(rows, p3, rows, conv_bufs, conv_w, row(conv_b), row(ln_g), row(ln_b),
      *kv_views, row(q_norm_g), row(k_norm_g),
      qkt, sct, wt, scv, wv, s0s, col(a_log), col(dt_bias), row(o_norm_g))
    ya, glu, yb, yc, n0, n1, n2, s_new = outs
    return (ya.reshape(b, CONV_CH), glu, yb.reshape(b, ATT_OUT), yc.reshape(b, DN_OUT), (n0, n1, n2), s_new)


SHIFT_ROWS = 512


def _cache_shift_body(cur_ref, nxt_ref, new_ref, o_ref):
    r = cur_ref.shape[3]
    o_ref[0, 0, 0, 0:r - 1] = cur_ref[0, 0, 0, 1:r]
    is_last = pl.program_id(3) == pl.num_programs(3) - 1
    o_ref[0, 0, 0, r - 1] = jnp.where(is_last, new_ref[0, 0, 0], nxt_ref[0, 0, 0, 0])


def _cache_shift(cache, new_rows):
    nl, b, _, lb, h, dh = cache.shape
    r = min(SHIFT_ROWS, lb)
    assert lb % r == 0
    nchunk = lb // r
    return pl.pallas_call(
        _cache_shift_body,
        grid=(nl, b, 2, nchunk),
        in_specs=[pl.BlockSpec((1, 1, 1, r, h, dh), lambda l, i, k, c: (l, i, k, c, 0, 0)),
                  pl.BlockSpec((1, 1, 1, 1, h, dh),
                               lambda l, i, k, c: (l, i, k, jnp.minimum((c + 1) * r, lb - 1), 0, 0)),
                  pl.BlockSpec((1, 1, 1, h, dh), lambda l, i, k, c: (l, i, k, 0, 0))],
        out_specs=pl.BlockSpec((1, 1, 1, r, h, dh), lambda l, i, k, c: (l, i, k, c, 0, 0)),
        out_shape=jax.ShapeDtypeStruct(cache.shape, cache.dtype),
        compiler_params=pltpu.CompilerParams(
            dimension_semantics=("parallel", "parallel", "parallel", "arbitrary"),
            vmem_limit_bytes=V7X_VMEM_LIMIT_BYTES),
        name="cache_shift",
    )(cache, cache, new_rows)


def _rms_norm(x, g):
    y = x * lax.rsqrt(jnp.mean(x * x, axis=-1, keepdims=True) + EPS)
    return y * g


def _layer_norm(x, g, b):
    xc = x - jnp.mean(x, axis=-1, keepdims=True)
    y = xc * lax.rsqrt(jnp.mean(xc * xc, axis=-1, keepdims=True) + EPS)
    return y * g + b


def _l2_norm(x):
    return x * lax.rsqrt(jnp.sum(x * x, axis=-1, keepdims=True) + EPS)


def _causal_depthwise_conv(x, buf, w):
    xc = jnp.concatenate([buf, x], axis=1)
    y = lax.conv_general_dilated(xc, w[:, None, :], window_strides=(1,), padding='VALID',
                                 dimension_numbers=('NWC', 'WIO', 'NWC'), feature_group_count=x.shape[-1])
    return y, xc[:, xc.shape[1] - (w.shape[0] - 1):]


def _dilated_attn_prompt(q, k, v, window, dil):
    B, S, H, Dh = q.shape
    span = window // dil
    L = S // dil
    nb = -(-L // ATT_BLOCK)
    Lp = nb * ATT_BLOCK

    def by_residue(t):
        t = t.reshape(B, L, dil, H, Dh).transpose(0, 2, 1, 3, 4)
        t = jnp.pad(t, ((0, 0), (0, 0), (0, Lp - L), (0, 0), (0, 0)))
        return t.reshape(B, dil, nb, ATT_BLOCK, H, Dh)

    def with_prev_block(t):
        prev = jnp.pad(t, ((0, 0), (0, 0), (1, 0), (0, 0), (0, 0), (0, 0)))[:, :, :nb]
        return jnp.concatenate([prev, t], axis=3)

    qb = by_residue(q)
    kb = with_prev_block(by_residue(k))
    vb = with_prev_block(by_residue(v))
    s = jnp.einsum('brnqhd,brnkhd->brnhqk', qb, kb, preferred_element_type=F32) * (Dh ** -0.5)
    qi = jnp.arange(ATT_BLOCK)[:, None]
    kj = jnp.arange(2 * ATT_BLOCK)[None, :]
    dist = ATT_BLOCK + qi - kj
    kpos = (jnp.arange(nb)[:, None, None] - 1) * ATT_BLOCK + kj
    mask = (dist >= 0) & (dist <= span) & (kpos >= 0)
    s = jnp.where(mask[:, None], s, NEG_INF)
    m = jnp.max(s, axis=-1, keepdims=True)
    p = jnp.exp(s - m)
    l = jnp.sum(p, axis=-1, keepdims=True)
    o = jnp.einsum('brnhqk,brnkhd->brnqhd', p, vb) / jnp.swapaxes(l, 3, 4)
    lse = jnp.swapaxes((m + jnp.log(l))[..., 0], 3, 4)
    o = o.reshape(B, dil, Lp, H, Dh)[:, :, :L].transpose(0, 2, 1, 3, 4).reshape(B, S, H, Dh)
    lse = lse.reshape(B, dil, Lp, H)[:, :, :L].transpose(0, 2, 1, 3).reshape(B, S, H)
    return o, lse


def _dilated_attn_cached(q, k, v, kv_buf, window, dil):
    B, T, H, Dh = q.shape
    Lb = kv_buf.shape[2]
    span = window // dil
    kc = jnp.concatenate([kv_buf[:, 0], k], axis=1)
    vc = jnp.concatenate([kv_buf[:, 1], v], axis=1)
    idx = Lb + jnp.arange(T)[:, None] - dil * jnp.arange(span + 1)[None, :]
    valid = idx >= 0
    idx = jnp.maximum(idx, 0)
    kg = kc[:, idx]
    vg = vc[:, idx]
    s = jnp.einsum('bthd,btmhd->bthm', q, kg, preferred_element_type=F32) * (Dh ** -0.5)
    s = jnp.where(valid[:, None, :], s, NEG_INF)
    m = jnp.max(s, axis=-1, keepdims=True)
    p = jnp.exp(s - m)
    l = jnp.sum(p, axis=-1, keepdims=True)
    o = jnp.einsum('bthm,btmhd->bthd', p, vg) / l
    lse = (m + jnp.log(l))[..., 0]
    start = kc.shape[1] - Lb
    new_buf = jnp.stack([kc[:, start:], vc[:, start:]], axis=1)
    return o, lse, new_buf


def _gated_delta_rule(q, k, v, beta, g, s0):
    B, T, H, dk = q.shape
    dv = v.shape[-1]
    C = DN_CHUNK
    n = -(-T // C)
    pad = n * C - T

    def chunked(t):
        t = jnp.pad(t, ((0, 0), (0, pad)) + ((0, 0),) * (t.ndim - 2))
        t = t.reshape((B, n, C) + t.shape[2:])
        return t.transpose((1, 0, 3, 2) + tuple(range(4, t.ndim)))

    qc, kc, vc, bc, gc = (chunked(t) for t in (q, k, v, beta, g))
    G = jnp.cumsum(gc, axis=-1)
    causal = jnp.tril(jnp.ones((C, C), dtype=bool))
    decay = jnp.exp(jnp.where(causal, G[..., :, None] - G[..., None, :], -jnp.inf))
    kb = kc * bc[..., None]
    strict = causal & ~jnp.eye(C, dtype=bool)
    a = jnp.where(strict, jnp.einsum('nbhid,nbhjd->nbhij', kb, kc) * decay, 0.0)
    rhs = jnp.concatenate([vc * bc[..., None], kb * jnp.exp(G)[..., None]], axis=-1)
    sol = lax.linalg.triangular_solve(a, rhs, left_side=True, lower=True, unit_diagonal=True)
    u0, w = sol[..., :dv], sol[..., dv:]
    qk = jnp.einsum('nbhid,nbhjd->nbhij', qc, kc) * decay
    q_dec = qc * jnp.exp(G)[..., None]
    g_last = G[..., -1]
    k_dec = kc * jnp.exp(g_last[..., None] - G)[..., None]

    def step(S, xs):
        u0_i, w_i, qk_i, qd_i, kd_i, gl_i = xs
        u = u0_i - jnp.einsum('bhck,bhkv->bhcv', w_i, S)
        o = jnp.einsum('bhck,bhkv->bhcv', qd_i, S) + jnp.einsum('bhij,bhjv->bhiv', qk_i, u)
        S = jnp.exp(gl_i)[..., None, None] * S + jnp.einsum('bhck,bhcv->bhkv', kd_i, u)
        return S, o

    s_fin, o = lax.scan(step, s0, (u0, w, qk, q_dec, k_dec, g_last))
    o = o.transpose(1, 0, 3, 2, 4).reshape(B, n * C, H, dv)[:, :T]
    return o, s_fin


def _gated_deltanet(c_qkv, c_gate, c_beta, c_alpha, sc_buf, s0, dconv_w, a_log, dt_bias, o_norm_g):
    B, T, _ = c_qkv.shape
    y, sc_new = _causal_depthwise_conv(c_qkv, sc_buf, dconv_w)
    y = jax.nn.silu(y)
    q, k, v = jnp.split(y, [DN_QK, 2 * DN_QK], axis=-1)
    q = _l2_norm(q.reshape(B, T, DN_HEADS, DN_DK)) * (DN_DK ** -0.5)
    k = _l2_norm(k.reshape(B, T, DN_HEADS, DN_DK))
    v = v.reshape(B, T, DN_HEADS, DN_DV)
    beta = jax.nn.sigmoid(c_beta)
    g = -jnp.exp(a_log) * jax.nn.softplus(c_alpha + dt_bias)
    o, s_new = _gated_delta_rule(q, k, v, beta, g, s0)
    o = _rms_norm(o, o_norm_g) * jax.nn.silu(c_gate.reshape(B, T, DN_HEADS, DN_DV))
    return o.reshape(B, T, DN_OUT), sc_new, s_new


MERGE_TM = 1024
MERGE_TN = 256
assert COL_MLOG % MERGE_TN == 0 and D_MODEL % MERGE_TN == 0


def _merge_body(ya_ref, yb_ref, yc_ref, ga_ref, gb_ref, gc_ref, w_ref, o_ref, lhs_ref):
    @pl.when(pl.program_id(1) == 0)
    def _():
        for b, y_ref in enumerate((ya_ref, yb_ref, yc_ref)):
            lhs_ref[b] = y_ref[...].astype(BF16)

    acc = None
    for b, g_ref in enumerate((ga_ref, gb_ref, gc_ref)):
        part = jax.nn.sigmoid(g_ref[...]) * jnp.dot(
            lhs_ref[b], w_ref[b * CONV_CH:(b + 1) * CONV_CH, :], preferred_element_type=F32)
        acc = part if acc is None else acc + part
    o_ref[...] = acc


def _branch_merge(ya, yb, yc, proj, w_branch):
    assert CONV_CH == ATT_OUT == DN_OUT
    m = ya.shape[0]
    tm = min(MERGE_TM, m)
    y_spec = pl.BlockSpec((tm, CONV_CH), lambda i, j: (i, 0))

    def gate_spec(b):
        blk0 = (COL_MLOG + b * D_MODEL) // MERGE_TN
        return pl.BlockSpec((tm, MERGE_TN), lambda i, j: (i, blk0 + j))

    return pl.pallas_call(
        _merge_body,
        grid=(_cdiv(m, tm), D_MODEL // MERGE_TN),
        in_specs=[y_spec, y_spec, y_spec, gate_spec(0), gate_spec(1), gate_spec(2),
                  pl.BlockSpec((N_BRANCH * CONV_CH, MERGE_TN), lambda i, j: (0, j))],
        out_specs=pl.BlockSpec((tm, MERGE_TN), lambda i, j: (i, j)),
        out_shape=jax.ShapeDtypeStruct((m, D_MODEL), F32),
        scratch_shapes=[pltpu.VMEM((N_BRANCH, tm, CONV_CH), BF16)],
        compiler_params=pltpu.CompilerParams(
            dimension_semantics=("parallel", "arbitrary"), vmem_limit_bytes=V7X_VMEM_LIMIT_BYTES),
        name="branch_merge",
    )(ya, yb, yc, proj, proj, proj, w_branch)


def _prompt_layer(x2, layer, caches, cache_bufs, norm_g, w_in_r, conv_w, conv_b, ln_g, ln_b, q_norm_g, k_norm_g,
                  dconv_w, a_log, dt_bias, o_norm_g, w_branch, w_out):
    t = x2.shape[0]
    proj, cache_bufs = _proj_shift(x2, w_in_r, norm_g, layer, caches, cache_bufs, tm=1024, tn=PROJ_TN)
    ya, conv_new = _conv_a_prompt(proj, conv_w, conv_b, ln_g, ln_b)
    yb, kv0, kv1, kv2 = _attn_prompt(proj, q_norm_g, k_norm_g)
    yc, s_new = _deltanet_prompt(proj, dconv_w, a_log, dt_bias, o_norm_g)
    y = _branch_merge(ya, yb, yc, proj, w_branch)
    out = _proj(y, w_out, resid=x2, tm=1024, tn=512)
    sc_new = proj[t - (DN_SHORT_K - 1):, COL_C_QKV:COL_C_QKV + DN_QKV]
    kv_new = [kv.reshape(1, 2, w, H_PER_GROUP, HEAD_DIM) for kv, (w, _) in zip((kv0, kv1, kv2), ATT_PATTERNS)]
    return out, conv_new[None], kv_new, sc_new[None], s_new[None], cache_bufs


def _sample_layer(x, layer, conv_bufs, kv_bufs, sc_buf, s0s, norm_g, w_in_r, conv_w, conv_b, ln_g, ln_b,
                  q_norm_g, k_norm_g, dconv_w, a_log, dt_bias, o_norm_g, w_branch, w_out):
    B, T, _ = x.shape
    assert T == 1
    x2 = x.reshape(B, D_MODEL)
    proj2 = _proj(x2, w_in_r, gain=norm_g, tm=B, tn=PROJ_TN)
    ya, glu, yb, yc, kv_new, s_new = _sample_mix(proj2, layer, conv_bufs, kv_bufs, sc_buf, s0s, conv_w, conv_b,
                                                 ln_g, ln_b, q_norm_g, k_norm_g, dconv_w, a_log, dt_bias, o_norm_g)
    y = _branch_merge(ya, yb, yc, proj2, w_branch)
    out = _proj(y, w_out, resid=x2, tm=B, tn=512).reshape(B, T, D_MODEL)
    conv_new = jnp.concatenate([conv_bufs[layer, :, 1:], glu], axis=1)
    sc_new = jnp.concatenate([sc_buf[:, 1:], proj2[:, None, COL_C_QKV:COL_C_QKV + DN_QKV]], axis=1)
    return out, conv_new, kv_new, sc_new, s_new


def kernel(x_prompt, x_sample, state_conv, cache_kv_w128, cache_kv_w512, cache_kv_w2048, state_short_conv, state_delta, norm_g, w_in, conv_w, conv_b, ln_g, ln_b, q_norm_g, k_norm_g, dconv_w, a_log, dt_bias, o_norm_g, w_branch, w_out):
    bp, t, _ = x_prompt.shape
    assert bp == 1
    xp, xs = x_prompt.reshape(t, D_MODEL), x_sample
    conv_p, conv_s, sc_p, sc_s, d_p, d_s = [], [], [], [], [], []
    kv_p = [[], [], []]
    kv_s = [[], [], []]
    caches = (cache_kv_w128, cache_kv_w512, cache_kv_w2048)
    cache_bufs = None
    for l in range(DEPTH):
        wts = (norm_g[l], _prep_w_in(w_in[l]), conv_w[l], conv_b[l], ln_g[l], ln_b[l], q_norm_g[l], k_norm_g[l],
               dconv_w[l], a_log[l], dt_bias[l], o_norm_g[l], w_branch[l].astype(BF16), w_out[l].astype(BF16))
        xp, c_new, kv_new, sc_new, s_new, cache_bufs = _prompt_layer(xp, l, caches, cache_bufs, *wts)
        conv_p.append(c_new)
        sc_p.append(sc_new)
        d_p.append(s_new)
        for gi in range(N_ATT_GROUPS):
            kv_p[gi].append(kv_new[gi])
        xs, c_new, kv_new, sc_new, s_new = _sample_layer(
            xs, l, state_conv, (cache_kv_w128, cache_kv_w512, cache_kv_w2048),
            state_short_conv[l], state_delta, *wts)
        conv_s.append(c_new)
        sc_s.append(sc_new)
        d_s.append(s_new)
        for gi in range(N_ATT_GROUPS):
            kv_s[gi].append(kv_new[gi])
    kv_s = [[_cache_append(cache_bufs[gi], jnp.stack(kv_s[gi]))] for gi in range(N_ATT_GROUPS)]
    return (xp.reshape(bp, t, D_MODEL), xs,
            jnp.stack(conv_p), jnp.stack(conv_s),
            jnp.stack(kv_p[0]), kv_s[0][0],
            jnp.stack(kv_p[1]), kv_s[1][0],
            jnp.stack(kv_p[2]), kv_s[2][0],
            jnp.stack(sc_p), jnp.stack(sc_s),
            jnp.stack(d_p), jnp.stack(d_s))
```

```python
import functools
import math

import jax
import jax.numpy as jnp
import numpy as np
from jax import lax
from jax.experimental import pallas as pl
from jax.experimental.pallas import tpu as pltpu

D_MODEL = 2048
DEPTH = 2
CONV_CH = 1024
CONV_K = 31
ATT_PATTERNS = ((128, 1), (512, 4), (2048, 16))
N_ATT_GROUPS = 3
H_PER_GROUP = 8
HEAD_DIM = 128
ATT_QKV = N_ATT_GROUPS * H_PER_GROUP * HEAD_DIM
ATT_OUT = H_PER_GROUP * HEAD_DIM
ATT_BLOCK = 128
DN_HEADS = 8
DN_DK = 128
DN_DV = 128
DN_QK = DN_HEADS * DN_DK
DN_QKV = DN_HEADS * (2 * DN_DK + DN_DV)
DN_OUT = DN_HEADS * DN_DV
DN_SHORT_K = 4
DN_CHUNK = 64
N_BRANCH = 3
IN_SPLITS = (CONV_CH, CONV_CH, CONV_CH, ATT_QKV, ATT_QKV, ATT_QKV, ATT_OUT,
             DN_QKV, DN_OUT, DN_HEADS, DN_HEADS, N_BRANCH * D_MODEL)
D_IN = sum(IN_SPLITS)
EPS = 1e-6
NEG_INF = -1e30
F32 = jnp.float32
BF16 = jnp.bfloat16

V7X_VMEM_LIMIT_BYTES = 56 * 1024 * 1024


def _cdiv(a, b):
    return -(-a // b)


def _proj_body(*refs, normalize, residual):
    a_ref, g_ref, w_ref = refs[:3]
    r_ref = refs[3] if residual else None
    o_ref, lhs_ref = refs[-2:]

    @pl.when(pl.program_id(1) == 0)
    def _():
        a = a_ref[...]
        if normalize:
            a = a * lax.rsqrt(jnp.mean(a * a, axis=-1, keepdims=True) + EPS) * g_ref[...]
        lhs_ref[...] = a.astype(BF16)

    acc = jnp.dot(lhs_ref[...], w_ref[...], preferred_element_type=F32)
    if residual:
        acc = acc + r_ref[...]
    o_ref[...] = acc


def _layer_w_spec(layer, k, tn):
    return pl.BlockSpec((None, k, tn), lambda i, j: (layer, 0, j))


def _proj(a, w, layer, gain=None, resid=None, *, tm, tn):
    m, k = a.shape
    n = w.shape[2]
    tm = min(tm, m)
    tn = min(tn, n)
    normalize = gain is not None
    if gain is None:
        gain = jnp.ones((k,), F32)
    in_specs = [
        pl.BlockSpec((tm, k), lambda i, j: (i, 0)),
        pl.BlockSpec((1, k), lambda i, j: (0, 0)),
        _layer_w_spec(layer, k, tn),
    ]
    args = [a, gain.reshape(1, k), w]
    if resid is not None:
        in_specs.append(pl.BlockSpec((tm, tn), lambda i, j: (i, j)))
        args.append(resid)
    return pl.pallas_call(
        functools.partial(_proj_body, normalize=normalize, residual=resid is not None),
        grid=(_cdiv(m, tm), _cdiv(n, tn)),
        in_specs=in_specs,
        out_specs=pl.BlockSpec((tm, tn), lambda i, j: (i, j)),
        out_shape=jax.ShapeDtypeStruct((m, n), F32),
        scratch_shapes=[pltpu.VMEM((tm, k), BF16)],
        compiler_params=pltpu.CompilerParams(
            dimension_semantics=("parallel", "arbitrary"),
            vmem_limit_bytes=V7X_VMEM_LIMIT_BYTES),
        name="proj",
    )(*args)


COPY_PARTS = 3


def _shift_copies(layer, caches, outs, stage, sems, pair, part, to_vmem):
    b, kv = pair // 2, pair % 2
    w0, w1, w2 = (c.shape[3] for c in caches)
    half = w2 // 2

    def mk(g, src0, rows, dst0, slot0, si):
        buf = stage.at[part, pl.ds(slot0, rows)]
        if to_vmem:
            return pltpu.make_async_copy(caches[g].at[layer, b, kv, pl.ds(src0, rows)], buf, sems.at[0, part, si])
        return pltpu.make_async_copy(buf, outs[g].at[layer, b, kv, pl.ds(dst0, rows)], sems.at[1, part, si])

    if part == 0:
        return [mk(2, 1, half, 0, 0, 0)]
    if part == 1:
        return [mk(2, 1 + half, w2 - 1 - half, half, 0, 0)]
    return [mk(1, 1, w1 - 1, 0, 0, 0), mk(0, 1, w0 - 1, 0, w1, 1)]


def _proj_shift_body(*refs, layer, n_slabs):
    a_ref, g_ref, w_ref = refs[:3]
    caches = refs[3:6]
    o_ref = refs[-7]
    outs = refs[-6:-3]
    lhs_ref, stage, sems = refs[-3:]
    step = pl.program_id(0) * pl.num_programs(1) + pl.program_id(1)

    def slab_ops(slab, fn):
        for part in range(COPY_PARTS):
            @pl.when((slab >= 0) & (slab < n_slabs) & (slab % COPY_PARTS == part))
            def _():
                fn(slab // COPY_PARTS, part)

    def start_in(pair, part):
        for c in _shift_copies(layer, caches, outs, stage, sems, pair, part, True):
            c.start()

    def turn_around(pair, part):
        for c in _shift_copies(layer, caches, outs, stage, sems, pair, part, True):
            c.wait()
        for c in _shift_copies(layer, caches, outs, stage, sems, pair, part, False):
            c.start()

    def finish(pair, part):
        for c in _shift_copies(layer, caches, outs, stage, sems, pair, part, False):
            c.wait()

    slab_ops(step - 2, finish)
    slab_ops(step - 1, turn_around)
    slab_ops(step, start_in)

    @pl.when(pl.program_id(1) == 0)
    def _():
        a = a_ref[...]
        lhs_ref[...] = (a * lax.rsqrt(jnp.mean(a * a, axis=-1, keepdims=True) + EPS) * g_ref[...]).astype(BF16)

    o_ref[...] = jnp.dot(lhs_ref[...], w_ref[...], preferred_element_type=F32)


def _proj_shift(a, w, gain, layer, caches, bufs, *, tm, tn):
    m, k = a.shape
    n = w.shape[2]
    assert m % tm == 0 and n % tn == 0
    grid = (m // tm, n // tn)
    nb = caches[0].shape[1]
    n_slabs = nb * 2 * COPY_PARTS
    assert n_slabs + 2 <= grid[0] * grid[1]
    _, _, _, w2, heads, dh = caches[2].shape
    assert caches[1].shape[3] + caches[0].shape[3] <= w2 // 2
    any_spec = pl.BlockSpec(memory_space=pl.ANY)
    in_specs = [pl.BlockSpec((tm, k), lambda i, j: (i, 0)),
                pl.BlockSpec((1, k), lambda i, j: (0, 0)),
                _layer_w_spec(layer, k, tn),
                any_spec, any_spec, any_spec]
    args = [a, gain.reshape(1, k), w, *caches]
    aliases = {}
    if bufs is not None:
        in_specs += [any_spec] * 3
        aliases = {len(args) + g: 1 + g for g in range(3)}
        args += list(bufs)
    outs = pl.pallas_call(
        functools.partial(_proj_shift_body, layer=layer, n_slabs=n_slabs),
        grid=grid,
        in_specs=in_specs,
        out_specs=[pl.BlockSpec((tm, tn), lambda i, j: (i, j)), any_spec, any_spec, any_spec],
        out_shape=[jax.ShapeDtypeStruct((m, n), F32)] + [jax.ShapeDtypeStruct(c.shape, c.dtype) for c in caches],
        scratch_shapes=[pltpu.VMEM((tm, k), BF16),
                        pltpu.VMEM((COPY_PARTS, w2 // 2, heads, dh), F32),
                        pltpu.SemaphoreType.DMA((2, COPY_PARTS, 2))],
        input_output_aliases=aliases,
        compiler_params=pltpu.CompilerParams(
            dimension_semantics=("arbitrary", "arbitrary"),
            vmem_limit_bytes=V7X_VMEM_LIMIT_BYTES),
        name="proj_shift",
    )(*args)
    return outs[0], tuple(outs[1:])


def _cache_append_body(buf_ref, new_ref, o_ref):
    del buf_ref
    o_ref[0, :, :, 0] = new_ref[0]


def _cache_append(buf, new_rows):
    nl, b, _, lb, h, dh = buf.shape
    return pl.pallas_call(
        _cache_append_body,
        grid=(nl,),
        in_specs=[pl.BlockSpec(memory_space=pl.ANY),
                  pl.BlockSpec((1, b, 2, h, dh), lambda l: (l, 0, 0, 0, 0))],
        out_specs=pl.BlockSpec((1, b, 2, 1, h, dh), lambda l: (l, 0, 0, lb - 1, 0, 0)),
        out_shape=jax.ShapeDtypeStruct(buf.shape, buf.dtype),
        input_output_aliases={0: 0},
        compiler_params=pltpu.CompilerParams(
            dimension_semantics=("arbitrary",), vmem_limit_bytes=V7X_VMEM_LIMIT_BYTES),
        name="cache_append",
    )(buf, new_rows)


LANES = 128
COL_A_VAL = 0
COL_A_GLU = CONV_CH
COL_A_GATE = 2 * CONV_CH
COL_Q = 3 * CONV_CH
COL_K = COL_Q + ATT_QKV
COL_V = COL_K + ATT_QKV
COL_B_GATE = COL_V + ATT_QKV
COL_C_QKV = COL_B_GATE + ATT_OUT
COL_C_GATE = COL_C_QKV + DN_QKV
COL_MLOG = COL_C_GATE + DN_OUT
COL_BETA = COL_MLOG + N_BRANCH * D_MODEL
COL_ALPHA = COL_BETA + DN_HEADS
W_PREP_TN = 256
N_PROJ = _cdiv(D_IN, W_PREP_TN) * W_PREP_TN
PROJ_TN = 768
assert N_PROJ % PROJ_TN == 0
_SRC_BETA = COL_MLOG
_BA = 2 * DN_HEADS
_T_MLOG = COL_MLOG // W_PREP_TN
_T_BETA = COL_BETA // W_PREP_TN
assert COL_MLOG % W_PREP_TN == 0 and COL_BETA % W_PREP_TN == 0 and _T_BETA == N_PROJ // W_PREP_TN - 1
W_PREP_ROWS = 256


def _prep_w_in_body(src_ref, nxt_ref, ba_ref, o_ref):
    t = pl.program_id(1)
    k = src_ref.shape[0]

    @pl.when(t < _T_MLOG)
    def _():
        o_ref[...] = src_ref[...].astype(BF16)

    @pl.when((t >= _T_MLOG) & (t < _T_BETA))
    def _():
        for r0 in range(0, k, W_PREP_ROWS):
            rows = slice(r0, r0 + W_PREP_ROWS)
            both = jnp.concatenate([src_ref[rows, :], nxt_ref[rows, :]], axis=1)
            o_ref[rows, :] = both[:, _BA:_BA + W_PREP_TN].astype(BF16)

    @pl.when(t == _T_BETA)
    def _():
        lane = lax.broadcasted_iota(jnp.int32, (k, LANES), 1)
        o_ref[:, 0:LANES] = jnp.where(lane < _BA, ba_ref[...], 0.0).astype(BF16)
        o_ref[:, LANES:] = jnp.zeros((k, W_PREP_TN - LANES), BF16)


def _prep_w_in(w_in):
    nl, k, d_in = w_in.shape
    assert d_in == D_IN
    sub = W_PREP_TN // LANES
    last_src = (COL_BETA + _BA) // W_PREP_TN - 1
    last_nxt = (D_IN - 1) // LANES
    return pl.pallas_call(
        _prep_w_in_body,
        grid=(nl, N_PROJ // W_PREP_TN),
        in_specs=[pl.BlockSpec((None, k, W_PREP_TN), lambda l, t: (l, 0, jnp.minimum(t, last_src))),
                  pl.BlockSpec((None, k, LANES), lambda l, t: (l, 0, jnp.minimum((t + 1) * sub, last_nxt))),
                  pl.BlockSpec((None, k, LANES), lambda l, t: (l, 0, _SRC_BETA // LANES))],
        out_specs=pl.BlockSpec((None, k, W_PREP_TN), lambda l, t: (l, 0, t)),
        out_shape=jax.ShapeDtypeStruct((nl, k, N_PROJ), BF16),
        compiler_params=pltpu.CompilerParams(
            dimension_semantics=("parallel", "arbitrary"), vmem_limit_bytes=V7X_VMEM_LIMIT_BYTES),
        name="w_prep",
    )(w_in, w_in, w_in)


ATT_SB = ATT_BLOCK * max(d for _, d in ATT_PATTERNS)
ATT_SPAN = ATT_BLOCK
assert all(w // d == ATT_SPAN for w, d in ATT_PATTERNS)


def _rows(start, size, stride):
    return pl.ds(start, size) if stride == 1 else pl.ds(start, size, stride=stride)


def _rms_rows(x, gain):
    return x * lax.rsqrt(jnp.mean(x * x, axis=-1, keepdims=True) + EPS) * gain


def _attn_prompt_body(q0, q1, q2, k0, k1, k2, v0, v1, v2, gate_ref, qg_ref, kg_ref,
                      yb_ref, kv0, kv1, kv2,
                      kr0, kr1, kr2, vr0, vr1, vr2, on0, on1, on2, ln0, ln1, ln2):
    q_refs, k_refs, v_refs = (q0, q1, q2), (k0, k1, k2), (v0, v1, v2)
    kv_refs = (kv0, kv1, kv2)
    kres, vres = (kr0, kr1, kr2), (vr0, vr1, vr2)
    onat, lnat = (on0, on1, on2), (ln0, ln1, ln2)
    i = pl.program_id(1)
    last = pl.num_programs(1) - 1
    qg = qg_ref[...]
    kg = kg_ref[...]
    scale = HEAD_DIM ** -0.5
    qi = lax.broadcasted_iota(jnp.int32, (ATT_BLOCK, 2 * ATT_BLOCK), 0)
    kj = lax.broadcasted_iota(jnp.int32, (ATT_BLOCK, 2 * ATT_BLOCK), 1)
    delta = kj - qi
    band = (delta >= 0) & (delta <= ATT_SPAN)

    for g, (_, d) in enumerate(ATT_PATTERNS):
        nblk = ATT_SB // (ATT_BLOCK * d)
        units = ATT_BLOCK * nblk

        @pl.when(i == 0)
        def _():
            kres[g][:, 0:ATT_BLOCK, :] = jnp.zeros((d, ATT_BLOCK, HEAD_DIM), BF16)
            vres[g][:, 0:ATT_BLOCK, :] = jnp.zeros((d, ATT_BLOCK, HEAD_DIM), BF16)

        @pl.when(i > 0)
        def _():
            kres[g][:, 0:ATT_BLOCK, :] = kres[g][:, units:units + ATT_BLOCK, :]
            vres[g][:, 0:ATT_BLOCK, :] = vres[g][:, units:units + ATT_BLOCK, :]

        for r in range(d):
            for c in range(nblk):
                rows = _rows(c * ATT_BLOCK * d + r, ATT_BLOCK, d)
                dst = slice(ATT_BLOCK + c * ATT_BLOCK, ATT_BLOCK + (c + 1) * ATT_BLOCK)
                kres[g][r, dst, :] = _rms_rows(k_refs[g][rows, :], kg).astype(BF16)
                vres[g][r, dst, :] = v_refs[g][rows, :].astype(BF16)

        for r in range(d):
            for c in range(nblk):
                rows = _rows(c * ATT_BLOCK * d + r, ATT_BLOCK, d)
                q = _rms_rows(q_refs[g][rows, :], qg).astype(BF16)
                kb = kres[g][r, c * ATT_BLOCK:(c + 2) * ATT_BLOCK, :]
                vb = vres[g][r, c * ATT_BLOCK:(c + 2) * ATT_BLOCK, :]
                s = lax.dot_general(q, kb, (((1,), (1,)), ((), ())), preferred_element_type=F32) * scale
                if c == 0:
                    first = jnp.where(i == 0, ATT_BLOCK, 0)
                    mask = band & (kj >= first)
                else:
                    mask = band
                s = jnp.where(mask, s, NEG_INF)
                m = jnp.max(s, axis=-1, keepdims=True)
                p = jnp.exp(s - m)
                l = jnp.sum(p, axis=-1, keepdims=True)
                o = jnp.dot(p.astype(BF16), vb, preferred_element_type=F32) / l
                onat[g][rows, :] = o
                lnat[g][rows, :] = jnp.broadcast_to(m + jnp.log(l), (ATT_BLOCK, HEAD_DIM))

    chunk = 256
    for c in range(ATT_SB // chunk):
        rows = slice(c * chunk, (c + 1) * chunk)
        l0, l1, l2 = lnat[0][rows, :], lnat[1][rows, :], lnat[2][rows, :]
        m = jnp.maximum(jnp.maximum(l0, l1), l2)
        e0, e1, e2 = jnp.exp(l0 - m), jnp.exp(l1 - m), jnp.exp(l2 - m)
        y = (e0 * onat[0][rows, :] + e1 * onat[1][rows, :] + e2 * onat[2][rows, :]) / (e0 + e1 + e2)
        gate = gate_ref[rows, :]
        yb_ref[rows, :] = y * (gate * jax.nn.sigmoid(gate))

    @pl.when(i == last)
    def _():
        for g, (window, _) in enumerate(ATT_PATTERNS):
            tail = slice(ATT_SB - window, ATT_SB)
            kv_refs[g][0, :, :] = _rms_rows(k_refs[g][tail, :], kg)
            kv_refs[g][1, :, :] = v_refs[g][tail, :]


def _attn_prompt(proj, q_norm_g, k_norm_g):
    t = proj.shape[0]
    assert t % ATT_SB == 0
    n_sb = t // ATT_SB

    def col_spec(col0, g):
        blk0 = col0 // LANES + g * H_PER_GROUP
        return pl.BlockSpec((ATT_SB, HEAD_DIM), lambda h, i: (i, blk0 + h))

    in_specs = ([col_spec(COL_Q, g) for g in range(N_ATT_GROUPS)]
                + [col_spec(COL_K, g) for g in range(N_ATT_GROUPS)]
                + [col_spec(COL_V, g) for g in range(N_ATT_GROUPS)]
                + [col_spec(COL_B_GATE, 0),
                   pl.BlockSpec((1, HEAD_DIM), lambda h, i: (0, 0)),
                   pl.BlockSpec((1, HEAD_DIM), lambda h, i: (0, 0))])
    out_specs = [pl.BlockSpec((ATT_SB, HEAD_DIM), lambda h, i: (i, h))]
    out_shape = [jax.ShapeDtypeStruct((t, ATT_OUT), F32)]
    scratch = []
    for window, _ in ATT_PATTERNS:
        out_specs.append(pl.BlockSpec((2, window, HEAD_DIM), lambda h, i: (0, 0, h)))
        out_shape.append(jax.ShapeDtypeStruct((2, window, ATT_OUT), F32))
    for _ in range(2):
        for _, d in ATT_PATTERNS:
            scratch.append(pltpu.VMEM((d, ATT_BLOCK + ATT_SB // d, HEAD_DIM), BF16))
    for _ in range(2 * N_ATT_GROUPS):
        scratch.append(pltpu.VMEM((ATT_SB, HEAD_DIM), F32))
    return pl.pallas_call(
        _attn_prompt_body,
        grid=(H_PER_GROUP, n_sb),
        in_specs=in_specs,
        out_specs=out_specs,
        out_shape=out_shape,
        scratch_shapes=scratch,
        compiler_params=pltpu.CompilerParams(
            dimension_semantics=("parallel", "arbitrary"),
            vmem_limit_bytes=V7X_VMEM_LIMIT_BYTES),
        name="attn_prompt",
    )(*([proj] * 10), q_norm_g.reshape(1, HEAD_DIM), k_norm_g.reshape(1, HEAD_DIM))


CONV_TT = 256
CONV_HALO = 32
CONV_RC = 64
CONV_LC = 256
SUBLANES = 8
CONV_SH_ROWS = 40
assert (CONV_HALO + CONV_TT - SUBLANES) % CONV_SH_ROWS == 0


def _silu(x):
    return x * jax.nn.sigmoid(x)


def _conv_a_body(aval_ref, aglu_ref, agate_ref, w_ref, b_ref, lg_ref, lb_ref,
                 ya_ref, tail_ref, ext_ref, y_ref, sh_ref):
    i = pl.program_id(0)

    @pl.when(i == 0)
    def _():
        ext_ref[0:CONV_HALO, :] = jnp.zeros((CONV_HALO, CONV_CH), F32)

    @pl.when(i > 0)
    def _():
        ext_ref[0:CONV_HALO, :] = ext_ref[CONV_TT:CONV_TT + CONV_HALO, :]

    for rc in range(CONV_TT // CONV_RC):
        rows = slice(rc * CONV_RC, (rc + 1) * CONV_RC)
        ext_ref[CONV_HALO + rc * CONV_RC:CONV_HALO + (rc + 1) * CONV_RC, :] = (
            aval_ref[rows, :] * jax.nn.sigmoid(aglu_ref[rows, :]))

    n_sh = CONV_HALO + CONV_TT - SUBLANES
    for s in range(1, SUBLANES):
        for r0 in range(0, n_sh, CONV_SH_ROWS):
            sh_ref[s, r0:r0 + CONV_SH_ROWS, :] = ext_ref[r0 + s:r0 + s + CONV_SH_ROWS, :]

    first = CONV_HALO - (CONV_K - 1)
    for lc in range(CONV_CH // CONV_LC):
        lanes = slice(lc * CONV_LC, (lc + 1) * CONV_LC)
        for rc in range(CONV_TT // CONV_RC):
            acc = jnp.broadcast_to(b_ref[:, lanes], (CONV_RC, CONV_LC))
            for k in range(CONV_K):
                s = (first + k) % SUBLANES
                r0 = first + k - s + rc * CONV_RC
                src = ext_ref[r0:r0 + CONV_RC, lanes] if s == 0 else sh_ref[s, r0:r0 + CONV_RC, lanes]
                acc = acc + w_ref[k:k + 1, lanes] * src
            y_ref[rc * CONV_RC:(rc + 1) * CONV_RC, lanes] = acc

    for rc in range(CONV_TT // CONV_RC):
        rows = slice(rc * CONV_RC, (rc + 1) * CONV_RC)
        y = y_ref[rows, :]
        yc = y - jnp.mean(y, axis=-1, keepdims=True)
        yn = yc * lax.rsqrt(jnp.mean(yc * yc, axis=-1, keepdims=True) + EPS)
        ya_ref[rows, :] = _silu(yn * lg_ref[...] + lb_ref[...]) * _silu(agate_ref[rows, :])

    @pl.when(i == pl.num_programs(0) - 1)
    def _():
        tail_ref[...] = ext_ref[CONV_HALO + CONV_TT - (CONV_K - 1):CONV_HALO + CONV_TT, :]


def _conv_a_prompt(proj, conv_w, conv_b, ln_g, ln_b):
    t = proj.shape[0]
    assert t % CONV_TT == 0
    nb = CONV_CH // LANES

    def col_spec(col0):
        return pl.BlockSpec((CONV_TT, CONV_CH), lambda i: (i, col0 // CONV_CH))

    def full(shape):
        return pl.BlockSpec(shape, lambda i: (0,) * len(shape))

    del nb
    return pl.pallas_call(
        _conv_a_body,
        grid=(t // CONV_TT,),
        in_specs=[col_spec(COL_A_VAL), col_spec(COL_A_GLU), col_spec(COL_A_GATE),
                  full((CONV_K, CONV_CH)), full((1, CONV_CH)), full((1, CONV_CH)), full((1, CONV_CH))],
        out_specs=[pl.BlockSpec((CONV_TT, CONV_CH), lambda i: (i, 0)), full((CONV_K - 1, CONV_CH))],
        out_shape=[jax.ShapeDtypeStruct((t, CONV_CH), F32),
                   jax.ShapeDtypeStruct((CONV_K - 1, CONV_CH), F32)],
        scratch_shapes=[pltpu.VMEM((CONV_HALO + CONV_TT, CONV_CH), F32),
                        pltpu.VMEM((CONV_TT, CONV_CH), F32),
                        pltpu.VMEM((SUBLANES, CONV_HALO + CONV_TT, CONV_CH), F32)],
        compiler_params=pltpu.CompilerParams(
            dimension_semantics=("arbitrary",), vmem_limit_bytes=V7X_VMEM_LIMIT_BYTES),
        name="conv_a",
    )(proj, proj, proj, conv_w, conv_b.reshape(1, CONV_CH), ln_g.reshape(1, CONV_CH), ln_b.reshape(1, CONV_CH))


DN_TB = 256
DN_NC = DN_TB // DN_CHUNK
DN_HALO = 8
assert DN_DK == DN_DV == LANES


def _split_bf16(x, parts):
    out = []
    for _ in range(parts):
        hi = x.astype(BF16)
        out.append(hi)
        x = x - hi.astype(F32)
    return out


def _dot_nt(a, b):
    return lax.dot_general(a, b, (((1,), (1,)), ((), ())), preferred_element_type=F32)


def _dn_prep_body(xq_ref, xk_ref, xv_ref, hq_ref, hk_ref, hv_ref, ba_ref, wq_ref, wk_ref, wv_ref,
                  pa_ref, pd_ref,
                  u0_ref, w_ref, qd_ref, kdt_ref, qk_ref, e_ref,
                  eq_ref, ek_ref, ev_ref):
    i = pl.program_id(0)
    for ext, halo, x in ((eq_ref, hq_ref, xq_ref), (ek_ref, hk_ref, xk_ref), (ev_ref, hv_ref, xv_ref)):
        ext[0:DN_HALO, :] = jnp.where(i == 0, 0.0, halo[...])
        ext[DN_HALO:DN_HALO + DN_TB, :] = x[...]

    ba = ba_ref[...]
    beta_all = jax.nn.sigmoid(ba)
    z = ba + pd_ref[...]
    g_all = -jnp.exp(pa_ref[...]) * (jnp.maximum(z, 0.0) + jnp.log1p(jnp.exp(-jnp.abs(z))))

    ri = lax.broadcasted_iota(jnp.int32, (DN_CHUNK, DN_CHUNK), 0)
    ci = lax.broadcasted_iota(jnp.int32, (DN_CHUNK, DN_CHUNK), 1)
    causal = ri >= ci
    strict = ri > ci
    tril = causal.astype(BF16)
    eye = (ri == ci).astype(F32)
    first = DN_HALO - (DN_SHORT_K - 1)

    for c in range(DN_NC):
        rows = slice(c * DN_CHUNK, (c + 1) * DN_CHUNK)
        g1, g2, g3 = _split_bf16(g_all[rows, :], 3)
        gsum = (jnp.dot(tril, g1, preferred_element_type=F32) + jnp.dot(tril, g2, preferred_element_type=F32)
                + jnp.dot(tril, g3, preferred_element_type=F32))
        a_all, rhs_all = [], []
        for h in range(DN_HEADS):
            lanes = slice(h * LANES, (h + 1) * LANES)

            def short_conv(ext, wref):
                acc = None
                for k in range(DN_SHORT_K):
                    r0 = first + c * DN_CHUNK + k
                    term = wref[k:k + 1, lanes] * ext[r0:r0 + DN_CHUNK, lanes]
                    acc = term if acc is None else acc + term
                return _silu(acc)

            q = short_conv(eq_ref, wq_ref)
            k = short_conv(ek_ref, wk_ref)
            v = short_conv(ev_ref, wv_ref)
            q = q * lax.rsqrt(jnp.sum(q * q, axis=-1, keepdims=True) + EPS) * (DN_DK ** -0.5)
            k = k * lax.rsqrt(jnp.sum(k * k, axis=-1, keepdims=True) + EPS)
            bcol = jnp.broadcast_to(beta_all[rows, h:h + 1], (DN_CHUNK, LANES))
            gcol = jnp.broadcast_to(gsum[:, DN_HEADS + h:DN_HEADS + h + 1], (DN_CHUNK, LANES))
            grow = jnp.transpose(gcol)[0:DN_CHUNK, :]
            decay = jnp.where(causal, jnp.exp(gcol[:, 0:DN_CHUNK] - grow), 0.0)
            kb = k * bcol
            kbf = k.astype(BF16)
            a = jnp.where(strict, _dot_nt(kb.astype(BF16), kbf) * decay, 0.0)
            qk = _dot_nt(q.astype(BF16), kbf) * decay
            eg = jnp.exp(gcol)
            a_all.append(a)
            rhs_all.append(jnp.concatenate([v * bcol, kb * eg], axis=1).astype(BF16))
            glast = gcol[DN_CHUNK - 1:DN_CHUNK, :]
            qd_ref[rows, lanes] = (q * eg).astype(BF16)
            kdt_ref[c, lanes, :] = jnp.transpose(k * jnp.exp(glast - gcol)).astype(BF16)
            qk_ref[rows, h * LANES:h * LANES + DN_CHUNK] = qk.astype(BF16)
            qk_ref[rows, h * LANES + DN_CHUNK:(h + 1) * LANES] = jnp.zeros((DN_CHUNK, LANES - DN_CHUNK), BF16)
            e_ref[c, h:h + 1, :] = jnp.exp(glast)

        tinv = [eye - a for a in a_all]
        pb = [a.astype(BF16) for a in a_all]
        for _ in range(5):
            pb = [jnp.dot(x, x, preferred_element_type=F32).astype(BF16) for x in pb]
            tinv = [t + jnp.dot(x, t.astype(BF16), preferred_element_type=F32) for x, t in zip(pb, tinv)]
        for h in range(DN_HEADS):
            lanes = slice(h * LANES, (h + 1) * LANES)
            sol = jnp.dot(tinv[h].astype(BF16), rhs_all[h], preferred_element_type=F32)
            u0_ref[rows, lanes] = sol[:, 0:DN_DV]
            w_ref[rows, lanes] = sol[:, DN_DV:].astype(BF16)


def _dn_rec_body(u0_ref, w_ref, qd_ref, kdt_ref, qk_ref, e_ref, gate_ref, og_ref,
                 yc_ref, sfin_ref, s_ref):
    i = pl.program_id(0)

    @pl.when(i == 0)
    def _():
        s_ref[...] = jnp.zeros(s_ref.shape, F32)

    og = og_ref[...]
    for c in range(DN_NC):
        rows = slice(c * DN_CHUNK, (c + 1) * DN_CHUNK)
        heads = range(DN_HEADS)
        lanes = [slice(h * LANES, (h + 1) * LANES) for h in heads]
        s = [s_ref[h] for h in heads]
        sb = [x.astype(BF16) for x in s]
        ub = [(u0_ref[rows, lanes[h]] - jnp.dot(w_ref[rows, lanes[h]], sb[h], preferred_element_type=F32)
               ).astype(BF16) for h in heads]
        for h in heads:
            s_ref[h] = (e_ref[c, h:h + 1, :] * s[h]
                        + jnp.dot(kdt_ref[c, lanes[h], :], ub[h], preferred_element_type=F32))
        for h in heads:
            o = (jnp.dot(qd_ref[rows, lanes[h]], sb[h], preferred_element_type=F32)
                 + jnp.dot(qk_ref[rows, h * LANES:h * LANES + DN_CHUNK], ub[h], preferred_element_type=F32))
            yc_ref[rows, lanes[h]] = _rms_rows(o, og) * _silu(gate_ref[rows, lanes[h]])

    @pl.when(i == pl.num_programs(0) - 1)
    def _():
        sfin_ref[...] = s_ref[...]


def _deltanet_prompt(proj, dconv_w, a_log, dt_bias, o_norm_g):
    t = proj.shape[0]
    assert t % DN_TB == 0
    nt = t // DN_TB
    qkv_blk = COL_C_QKV // DN_QK
    halo_per_tb = DN_TB // DN_HALO

    def x_spec(j):
        return pl.BlockSpec((DN_TB, DN_QK), lambda i: (i, qkv_blk + j))

    def halo_spec(j):
        return pl.BlockSpec((DN_HALO, DN_QK), lambda i: (jnp.maximum(i * halo_per_tb - 1, 0), qkv_blk + j))

    def w_spec(j):
        return pl.BlockSpec((DN_SHORT_K, DN_QK), lambda i: (0, j))

    row = pl.BlockSpec((1, LANES), lambda i: (0, 0))
    pad_a = jnp.zeros((1, LANES), F32).at[0, DN_HEADS:2 * DN_HEADS].set(a_log)
    pad_d = jnp.zeros((1, LANES), F32).at[0, DN_HEADS:2 * DN_HEADS].set(dt_bias)
    wide = pl.BlockSpec((DN_TB, DN_OUT), lambda i: (i, 0))
    kdt_spec = pl.BlockSpec((DN_NC, DN_OUT, DN_CHUNK), lambda i: (i, 0, 0))
    e_spec = pl.BlockSpec((DN_NC, DN_HEADS, LANES), lambda i: (i, 0, 0))
    u0, w, qd, kdt, qk, e = pl.pallas_call(
        _dn_prep_body,
        grid=(nt,),
        in_specs=[x_spec(0), x_spec(1), x_spec(2), halo_spec(0), halo_spec(1), halo_spec(2),
                  pl.BlockSpec((DN_TB, LANES), lambda i: (i, COL_BETA // LANES)),
                  w_spec(0), w_spec(1), w_spec(2), row, row],
        out_specs=[wide, wide, wide, kdt_spec, wide, e_spec],
        out_shape=[jax.ShapeDtypeStruct((t, DN_OUT), F32),
                   jax.ShapeDtypeStruct((t, DN_OUT), BF16),
                   jax.ShapeDtypeStruct((t, DN_OUT), BF16),
                   jax.ShapeDtypeStruct((t // DN_CHUNK, DN_OUT, DN_CHUNK), BF16),
                   jax.ShapeDtypeStruct((t, DN_OUT), BF16),
                   jax.ShapeDtypeStruct((t // DN_CHUNK, DN_HEADS, LANES), F32)],
        scratch_shapes=[pltpu.VMEM((DN_HALO + DN_TB, DN_QK), F32) for _ in range(3)],
        compiler_params=pltpu.CompilerParams(
            dimension_semantics=("parallel",), vmem_limit_bytes=V7X_VMEM_LIMIT_BYTES),
        name="dn_prep",
    )(proj, proj, proj, proj, proj, proj, proj, dconv_w, dconv_w, dconv_w, pad_a, pad_d)

    state = pl.BlockSpec((DN_HEADS, DN_DK, DN_DV), lambda i: (0, 0, 0))
    yc, s_fin = pl.pallas_call(
        _dn_rec_body,
        grid=(nt,),
        in_specs=[wide, wide, wide, kdt_spec, wide, e_spec,
                  pl.BlockSpec((DN_TB, DN_OUT), lambda i: (i, COL_C_GATE // DN_OUT)), row],
        out_specs=[wide, state],
        out_shape=[jax.ShapeDtypeStruct((t, DN_OUT), F32),
                   jax.ShapeDtypeStruct((DN_HEADS, DN_DK, DN_DV), F32)],
        scratch_shapes=[pltpu.VMEM((DN_HEADS, DN_DK, DN_DV), F32)],
        compiler_params=pltpu.CompilerParams(
            dimension_semantics=("arbitrary",), vmem_limit_bytes=V7X_VMEM_LIMIT_BYTES),
        name="dn_rec",
    )(u0, w, qd, kdt, qk, e, proj, o_norm_g.reshape(1, LANES))
    return yc, s_fin


BLK = CONV_CH // LANES
assert CONV_CH == ATT_OUT == DN_OUT == DN_QK == BLK * LANES and H_PER_GROUP == DN_HEADS == BLK


def _sample_mix_body(a_ref, p3_ref, ba_ref, cst_ref, cw_ref, cb_ref, lg_ref, lb_ref,
                     kv0_ref, kv1_ref, kv2_ref, qg_ref, kg_ref,
                     qkt_ref, sct_ref, wt_ref, scv_ref, wv_ref, s0_ref, al_ref, dtb_ref, og_ref,
                     ya_ref, glu_ref, yb_ref, yc_ref, n0_ref, n1_ref, n2_ref, s_ref):
    def seg(col0):
        return p3_ref[0, col0 // LANES:col0 // LANES + BLK, :]

    glu = a_ref[0, :, COL_A_VAL:COL_A_VAL + CONV_CH] * jax.nn.sigmoid(a_ref[0, :, COL_A_GLU:COL_A_GLU + CONV_CH])
    y = (jnp.sum(cst_ref[0] * cw_ref[0:CONV_K - 1, :], axis=0, keepdims=True)
         + glu * cw_ref[CONV_K - 1:CONV_K, :] + cb_ref[...])
    yc = y - jnp.mean(y, axis=-1, keepdims=True)
    yn = yc * lax.rsqrt(jnp.mean(yc * yc, axis=-1, keepdims=True) + EPS)
    ya_ref[0] = _silu(yn * lg_ref[...] + lb_ref[...]) * _silu(a_ref[0, :, COL_A_GATE:COL_A_GATE + CONV_CH])
    glu_ref[0] = glu

    scale = HEAD_DIM ** -0.5
    outs, lses = [], []
    for g, (kv_ref, new_ref) in enumerate(((kv0_ref, n0_ref), (kv1_ref, n1_ref), (kv2_ref, n2_ref))):
        q = _rms_rows(seg(COL_Q + g * ATT_OUT), qg_ref[...])
        k_new = _rms_rows(seg(COL_K + g * ATT_OUT), kg_ref[...])
        v_new = seg(COL_V + g * ATT_OUT)
        new_ref[0, 0] = k_new
        new_ref[0, 1] = v_new
        kc = kv_ref[0, 0]
        vc = kv_ref[0, 1]
        s = jnp.sum(kc * q[None], axis=-1, keepdims=True) * scale
        s_new = jnp.sum(k_new * q, axis=-1, keepdims=True) * scale
        m = jnp.maximum(jnp.max(s, axis=0), s_new)
        p = jnp.exp(s - m[None])
        p_new = jnp.exp(s_new - m)
        l = jnp.sum(p, axis=0) + p_new
        outs.append((jnp.sum(p * vc, axis=0) + p_new * v_new) / l)
        lses.append(m + jnp.log(l))
    m = jnp.maximum(jnp.maximum(lses[0], lses[1]), lses[2])
    e = [jnp.exp(x - m) for x in lses]
    yb = (e[0] * outs[0] + e[1] * outs[1] + e[2] * outs[2]) / (e[0] + e[1] + e[2])
    yb_ref[0] = yb * _silu(seg(COL_B_GATE))

    first = DN_SHORT_K - 1
    qk = sct_ref[0, 0] * wt_ref[0]
    for j in range(1, first):
        qk = qk + sct_ref[0, j] * wt_ref[j]
    qk = _silu(qk + qkt_ref[0] * wt_ref[first])
    qk = qk * lax.rsqrt(jnp.sum(qk * qk, axis=0, keepdims=True) + EPS)
    v = scv_ref[0, 0] * wv_ref[0]
    for j in range(1, first):
        v = v + scv_ref[0, j] * wv_ref[j]
    v = _silu(v + seg(COL_C_QKV + 2 * DN_QK) * wv_ref[first])
    hi = lax.broadcasted_iota(jnp.int32, (DN_HEADS, LANES), 0)
    li = lax.broadcasted_iota(jnp.int32, (DN_HEADS, LANES), 1)
    ba = jnp.broadcast_to(ba_ref[0], (DN_HEADS, LANES))
    beta = jax.nn.sigmoid(jnp.sum(jnp.where(li == hi, ba, 0.0), axis=-1, keepdims=True))
    z = jnp.sum(jnp.where(li == hi + DN_HEADS, ba, 0.0), axis=-1, keepdims=True) + dtb_ref[...]
    gdec = jnp.exp(-jnp.exp(al_ref[...]) * (jnp.maximum(z, 0.0) + jnp.log1p(jnp.exp(-jnp.abs(z)))))
    gate = seg(COL_C_GATE)
    for h in range(DN_HEADS):
        qc = qk[:, h:h + 1] * (DN_DK ** -0.5)
        kc = qk[:, DN_HEADS + h:DN_HEADS + h + 1]
        eg = gdec[h:h + 1, :]
        s0 = s0_ref[0, h]
        u = beta[h:h + 1, :] * (v[h:h + 1, :] - eg * jnp.sum(kc * s0, axis=0, keepdims=True))
        s1 = eg * s0 + kc * u
        s_ref[0, h] = s1
        o = jnp.sum(qc * s1, axis=0, keepdims=True)
        yc_ref[0, h:h + 1, :] = _rms_rows(o, og_ref[...]) * _silu(gate[h:h + 1, :])


def _sample_mix(proj2, layer, conv_bufs, kv_bufs, sc_buf, s0s, conv_w, conv_b, ln_g, ln_b, q_norm_g, k_norm_g,
                dconv_w, a_log, dt_bias, o_norm_g):
    b = proj2.shape[0]
    rows = proj2.reshape(b, 1, N_PROJ)
    p3 = proj2.reshape(b, N_PROJ // LANES, LANES)
    nq = 2 * DN_QK
    qkt = proj2[:, COL_C_QKV:COL_C_QKV + nq].reshape(b, 2 * DN_HEADS, DN_DK).transpose(0, 2, 1)
    sct = sc_buf[:, :, :nq].reshape(b, DN_SHORT_K - 1, 2 * DN_HEADS, DN_DK).transpose(0, 1, 3, 2)
    wt = dconv_w[:, :nq].reshape(DN_SHORT_K, 2 * DN_HEADS, DN_DK).transpose(0, 2, 1)
    scv = sc_buf[:, :, nq:].reshape(b, DN_SHORT_K - 1, DN_HEADS, DN_DV)
    wv = dconv_w[:, nq:].reshape(DN_SHORT_K, DN_HEADS, DN_DV)
    kv_views, kv_specs = [], []
    for kv, (window, d) in zip(kv_bufs, ATT_PATTERNS):
        assert kv.shape[1:] == (b, 2, window, H_PER_GROUP, HEAD_DIM) and window == ATT_SPAN * d
        kv_views.append(kv.reshape(kv.shape[0], b, 2, ATT_SPAN, d, H_PER_GROUP, HEAD_DIM))
        kv_specs.append(pl.BlockSpec((None, 1, 2, ATT_SPAN, None, H_PER_GROUP, HEAD_DIM),
                                     lambda i: (layer, i, 0, 0, 0, 0, 0)))

    def full(shape):
        return pl.BlockSpec(shape, lambda i: (0,) * len(shape))

    def per_b(shape):
        return pl.BlockSpec((1,) + shape, lambda i: (i,) + (0,) * len(shape))

    def per_lb(shape):
        return pl.BlockSpec((None, 1) + shape, lambda i: (layer, i) + (0,) * len(shape))

    col = lambda x: x.reshape(-1, 1)
    row = lambda x: x.reshape(1, -1)
    tile_out = jax.ShapeDtypeStruct((b, BLK, LANES), F32)
    new_out = jax.ShapeDtypeStruct((b, 2, H_PER_GROUP, HEAD_DIM), F32)
    outs = pl.pallas_call(
        _sample_mix_body,
        grid=(b,),
        in_specs=[per_b((1, N_PROJ)), per_b((N_PROJ // LANES, LANES)),
                  pl.BlockSpec((1, 1, LANES), lambda i: (i, 0, COL_BETA // LANES)),
                  per_lb((CONV_K - 1, CONV_CH)), full((CONV_K, CONV_CH)),
                  full((1, CONV_CH)), full((1, CONV_CH)), full((1, CONV_CH)),
                  *kv_specs, full((1, HEAD_DIM)), full((1, HEAD_DIM)),
                  per_b((DN_DK, 2 * DN_HEADS)), per_b((DN_SHORT_K - 1, DN_DK, 2 * DN_HEADS)),
                  full((DN_SHORT_K, DN_DK, 2 * DN_HEADS)),
                  per_b((DN_SHORT_K - 1, DN_HEADS, DN_DV)), full((DN_SHORT_K, DN_HEADS, DN_DV)),
                  per_lb((DN_HEADS, DN_DK, DN_DV)), full((DN_HEADS, 1)), full((DN_HEADS, 1)), full((1, DN_DV))],
        out_specs=[per_b((1, CONV_CH)), per_b((1, CONV_CH)), per_b((BLK, LANES)), per_b((BLK, LANES)),
                   per_b((2, H_PER_GROUP, HEAD_DIM)), per_b((2, H_PER_GROUP, HEAD_DIM)),
                   per_b((2, H_PER_GROUP, HEAD_DIM)), per_b((DN_HEADS, DN_DK, DN_DV))],
        out_shape=[jax.ShapeDtypeStruct((b, 1, CONV_CH), F32), jax.ShapeDtypeStruct((b, 1, CONV_CH), F32),
                   tile_out, tile_out, new_out, new_out, new_out,
                   jax.ShapeDtypeStruct((b, DN_HEADS, DN_DK, DN_DV), F32)],
        compiler_params=pltpu.CompilerParams(
            dimension_semantics=("parallel",), vmem_limit_bytes=V7X_VMEM_LIMIT_BYTES),
        name="sample_mix",
    )(rows, p3, rows, conv_bufs, conv_w, row(conv_b), row(ln_g), row(ln_b),
      *kv_views, row(q_norm_g), row(k_norm_g),
      qkt, sct, wt, scv, wv, s0s, col(a_log), col(dt_bias), row(o_norm_g))
    ya, glu, yb, yc, n0, n1, n2, s_new = outs
    return (ya.reshape(b, CONV_CH), glu, yb.reshape(b, ATT_OUT), yc.reshape(b, DN_OUT), (n0, n1, n2), s_new)


SHIFT_ROWS = 512


def _cache_shift_body(cur_ref, nxt_ref, new_ref, o_ref):
    r = cur_ref.shape[3]
    o_ref[0, 0, 0, 0:r - 1] = cur_ref[0, 0, 0, 1:r]
    is_last = pl.program_id(3) == pl.num_programs(3) - 1
    o_ref[0, 0, 0, r - 1] = jnp.where(is_last, new_ref[0, 0, 0], nxt_ref[0, 0, 0, 0])


def _cache_shift(cache, new_rows):
    nl, b, _, lb, h, dh = cache.shape
    r = min(SHIFT_ROWS, lb)
    assert lb % r == 0
    nchunk = lb // r
    return pl.pallas_call(
        _cache_shift_body,
        grid=(nl, b, 2, nchunk),
        in_specs=[pl.BlockSpec((1, 1, 1, r, h, dh), lambda l, i, k, c: (l, i, k, c, 0, 0)),
                  pl.BlockSpec((1, 1, 1, 1, h, dh),
                               lambda l, i, k, c: (l, i, k, jnp.minimum((c + 1) * r, lb - 1), 0, 0)),
                  pl.BlockSpec((1, 1, 1, h, dh), lambda l, i, k, c: (l, i, k, 0, 0))],
        out_specs=pl.BlockSpec((1, 1, 1, r, h, dh), lambda l, i, k, c: (l, i, k, c, 0, 0)),
        out_shape=jax.ShapeDtypeStruct(cache.shape, cache.dtype),
        compiler_params=pltpu.CompilerParams(
            dimension_semantics=("parallel", "parallel", "parallel", "arbitrary"),
            vmem_limit_bytes=V7X_VMEM_LIMIT_BYTES),
        name="cache_shift",
    )(cache, cache, new_rows)


def _rms_norm(x, g):
    y = x * lax.rsqrt(jnp.mean(x * x, axis=-1, keepdims=True) + EPS)
    return y * g


def _layer_norm(x, g, b):
    xc = x - jnp.mean(x, axis=-1, keepdims=True)
    y = xc * lax.rsqrt(jnp.mean(xc * xc, axis=-1, keepdims=True) + EPS)
    return y * g + b


def _l2_norm(x):
    return x * lax.rsqrt(jnp.sum(x * x, axis=-1, keepdims=True) + EPS)


def _causal_depthwise_conv(x, buf, w):
    xc = jnp.concatenate([buf, x], axis=1)
    y = lax.conv_general_dilated(xc, w[:, None, :], window_strides=(1,), padding='VALID',
                                 dimension_numbers=('NWC', 'WIO', 'NWC'), feature_group_count=x.shape[-1])
    return y, xc[:, xc.shape[1] - (w.shape[0] - 1):]


def _dilated_attn_prompt(q, k, v, window, dil):
    B, S, H, Dh = q.shape
    span = window // dil
    L = S // dil
    nb = -(-L // ATT_BLOCK)
    Lp = nb * ATT_BLOCK

    def by_residue(t):
        t = t.reshape(B, L, dil, H, Dh).transpose(0, 2, 1, 3, 4)
        t = jnp.pad(t, ((0, 0), (0, 0), (0, Lp - L), (0, 0), (0, 0)))
        return t.reshape(B, dil, nb, ATT_BLOCK, H, Dh)

    def with_prev_block(t):
        prev = jnp.pad(t, ((0, 0), (0, 0), (1, 0), (0, 0), (0, 0), (0, 0)))[:, :, :nb]
        return jnp.concatenate([prev, t], axis=3)

    qb = by_residue(q)
    kb = with_prev_block(by_residue(k))
    vb = with_prev_block(by_residue(v))
    s = jnp.einsum('brnqhd,brnkhd->brnhqk', qb, kb, preferred_element_type=F32) * (Dh ** -0.5)
    qi = jnp.arange(ATT_BLOCK)[:, None]
    kj = jnp.arange(2 * ATT_BLOCK)[None, :]
    dist = ATT_BLOCK + qi - kj
    kpos = (jnp.arange(nb)[:, None, None] - 1) * ATT_BLOCK + kj
    mask = (dist >= 0) & (dist <= span) & (kpos >= 0)
    s = jnp.where(mask[:, None], s, NEG_INF)
    m = jnp.max(s, axis=-1, keepdims=True)
    p = jnp.exp(s - m)
    l = jnp.sum(p, axis=-1, keepdims=True)
    o = jnp.einsum('brnhqk,brnkhd->brnqhd', p, vb) / jnp.swapaxes(l, 3, 4)
    lse = jnp.swapaxes((m + jnp.log(l))[..., 0], 3, 4)
    o = o.reshape(B, dil, Lp, H, Dh)[:, :, :L].transpose(0, 2, 1, 3, 4).reshape(B, S, H, Dh)
    lse = lse.reshape(B, dil, Lp, H)[:, :, :L].transpose(0, 2, 1, 3).reshape(B, S, H)
    return o, lse


def _dilated_attn_cached(q, k, v, kv_buf, window, dil):
    B, T, H, Dh = q.shape
    Lb = kv_buf.shape[2]
    span = window // dil
    kc = jnp.concatenate([kv_buf[:, 0], k], axis=1)
    vc = jnp.concatenate([kv_buf[:, 1], v], axis=1)
    idx = Lb + jnp.arange(T)[:, None] - dil * jnp.arange(span + 1)[None, :]
    valid = idx >= 0
    idx = jnp.maximum(idx, 0)
    kg = kc[:, idx]
    vg = vc[:, idx]
    s = jnp.einsum('bthd,btmhd->bthm', q, kg, preferred_element_type=F32) * (Dh ** -0.5)
    s = jnp.where(valid[:, None, :], s, NEG_INF)
    m = jnp.max(s, axis=-1, keepdims=True)
    p = jnp.exp(s - m)
    l = jnp.sum(p, axis=-1, keepdims=True)
    o = jnp.einsum('bthm,btmhd->bthd', p, vg) / l
    lse = (m + jnp.log(l))[..., 0]
    start = kc.shape[1] - Lb
    new_buf = jnp.stack([kc[:, start:], vc[:, start:]], axis=1)
    return o, lse, new_buf


def _gated_delta_rule(q, k, v, beta, g, s0):
    B, T, H, dk = q.shape
    dv = v.shape[-1]
    C = DN_CHUNK
    n = -(-T // C)
    pad = n * C - T

    def chunked(t):
        t = jnp.pad(t, ((0, 0), (0, pad)) + ((0, 0),) * (t.ndim - 2))
        t = t.reshape((B, n, C) + t.shape[2:])
        return t.transpose((1, 0, 3, 2) + tuple(range(4, t.ndim)))

    qc, kc, vc, bc, gc = (chunked(t) for t in (q, k, v, beta, g))
    G = jnp.cumsum(gc, axis=-1)
    causal = jnp.tril(jnp.ones((C, C), dtype=bool))
    decay = jnp.exp(jnp.where(causal, G[..., :, None] - G[..., None, :], -jnp.inf))
    kb = kc * bc[..., None]
    strict = causal & ~jnp.eye(C, dtype=bool)
    a = jnp.where(strict, jnp.einsum('nbhid,nbhjd->nbhij', kb, kc) * decay, 0.0)
    rhs = jnp.concatenate([vc * bc[..., None], kb * jnp.exp(G)[..., None]], axis=-1)
    sol = lax.linalg.triangular_solve(a, rhs, left_side=True, lower=True, unit_diagonal=True)
    u0, w = sol[..., :dv], sol[..., dv:]
    qk = jnp.einsum('nbhid,nbhjd->nbhij', qc, kc) * decay
    q_dec = qc * jnp.exp(G)[..., None]
    g_last = G[..., -1]
    k_dec = kc * jnp.exp(g_last[..., None] - G)[..., None]

    def step(S, xs):
        u0_i, w_i, qk_i, qd_i, kd_i, gl_i = xs
        u = u0_i - jnp.einsum('bhck,bhkv->bhcv', w_i, S)
        o = jnp.einsum('bhck,bhkv->bhcv', qd_i, S) + jnp.einsum('bhij,bhjv->bhiv', qk_i, u)
        S = jnp.exp(gl_i)[..., None, None] * S + jnp.einsum('bhck,bhcv->bhkv', kd_i, u)
        return S, o

    s_fin, o = lax.scan(step, s0, (u0, w, qk, q_dec, k_dec, g_last))
    o = o.transpose(1, 0, 3, 2, 4).reshape(B, n * C, H, dv)[:, :T]
    return o, s_fin


def _gated_deltanet(c_qkv, c_gate, c_beta, c_alpha, sc_buf, s0, dconv_w, a_log, dt_bias, o_norm_g):
    B, T, _ = c_qkv.shape
    y, sc_new = _causal_depthwise_conv(c_qkv, sc_buf, dconv_w)
    y = jax.nn.silu(y)
    q, k, v = jnp.split(y, [DN_QK, 2 * DN_QK], axis=-1)
    q = _l2_norm(q.reshape(B, T, DN_HEADS, DN_DK)) * (DN_DK ** -0.5)
    k = _l2_norm(k.reshape(B, T, DN_HEADS, DN_DK))
    v = v.reshape(B, T, DN_HEADS, DN_DV)
    beta = jax.nn.sigmoid(c_beta)
    g = -jnp.exp(a_log) * jax.nn.softplus(c_alpha + dt_bias)
    o, s_new = _gated_delta_rule(q, k, v, beta, g, s0)
    o = _rms_norm(o, o_norm_g) * jax.nn.silu(c_gate.reshape(B, T, DN_HEADS, DN_DV))
    return o.reshape(B, T, DN_OUT), sc_new, s_new


MERGE_TM = 1024
MERGE_TN = 256
assert COL_MLOG % MERGE_TN == 0 and D_MODEL % MERGE_TN == 0


def _merge_body(ya_ref, yb_ref, yc_ref, ga_ref, gb_ref, gc_ref, w_ref, o_ref, lhs_ref):
    @pl.when(pl.program_id(1) == 0)
    def _():
        for b, y_ref in enumerate((ya_ref, yb_ref, yc_ref)):
            lhs_ref[b] = y_ref[...].astype(BF16)

    acc = None
    for b, g_ref in enumerate((ga_ref, gb_ref, gc_ref)):
        part = jax.nn.sigmoid(g_ref[...]) * jnp.dot(
            lhs_ref[b], w_ref[b * CONV_CH:(b + 1) * CONV_CH, :], preferred_element_type=F32)
        acc = part if acc is None else acc + part
    o_ref[...] = acc


def _branch_merge(ya, yb, yc, proj, w_branch, layer):
    assert CONV_CH == ATT_OUT == DN_OUT
    m = ya.shape[0]
    tm = min(MERGE_TM, m)
    y_spec = pl.BlockSpec((tm, CONV_CH), lambda i, j: (i, 0))

    def gate_spec(b):
        blk0 = (COL_MLOG + b * D_MODEL) // MERGE_TN
        return pl.BlockSpec((tm, MERGE_TN), lambda i, j: (i, blk0 + j))

    return pl.pallas_call(
        _merge_body,
        grid=(_cdiv(m, tm), D_MODEL // MERGE_TN),
        in_specs=[y_spec, y_spec, y_spec, gate_spec(0), gate_spec(1), gate_spec(2),
                  _layer_w_spec(layer, N_BRANCH * CONV_CH, MERGE_TN)],
        out_specs=pl.BlockSpec((tm, MERGE_TN), lambda i, j: (i, j)),
        out_shape=jax.ShapeDtypeStruct((m, D_MODEL), F32),
        scratch_shapes=[pltpu.VMEM((N_BRANCH, tm, CONV_CH), BF16)],
        compiler_params=pltpu.CompilerParams(
            dimension_semantics=("parallel", "arbitrary"), vmem_limit_bytes=V7X_VMEM_LIMIT_BYTES),
        name="branch_merge",
    )(ya, yb, yc, proj, proj, proj, w_branch)


def _prompt_layer(x2, layer, caches, cache_bufs, norm_g, w_in_r, conv_w, conv_b, ln_g, ln_b, q_norm_g, k_norm_g,
                  dconv_w, a_log, dt_bias, o_norm_g, w_branch, w_out):
    t = x2.shape[0]
    proj, cache_bufs = _proj_shift(x2, w_in_r, norm_g, layer, caches, cache_bufs, tm=1024, tn=PROJ_TN)
    ya, conv_new = _conv_a_prompt(proj, conv_w, conv_b, ln_g, ln_b)
    yb, kv0, kv1, kv2 = _attn_prompt(proj, q_norm_g, k_norm_g)
    yc, s_new = _deltanet_prompt(proj, dconv_w, a_log, dt_bias, o_norm_g)
    y = _branch_merge(ya, yb, yc, proj, w_branch, layer)
    out = _proj(y, w_out, layer, resid=x2, tm=1024, tn=512)
    sc_new = proj[t - (DN_SHORT_K - 1):, COL_C_QKV:COL_C_QKV + DN_QKV]
    kv_new = [kv.reshape(1, 2, w, H_PER_GROUP, HEAD_DIM) for kv, (w, _) in zip((kv0, kv1, kv2), ATT_PATTERNS)]
    return out, conv_new[None], kv_new, sc_new[None], s_new[None], cache_bufs


def _sample_layer(x, layer, conv_bufs, kv_bufs, sc_buf, s0s, norm_g, w_in_r, conv_w, conv_b, ln_g, ln_b,
                  q_norm_g, k_norm_g, dconv_w, a_log, dt_bias, o_norm_g, w_branch, w_out):
    B, T, _ = x.shape
    assert T == 1
    x2 = x.reshape(B, D_MODEL)
    proj2 = _proj(x2, w_in_r, layer, gain=norm_g, tm=B, tn=PROJ_TN)
    ya, glu, yb, yc, kv_new, s_new = _sample_mix(proj2, layer, conv_bufs, kv_bufs, sc_buf, s0s, conv_w, conv_b,
                                                 ln_g, ln_b, q_norm_g, k_norm_g, dconv_w, a_log, dt_bias, o_norm_g)
    y = _branch_merge(ya, yb, yc, proj2, w_branch, layer)
    out = _proj(y, w_out, layer, resid=x2, tm=B, tn=512).reshape(B, T, D_MODEL)
    conv_new = jnp.concatenate([conv_bufs[layer, :, 1:], glu], axis=1)
    sc_new = jnp.concatenate([sc_buf[:, 1:], proj2[:, None, COL_C_QKV:COL_C_QKV + DN_QKV]], axis=1)
    return out, conv_new, kv_new, sc_new, s_new


def kernel(x_prompt, x_sample, state_conv, cache_kv_w128, cache_kv_w512, cache_kv_w2048, state_short_conv, state_delta, norm_g, w_in, conv_w, conv_b, ln_g, ln_b, q_norm_g, k_norm_g, dconv_w, a_log, dt_bias, o_norm_g, w_branch, w_out):
    bp, t, _ = x_prompt.shape
    assert bp == 1
    xp, xs = x_prompt.reshape(t, D_MODEL), x_sample
    conv_p, conv_s, sc_p, sc_s, d_p, d_s = [], [], [], [], [], []
    kv_p = [[], [], []]
    kv_s = [[], [], []]
    caches = (cache_kv_w128, cache_kv_w512, cache_kv_w2048)
    cache_bufs = None
    w_in_r, w_branch_b, w_out_b = _prep_w_in(w_in), w_branch.astype(BF16), w_out.astype(BF16)
    for l in range(DEPTH):
        wts = (norm_g[l], w_in_r, conv_w[l], conv_b[l], ln_g[l], ln_b[l], q_norm_g[l], k_norm_g[l],
               dconv_w[l], a_log[l], dt_bias[l], o_norm_g[l], w_branch_b, w_out_b)
        xp, c_new, kv_new, sc_new, s_new, cache_bufs = _prompt_layer(xp, l, caches, cache_bufs, *wts)
        conv_p.append(c_new)
        sc_p.append(sc_new)
        d_p.append(s_new)
        for gi in range(N_ATT_GROUPS):
            kv_p[gi].append(kv_new[gi])
        xs, c_new, kv_new, sc_new, s_new = _sample_layer(
            xs, l, state_conv, (cache_kv_w128, cache_kv_w512, cache_kv_w2048),
            state_short_conv[l], state_delta, *wts)
        conv_s.append(c_new)
        sc_s.append(sc_new)
        d_s.append(s_new)
        for gi in range(N_ATT_GROUPS):
            kv_s[gi].append(kv_new[gi])
    kv_s = [[_cache_append(cache_bufs[gi], jnp.stack(kv_s[gi]))] for gi in range(N_ATT_GROUPS)]
    return (xp.reshape(bp, t, D_MODEL), xs,
            jnp.stack(conv_p), jnp.stack(conv_s),
            jnp.stack(kv_p[0]), kv_s[0][0],
            jnp.stack(kv_p[1]), kv_s[1][0],
            jnp.stack(kv_p[2]), kv_s[2][0],
            jnp.stack(sc_p), jnp.stack(sc_s),
            jnp.stack(d_p), jnp.stack(d_s))
```

```python
import functools
import math

import jax
import jax.numpy as jnp
import numpy as np
from jax import lax
from jax.experimental import pallas as pl
from jax.experimental.pallas import tpu as pltpu

D_MODEL = 2048
DEPTH = 2
CONV_CH = 1024
CONV_K = 31
ATT_PATTERNS = ((128, 1), (512, 4), (2048, 16))
N_ATT_GROUPS = 3
H_PER_GROUP = 8
HEAD_DIM = 128
ATT_QKV = N_ATT_GROUPS * H_PER_GROUP * HEAD_DIM
ATT_OUT = H_PER_GROUP * HEAD_DIM
ATT_BLOCK = 128
DN_HEADS = 8
DN_DK = 128
DN_DV = 128
DN_QK = DN_HEADS * DN_DK
DN_QKV = DN_HEADS * (2 * DN_DK + DN_DV)
DN_OUT = DN_HEADS * DN_DV
DN_SHORT_K = 4
DN_CHUNK = 64
N_BRANCH = 3
IN_SPLITS = (CONV_CH, CONV_CH, CONV_CH, ATT_QKV, ATT_QKV, ATT_QKV, ATT_OUT,
             DN_QKV, DN_OUT, DN_HEADS, DN_HEADS, N_BRANCH * D_MODEL)
D_IN = sum(IN_SPLITS)
EPS = 1e-6
NEG_INF = -1e30
F32 = jnp.float32
BF16 = jnp.bfloat16

V7X_VMEM_LIMIT_BYTES = 56 * 1024 * 1024


def _cdiv(a, b):
    return -(-a // b)


def _mm(lhs, w, w_t):
    if w_t:
        return lax.dot_general(lhs, w, (((1,), (1,)), ((), ())), preferred_element_type=F32)
    return jnp.dot(lhs, w, preferred_element_type=F32)


def _proj_body(*refs, normalize, residual):
    a_ref, g_ref, w_ref = refs[:3]
    r_ref = refs[3] if residual else None
    o_ref, lhs_ref = refs[-2:]

    @pl.when(pl.program_id(1) == 0)
    def _():
        a = a_ref[...]
        if normalize:
            a = a * lax.rsqrt(jnp.mean(a * a, axis=-1, keepdims=True) + EPS) * g_ref[...]
        lhs_ref[...] = a.astype(BF16)

    acc = _mm(lhs_ref[...], w_ref[...], False)
    if residual:
        acc = acc + r_ref[...]
    o_ref[...] = acc


def _layer_w_spec(layer, k, tn, w_t=False):
    if w_t:
        return pl.BlockSpec((None, tn, k), lambda i, j: (layer, j, 0))
    return pl.BlockSpec((None, k, tn), lambda i, j: (layer, 0, j))


def _proj(a, w, layer, gain=None, resid=None, *, tm, tn):
    m, k = a.shape
    n = w.shape[2]
    tm = min(tm, m)
    tn = min(tn, n)
    normalize = gain is not None
    if gain is None:
        gain = jnp.ones((k,), F32)
    in_specs = [
        pl.BlockSpec((tm, k), lambda i, j: (i, 0)),
        pl.BlockSpec((1, k), lambda i, j: (0, 0)),
        _layer_w_spec(layer, k, tn),
    ]
    args = [a, gain.reshape(1, k), w]
    if resid is not None:
        in_specs.append(pl.BlockSpec((tm, tn), lambda i, j: (i, j)))
        args.append(resid)
    return pl.pallas_call(
        functools.partial(_proj_body, normalize=normalize, residual=resid is not None),
        grid=(_cdiv(m, tm), _cdiv(n, tn)),
        in_specs=in_specs,
        out_specs=pl.BlockSpec((tm, tn), lambda i, j: (i, j)),
        out_shape=jax.ShapeDtypeStruct((m, n), F32),
        scratch_shapes=[pltpu.VMEM((tm, k), BF16)],
        compiler_params=pltpu.CompilerParams(
            dimension_semantics=("parallel", "arbitrary"),
            vmem_limit_bytes=V7X_VMEM_LIMIT_BYTES),
        name="proj",
    )(*args)


def _proj_few_rows_body(a_ref, g_ref, w_ref, o_ref, lhs_ref):
    @pl.when(pl.program_id(0) == 0)
    def _():
        a = a_ref[...]
        lhs_ref[...] = (a * lax.rsqrt(jnp.mean(a * a, axis=-1, keepdims=True) + EPS) * g_ref[...]).astype(BF16)

    o_ref[...] = _mm(w_ref[...], lhs_ref[...], True)


def _proj_few_rows(a, w, layer, gain, *, tn):
    b, k = a.shape
    n = w.shape[1]
    assert n % tn == 0
    out_t = pl.pallas_call(
        _proj_few_rows_body,
        grid=(n // tn,),
        in_specs=[pl.BlockSpec((b, k), lambda j: (0, 0)),
                  pl.BlockSpec((1, k), lambda j: (0, 0)),
                  pl.BlockSpec((None, tn, k), lambda j: (layer, j, 0))],
        out_specs=pl.BlockSpec((tn, b), lambda j: (j, 0)),
        out_shape=jax.ShapeDtypeStruct((n, b), F32),
        scratch_shapes=[pltpu.VMEM((b, k), BF16)],
        compiler_params=pltpu.CompilerParams(
            dimension_semantics=("arbitrary",), vmem_limit_bytes=V7X_VMEM_LIMIT_BYTES),
        name="proj_few_rows",
    )(a, gain.reshape(1, k), w)
    return out_t.T


COPY_PARTS = 3


def _shift_copies(layer, caches, outs, stage, sems, pair, part, to_vmem):
    b, kv = pair // 2, pair % 2
    w0, w1, w2 = (c.shape[3] for c in caches)
    half = w2 // 2

    def mk(g, src0, rows, dst0, slot0, si):
        buf = stage.at[part, pl.ds(slot0, rows)]
        if to_vmem:
            return pltpu.make_async_copy(caches[g].at[layer, b, kv, pl.ds(src0, rows)], buf, sems.at[0, part, si])
        return pltpu.make_async_copy(buf, outs[g].at[layer, b, kv, pl.ds(dst0, rows)], sems.at[1, part, si])

    if part == 0:
        return [mk(2, 1, half, 0, 0, 0)]
    if part == 1:
        return [mk(2, 1 + half, w2 - 1 - half, half, 0, 0)]
    return [mk(1, 1, w1 - 1, 0, 0, 0), mk(0, 1, w0 - 1, 0, w1, 1)]


def _proj_shift_body(*refs, layer, n_slabs):
    a_ref, g_ref, w_ref = refs[:3]
    caches = refs[3:6]
    o_ref = refs[-7]
    outs = refs[-6:-3]
    lhs_ref, stage, sems = refs[-3:]
    step = pl.program_id(0) * pl.num_programs(1) + pl.program_id(1)

    def slab_ops(slab, fn):
        for part in range(COPY_PARTS):
            @pl.when((slab >= 0) & (slab < n_slabs) & (slab % COPY_PARTS == part))
            def _():
                fn(slab // COPY_PARTS, part)

    def start_in(pair, part):
        for c in _shift_copies(layer, caches, outs, stage, sems, pair, part, True):
            c.start()

    def turn_around(pair, part):
        for c in _shift_copies(layer, caches, outs, stage, sems, pair, part, True):
            c.wait()
        for c in _shift_copies(layer, caches, outs, stage, sems, pair, part, False):
            c.start()

    def finish(pair, part):
        for c in _shift_copies(layer, caches, outs, stage, sems, pair, part, False):
            c.wait()

    slab_ops(step - 2, finish)
    slab_ops(step - 1, turn_around)
    slab_ops(step, start_in)

    @pl.when(pl.program_id(1) == 0)
    def _():
        a = a_ref[...]
        lhs_ref[...] = (a * lax.rsqrt(jnp.mean(a * a, axis=-1, keepdims=True) + EPS) * g_ref[...]).astype(BF16)

    o_ref[...] = _mm(lhs_ref[...], w_ref[...], True)


def _proj_shift(a, w, gain, layer, caches, bufs, *, tm, tn):
    m, k = a.shape
    n = w.shape[1]
    assert m % tm == 0 and n % tn == 0
    grid = (m // tm, n // tn)
    nb = caches[0].shape[1]
    n_slabs = nb * 2 * COPY_PARTS
    assert n_slabs + 2 <= grid[0] * grid[1]
    _, _, _, w2, heads, dh = caches[2].shape
    assert caches[1].shape[3] + caches[0].shape[3] <= w2 // 2
    any_spec = pl.BlockSpec(memory_space=pl.ANY)
    in_specs = [pl.BlockSpec((tm, k), lambda i, j: (i, 0)),
                pl.BlockSpec((1, k), lambda i, j: (0, 0)),
                _layer_w_spec(layer, k, tn, True),
                any_spec, any_spec, any_spec]
    args = [a, gain.reshape(1, k), w, *caches]
    aliases = {}
    if bufs is not None:
        in_specs += [any_spec] * 3
        aliases = {len(args) + g: 1 + g for g in range(3)}
        args += list(bufs)
    outs = pl.pallas_call(
        functools.partial(_proj_shift_body, layer=layer, n_slabs=n_slabs),
        grid=grid,
        in_specs=in_specs,
        out_specs=[pl.BlockSpec((tm, tn), lambda i, j: (i, j)), any_spec, any_spec, any_spec],
        out_shape=[jax.ShapeDtypeStruct((m, n), F32)] + [jax.ShapeDtypeStruct(c.shape, c.dtype) for c in caches],
        scratch_shapes=[pltpu.VMEM((tm, k), BF16),
                        pltpu.VMEM((COPY_PARTS, w2 // 2, heads, dh), F32),
                        pltpu.SemaphoreType.DMA((2, COPY_PARTS, 2))],
        input_output_aliases=aliases,
        compiler_params=pltpu.CompilerParams(
            dimension_semantics=("arbitrary", "arbitrary"),
            vmem_limit_bytes=V7X_VMEM_LIMIT_BYTES),
        name="proj_shift",
    )(*args)
    return outs[0], tuple(outs[1:])


def _cache_append_body(buf_ref, new_ref, o_ref):
    del buf_ref
    o_ref[0, :, :, 0] = new_ref[0]


def _cache_append(buf, new_rows):
    nl, b, _, lb, h, dh = buf.shape
    return pl.pallas_call(
        _cache_append_body,
        grid=(nl,),
        in_specs=[pl.BlockSpec(memory_space=pl.ANY),
                  pl.BlockSpec((1, b, 2, h, dh), lambda l: (l, 0, 0, 0, 0))],
        out_specs=pl.BlockSpec((1, b, 2, 1, h, dh), lambda l: (l, 0, 0, lb - 1, 0, 0)),
        out_shape=jax.ShapeDtypeStruct(buf.shape, buf.dtype),
        input_output_aliases={0: 0},
        compiler_params=pltpu.CompilerParams(
            dimension_semantics=("arbitrary",), vmem_limit_bytes=V7X_VMEM_LIMIT_BYTES),
        name="cache_append",
    )(buf, new_rows)


LANES = 128
COL_A_VAL = 0
COL_A_GLU = CONV_CH
COL_A_GATE = 2 * CONV_CH
COL_Q = 3 * CONV_CH
COL_K = COL_Q + ATT_QKV
COL_V = COL_K + ATT_QKV
COL_B_GATE = COL_V + ATT_QKV
COL_C_QKV = COL_B_GATE + ATT_OUT
COL_C_GATE = COL_C_QKV + DN_QKV
COL_MLOG = COL_C_GATE + DN_OUT
COL_BETA = COL_MLOG + N_BRANCH * D_MODEL
COL_ALPHA = COL_BETA + DN_HEADS
W_PREP_TN = 256
N_PROJ = _cdiv(D_IN, W_PREP_TN) * W_PREP_TN
PROJ_TN = 768
assert N_PROJ % PROJ_TN == 0
_SRC_BETA = COL_MLOG
_BA = 2 * DN_HEADS
_T_MLOG = COL_MLOG // W_PREP_TN
_T_BETA = COL_BETA // W_PREP_TN
assert COL_MLOG % W_PREP_TN == 0 and COL_BETA % W_PREP_TN == 0 and _T_BETA == N_PROJ // W_PREP_TN - 1
assert _BA % 16 == 0 and D_IN % _BA == 0


def _prep_w_in_body(src_ref, nxt_ref, ba_ref, o_ref):
    t = pl.program_id(1)
    k = src_ref.shape[1]

    @pl.when(t < _T_MLOG)
    def _():
        o_ref[...] = src_ref[...].astype(BF16)

    @pl.when((t >= _T_MLOG) & (t < _T_BETA))
    def _():
        o_ref[0:W_PREP_TN - _BA, :] = src_ref[_BA:W_PREP_TN, :].astype(BF16)
        o_ref[W_PREP_TN - _BA:W_PREP_TN, :] = nxt_ref[...].astype(BF16)

    @pl.when(t == _T_BETA)
    def _():
        o_ref[0:_BA, :] = ba_ref[...].astype(BF16)
        o_ref[_BA:W_PREP_TN, :] = jnp.zeros((W_PREP_TN - _BA, k), BF16)


def _prep_w_in(w_in):
    nl, k, d_in = w_in.shape
    assert d_in == D_IN
    w_t = jnp.swapaxes(w_in, 1, 2)
    per = W_PREP_TN // _BA
    last_src = (COL_BETA + _BA) // W_PREP_TN - 1
    last_nxt = D_IN // _BA - 1
    return pl.pallas_call(
        _prep_w_in_body,
        grid=(nl, N_PROJ // W_PREP_TN),
        in_specs=[pl.BlockSpec((None, W_PREP_TN, k), lambda l, t: (l, jnp.minimum(t, last_src), 0)),
                  pl.BlockSpec((None, _BA, k), lambda l, t: (l, jnp.minimum((t + 1) * per, last_nxt), 0)),
                  pl.BlockSpec((None, _BA, k), lambda l, t: (l, _SRC_BETA // _BA, 0))],
        out_specs=pl.BlockSpec((None, W_PREP_TN, k), lambda l, t: (l, t, 0)),
        out_shape=jax.ShapeDtypeStruct((nl, N_PROJ, k), BF16),
        compiler_params=pltpu.CompilerParams(
            dimension_semantics=("parallel", "arbitrary"), vmem_limit_bytes=V7X_VMEM_LIMIT_BYTES),
        name="w_prep",
    )(w_t, w_t, w_t)


ATT_SB = ATT_BLOCK * max(d for _, d in ATT_PATTERNS)
ATT_SPAN = ATT_BLOCK
assert all(w // d == ATT_SPAN for w, d in ATT_PATTERNS)


def _rows(start, size, stride):
    return pl.ds(start, size) if stride == 1 else pl.ds(start, size, stride=stride)


def _rms_rows(x, gain):
    return x * lax.rsqrt(jnp.mean(x * x, axis=-1, keepdims=True) + EPS) * gain


def _attn_prompt_body(q0, q1, q2, k0, k1, k2, v0, v1, v2, gate_ref, qg_ref, kg_ref,
                      yb_ref, kv0, kv1, kv2,
                      kr0, kr1, kr2, vr0, vr1, vr2, on0, on1, on2, ln0, ln1, ln2):
    q_refs, k_refs, v_refs = (q0, q1, q2), (k0, k1, k2), (v0, v1, v2)
    kv_refs = (kv0, kv1, kv2)
    kres, vres = (kr0, kr1, kr2), (vr0, vr1, vr2)
    onat, lnat = (on0, on1, on2), (ln0, ln1, ln2)
    i = pl.program_id(1)
    last = pl.num_programs(1) - 1
    qg = qg_ref[...]
    kg = kg_ref[...]
    scale = HEAD_DIM ** -0.5
    qi = lax.broadcasted_iota(jnp.int32, (ATT_BLOCK, 2 * ATT_BLOCK), 0)
    kj = lax.broadcasted_iota(jnp.int32, (ATT_BLOCK, 2 * ATT_BLOCK), 1)
    delta = kj - qi
    band = (delta >= 0) & (delta <= ATT_SPAN)

    for g, (_, d) in enumerate(ATT_PATTERNS):
        nblk = ATT_SB // (ATT_BLOCK * d)
        units = ATT_BLOCK * nblk

        @pl.when(i == 0)
        def _():
            kres[g][:, 0:ATT_BLOCK, :] = jnp.zeros((d, ATT_BLOCK, HEAD_DIM), BF16)
            vres[g][:, 0:ATT_BLOCK, :] = jnp.zeros((d, ATT_BLOCK, HEAD_DIM), BF16)

        @pl.when(i > 0)
        def _():
            kres[g][:, 0:ATT_BLOCK, :] = kres[g][:, units:units + ATT_BLOCK, :]
            vres[g][:, 0:ATT_BLOCK, :] = vres[g][:, units:units + ATT_BLOCK, :]

        for r in range(d):
            for c in range(nblk):
                rows = _rows(c * ATT_BLOCK * d + r, ATT_BLOCK, d)
                dst = slice(ATT_BLOCK + c * ATT_BLOCK, ATT_BLOCK + (c + 1) * ATT_BLOCK)
                kres[g][r, dst, :] = _rms_rows(k_refs[g][rows, :], kg).astype(BF16)
                vres[g][r, dst, :] = v_refs[g][rows, :].astype(BF16)

        for r in range(d):
            for c in range(nblk):
                rows = _rows(c * ATT_BLOCK * d + r, ATT_BLOCK, d)
                q = _rms_rows(q_refs[g][rows, :], qg).astype(BF16)
                kb = kres[g][r, c * ATT_BLOCK:(c + 2) * ATT_BLOCK, :]
                vb = vres[g][r, c * ATT_BLOCK:(c + 2) * ATT_BLOCK, :]
                s = lax.dot_general(q, kb, (((1,), (1,)), ((), ())), preferred_element_type=F32) * scale
                if c == 0:
                    first = jnp.where(i == 0, ATT_BLOCK, 0)
                    mask = band & (kj >= first)
                else:
                    mask = band
                s = jnp.where(mask, s, NEG_INF)
                m = jnp.max(s, axis=-1, keepdims=True)
                p = jnp.exp(s - m)
                l = jnp.sum(p, axis=-1, keepdims=True)
                o = jnp.dot(p.astype(BF16), vb, preferred_element_type=F32) / l
                onat[g][rows, :] = o
                lnat[g][rows, :] = jnp.broadcast_to(m + jnp.log(l), (ATT_BLOCK, HEAD_DIM))

    chunk = 256
    for c in range(ATT_SB // chunk):
        rows = slice(c * chunk, (c + 1) * chunk)
        l0, l1, l2 = lnat[0][rows, :], lnat[1][rows, :], lnat[2][rows, :]
        m = jnp.maximum(jnp.maximum(l0, l1), l2)
        e0, e1, e2 = jnp.exp(l0 - m), jnp.exp(l1 - m), jnp.exp(l2 - m)
        y = (e0 * onat[0][rows, :] + e1 * onat[1][rows, :] + e2 * onat[2][rows, :]) / (e0 + e1 + e2)
        gate = gate_ref[rows, :]
        yb_ref[rows, :] = y * (gate * jax.nn.sigmoid(gate))

    @pl.when(i == last)
    def _():
        for g, (window, _) in enumerate(ATT_PATTERNS):
            tail = slice(ATT_SB - window, ATT_SB)
            kv_refs[g][0, :, :] = _rms_rows(k_refs[g][tail, :], kg)
            kv_refs[g][1, :, :] = v_refs[g][tail, :]


def _attn_prompt(proj, q_norm_g, k_norm_g):
    t = proj.shape[0]
    assert t % ATT_SB == 0
    n_sb = t // ATT_SB

    def col_spec(col0, g):
        blk0 = col0 // LANES + g * H_PER_GROUP
        return pl.BlockSpec((ATT_SB, HEAD_DIM), lambda h, i: (i, blk0 + h))

    in_specs = ([col_spec(COL_Q, g) for g in range(N_ATT_GROUPS)]
                + [col_spec(COL_K, g) for g in range(N_ATT_GROUPS)]
                + [col_spec(COL_V, g) for g in range(N_ATT_GROUPS)]
                + [col_spec(COL_B_GATE, 0),
                   pl.BlockSpec((1, HEAD_DIM), lambda h, i: (0, 0)),
                   pl.BlockSpec((1, HEAD_DIM), lambda h, i: (0, 0))])
    out_specs = [pl.BlockSpec((ATT_SB, HEAD_DIM), lambda h, i: (i, h))]
    out_shape = [jax.ShapeDtypeStruct((t, ATT_OUT), F32)]
    scratch = []
    for window, _ in ATT_PATTERNS:
        out_specs.append(pl.BlockSpec((2, window, HEAD_DIM), lambda h, i: (0, 0, h)))
        out_shape.append(jax.ShapeDtypeStruct((2, window, ATT_OUT), F32))
    for _ in range(2):
        for _, d in ATT_PATTERNS:
            scratch.append(pltpu.VMEM((d, ATT_BLOCK + ATT_SB // d, HEAD_DIM), BF16))
    for _ in range(2 * N_ATT_GROUPS):
        scratch.append(pltpu.VMEM((ATT_SB, HEAD_DIM), F32))
    return pl.pallas_call(
        _attn_prompt_body,
        grid=(H_PER_GROUP, n_sb),
        in_specs=in_specs,
        out_specs=out_specs,
        out_shape=out_shape,
        scratch_shapes=scratch,
        compiler_params=pltpu.CompilerParams(
            dimension_semantics=("parallel", "arbitrary"),
            vmem_limit_bytes=V7X_VMEM_LIMIT_BYTES),
        name="attn_prompt",
    )(*([proj] * 10), q_norm_g.reshape(1, HEAD_DIM), k_norm_g.reshape(1, HEAD_DIM))


CONV_TT = 256
CONV_HALO = 32
CONV_RC = 64
CONV_LC = 256
SUBLANES = 8
CONV_SH_ROWS = 40
assert (CONV_HALO + CONV_TT - SUBLANES) % CONV_SH_ROWS == 0


def _silu(x):
    return x * jax.nn.sigmoid(x)


def _conv_a_body(aval_ref, aglu_ref, agate_ref, w_ref, b_ref, lg_ref, lb_ref,
                 ya_ref, tail_ref, ext_ref, y_ref, sh_ref):
    i = pl.program_id(0)

    @pl.when(i == 0)
    def _():
        ext_ref[0:CONV_HALO, :] = jnp.zeros((CONV_HALO, CONV_CH), F32)

    @pl.when(i > 0)
    def _():
        ext_ref[0:CONV_HALO, :] = ext_ref[CONV_TT:CONV_TT + CONV_HALO, :]

    for rc in range(CONV_TT // CONV_RC):
        rows = slice(rc * CONV_RC, (rc + 1) * CONV_RC)
        ext_ref[CONV_HALO + rc * CONV_RC:CONV_HALO + (rc + 1) * CONV_RC, :] = (
            aval_ref[rows, :] * jax.nn.sigmoid(aglu_ref[rows, :]))

    n_sh = CONV_HALO + CONV_TT - SUBLANES
    for s in range(1, SUBLANES):
        for r0 in range(0, n_sh, CONV_SH_ROWS):
            sh_ref[s, r0:r0 + CONV_SH_ROWS, :] = ext_ref[r0 + s:r0 + s + CONV_SH_ROWS, :]

    first = CONV_HALO - (CONV_K - 1)
    for lc in range(CONV_CH // CONV_LC):
        lanes = slice(lc * CONV_LC, (lc + 1) * CONV_LC)
        for rc in range(CONV_TT // CONV_RC):
            acc = jnp.broadcast_to(b_ref[:, lanes], (CONV_RC, CONV_LC))
            for k in range(CONV_K):
                s = (first + k) % SUBLANES
                r0 = first + k - s + rc * CONV_RC
                src = ext_ref[r0:r0 + CONV_RC, lanes] if s == 0 else sh_ref[s, r0:r0 + CONV_RC, lanes]
                acc = acc + w_ref[k:k + 1, lanes] * src
            y_ref[rc * CONV_RC:(rc + 1) * CONV_RC, lanes] = acc

    for rc in range(CONV_TT // CONV_RC):
        rows = slice(rc * CONV_RC, (rc + 1) * CONV_RC)
        y = y_ref[rows, :]
        yc = y - jnp.mean(y, axis=-1, keepdims=True)
        yn = yc * lax.rsqrt(jnp.mean(yc * yc, axis=-1, keepdims=True) + EPS)
        ya_ref[rows, :] = _silu(yn * lg_ref[...] + lb_ref[...]) * _silu(agate_ref[rows, :])

    @pl.when(i == pl.num_programs(0) - 1)
    def _():
        tail_ref[...] = ext_ref[CONV_HALO + CONV_TT - (CONV_K - 1):CONV_HALO + CONV_TT, :]


def _conv_a_prompt(proj, conv_w, conv_b, ln_g, ln_b):
    t = proj.shape[0]
    assert t % CONV_TT == 0
    nb = CONV_CH // LANES

    def col_spec(col0):
        return pl.BlockSpec((CONV_TT, CONV_CH), lambda i: (i, col0 // CONV_CH))

    def full(shape):
        return pl.BlockSpec(shape, lambda i: (0,) * len(shape))

    del nb
    return pl.pallas_call(
        _conv_a_body,
        grid=(t // CONV_TT,),
        in_specs=[col_spec(COL_A_VAL), col_spec(COL_A_GLU), col_spec(COL_A_GATE),
                  full((CONV_K, CONV_CH)), full((1, CONV_CH)), full((1, CONV_CH)), full((1, CONV_CH))],
        out_specs=[pl.BlockSpec((CONV_TT, CONV_CH), lambda i: (i, 0)), full((CONV_K - 1, CONV_CH))],
        out_shape=[jax.ShapeDtypeStruct((t, CONV_CH), F32),
                   jax.ShapeDtypeStruct((CONV_K - 1, CONV_CH), F32)],
        scratch_shapes=[pltpu.VMEM((CONV_HALO + CONV_TT, CONV_CH), F32),
                        pltpu.VMEM((CONV_TT, CONV_CH), F32),
                        pltpu.VMEM((SUBLANES, CONV_HALO + CONV_TT, CONV_CH), F32)],
        compiler_params=pltpu.CompilerParams(
            dimension_semantics=("arbitrary",), vmem_limit_bytes=V7X_VMEM_LIMIT_BYTES),
        name="conv_a",
    )(proj, proj, proj, conv_w, conv_b.reshape(1, CONV_CH), ln_g.reshape(1, CONV_CH), ln_b.reshape(1, CONV_CH))


DN_TB = 256
DN_NC = DN_TB // DN_CHUNK
DN_HALO = 8
assert DN_DK == DN_DV == LANES


def _split_bf16(x, parts):
    out = []
    for _ in range(parts):
        hi = x.astype(BF16)
        out.append(hi)
        x = x - hi.astype(F32)
    return out


def _dot_nt(a, b):
    return lax.dot_general(a, b, (((1,), (1,)), ((), ())), preferred_element_type=F32)


def _dn_prep_body(xq_ref, xk_ref, xv_ref, hq_ref, hk_ref, hv_ref, ba_ref, wq_ref, wk_ref, wv_ref,
                  pa_ref, pd_ref,
                  u0_ref, w_ref, qd_ref, kdt_ref, qk_ref, e_ref,
                  eq_ref, ek_ref, ev_ref):
    i = pl.program_id(0)
    for ext, halo, x in ((eq_ref, hq_ref, xq_ref), (ek_ref, hk_ref, xk_ref), (ev_ref, hv_ref, xv_ref)):
        ext[0:DN_HALO, :] = jnp.where(i == 0, 0.0, halo[...])
        ext[DN_HALO:DN_HALO + DN_TB, :] = x[...]

    ba = ba_ref[...]
    beta_all = jax.nn.sigmoid(ba)
    z = ba + pd_ref[...]
    g_all = -jnp.exp(pa_ref[...]) * (jnp.maximum(z, 0.0) + jnp.log1p(jnp.exp(-jnp.abs(z))))

    ri = lax.broadcasted_iota(jnp.int32, (DN_CHUNK, DN_CHUNK), 0)
    ci = lax.broadcasted_iota(jnp.int32, (DN_CHUNK, DN_CHUNK), 1)
    causal = ri >= ci
    strict = ri > ci
    tril = causal.astype(BF16)
    eye = (ri == ci).astype(F32)
    first = DN_HALO - (DN_SHORT_K - 1)

    for c in range(DN_NC):
        rows = slice(c * DN_CHUNK, (c + 1) * DN_CHUNK)
        g1, g2, g3 = _split_bf16(g_all[rows, :], 3)
        gsum = (jnp.dot(tril, g1, preferred_element_type=F32) + jnp.dot(tril, g2, preferred_element_type=F32)
                + jnp.dot(tril, g3, preferred_element_type=F32))
        a_all, rhs_all = [], []
        for h in range(DN_HEADS):
            lanes = slice(h * LANES, (h + 1) * LANES)

            def short_conv(ext, wref):
                acc = None
                for k in range(DN_SHORT_K):
                    r0 = first + c * DN_CHUNK + k
                    term = wref[k:k + 1, lanes] * ext[r0:r0 + DN_CHUNK, lanes]
                    acc = term if acc is None else acc + term
                return _silu(acc)

            q = short_conv(eq_ref, wq_ref)
            k = short_conv(ek_ref, wk_ref)
            v = short_conv(ev_ref, wv_ref)
            q = q * lax.rsqrt(jnp.sum(q * q, axis=-1, keepdims=True) + EPS) * (DN_DK ** -0.5)
            k = k * lax.rsqrt(jnp.sum(k * k, axis=-1, keepdims=True) + EPS)
            bcol = jnp.broadcast_to(beta_all[rows, h:h + 1], (DN_CHUNK, LANES))
            gcol = jnp.broadcast_to(gsum[:, DN_HEADS + h:DN_HEADS + h + 1], (DN_CHUNK, LANES))
            grow = jnp.transpose(gcol)[0:DN_CHUNK, :]
            decay = jnp.where(causal, jnp.exp(gcol[:, 0:DN_CHUNK] - grow), 0.0)
            kb = k * bcol
            kbf = k.astype(BF16)
            a = jnp.where(strict, _dot_nt(kb.astype(BF16), kbf) * decay, 0.0)
            qk = _dot_nt(q.astype(BF16), kbf) * decay
            eg = jnp.exp(gcol)
            a_all.append(a)
            rhs_all.append(jnp.concatenate([v * bcol, kb * eg], axis=1).astype(BF16))
            glast = gcol[DN_CHUNK - 1:DN_CHUNK, :]
            qd_ref[rows, lanes] = (q * eg).astype(BF16)
            kdt_ref[c, lanes, :] = jnp.transpose(k * jnp.exp(glast - gcol)).astype(BF16)
            qk_ref[rows, h * LANES:h * LANES + DN_CHUNK] = qk.astype(BF16)
            qk_ref[rows, h * LANES + DN_CHUNK:(h + 1) * LANES] = jnp.zeros((DN_CHUNK, LANES - DN_CHUNK), BF16)
            e_ref[c, h:h + 1, :] = jnp.exp(glast)

        tinv = [eye - a for a in a_all]
        pb = [a.astype(BF16) for a in a_all]
        for _ in range(5):
            pb = [jnp.dot(x, x, preferred_element_type=F32).astype(BF16) for x in pb]
            tinv = [t + jnp.dot(x, t.astype(BF16), preferred_element_type=F32) for x, t in zip(pb, tinv)]
        for h in range(DN_HEADS):
            lanes = slice(h * LANES, (h + 1) * LANES)
            sol = jnp.dot(tinv[h].astype(BF16), rhs_all[h], preferred_element_type=F32)
            u0_ref[rows, lanes] = sol[:, 0:DN_DV]
            w_ref[rows, lanes] = sol[:, DN_DV:].astype(BF16)


def _dn_rec_body(u0_ref, w_ref, qd_ref, kdt_ref, qk_ref, e_ref, gate_ref, og_ref,
                 yc_ref, sfin_ref, s_ref):
    i = pl.program_id(0)

    @pl.when(i == 0)
    def _():
        s_ref[...] = jnp.zeros(s_ref.shape, F32)

    og = og_ref[...]
    for c in range(DN_NC):
        rows = slice(c * DN_CHUNK, (c + 1) * DN_CHUNK)
        heads = range(DN_HEADS)
        lanes = [slice(h * LANES, (h + 1) * LANES) for h in heads]
        s = [s_ref[h] for h in heads]
        sb = [x.astype(BF16) for x in s]
        ub = [(u0_ref[rows, lanes[h]] - jnp.dot(w_ref[rows, lanes[h]], sb[h], preferred_element_type=F32)
               ).astype(BF16) for h in heads]
        for h in heads:
            s_ref[h] = (e_ref[c, h:h + 1, :] * s[h]
                        + jnp.dot(kdt_ref[c, lanes[h], :], ub[h], preferred_element_type=F32))
        for h in heads:
            o = (jnp.dot(qd_ref[rows, lanes[h]], sb[h], preferred_element_type=F32)
                 + jnp.dot(qk_ref[rows, h * LANES:h * LANES + DN_CHUNK], ub[h], preferred_element_type=F32))
            yc_ref[rows, lanes[h]] = _rms_rows(o, og) * _silu(gate_ref[rows, lanes[h]])

    @pl.when(i == pl.num_programs(0) - 1)
    def _():
        sfin_ref[...] = s_ref[...]


def _deltanet_prompt(proj, dconv_w, a_log, dt_bias, o_norm_g):
    t = proj.shape[0]
    assert t % DN_TB == 0
    nt = t // DN_TB
    qkv_blk = COL_C_QKV // DN_QK
    halo_per_tb = DN_TB // DN_HALO

    def x_spec(j):
        return pl.BlockSpec((DN_TB, DN_QK), lambda i: (i, qkv_blk + j))

    def halo_spec(j):
        return pl.BlockSpec((DN_HALO, DN_QK), lambda i: (jnp.maximum(i * halo_per_tb - 1, 0), qkv_blk + j))

    def w_spec(j):
        return pl.BlockSpec((DN_SHORT_K, DN_QK), lambda i: (0, j))

    row = pl.BlockSpec((1, LANES), lambda i: (0, 0))
    pad_a = jnp.zeros((1, LANES), F32).at[0, DN_HEADS:2 * DN_HEADS].set(a_log)
    pad_d = jnp.zeros((1, LANES), F32).at[0, DN_HEADS:2 * DN_HEADS].set(dt_bias)
    wide = pl.BlockSpec((DN_TB, DN_OUT), lambda i: (i, 0))
    kdt_spec = pl.BlockSpec((DN_NC, DN_OUT, DN_CHUNK), lambda i: (i, 0, 0))
    e_spec = pl.BlockSpec((DN_NC, DN_HEADS, LANES), lambda i: (i, 0, 0))
    u0, w, qd, kdt, qk, e = pl.pallas_call(
        _dn_prep_body,
        grid=(nt,),
        in_specs=[x_spec(0), x_spec(1), x_spec(2), halo_spec(0), halo_spec(1), halo_spec(2),
                  pl.BlockSpec((DN_TB, LANES), lambda i: (i, COL_BETA // LANES)),
                  w_spec(0), w_spec(1), w_spec(2), row, row],
        out_specs=[wide, wide, wide, kdt_spec, wide, e_spec],
        out_shape=[jax.ShapeDtypeStruct((t, DN_OUT), F32),
                   jax.ShapeDtypeStruct((t, DN_OUT), BF16),
                   jax.ShapeDtypeStruct((t, DN_OUT), BF16),
                   jax.ShapeDtypeStruct((t // DN_CHUNK, DN_OUT, DN_CHUNK), BF16),
                   jax.ShapeDtypeStruct((t, DN_OUT), BF16),
                   jax.ShapeDtypeStruct((t // DN_CHUNK, DN_HEADS, LANES), F32)],
        scratch_shapes=[pltpu.VMEM((DN_HALO + DN_TB, DN_QK), F32) for _ in range(3)],
        compiler_params=pltpu.CompilerParams(
            dimension_semantics=("parallel",), vmem_limit_bytes=V7X_VMEM_LIMIT_BYTES),
        name="dn_prep",
    )(proj, proj, proj, proj, proj, proj, proj, dconv_w, dconv_w, dconv_w, pad_a, pad_d)

    state = pl.BlockSpec((DN_HEADS, DN_DK, DN_DV), lambda i: (0, 0, 0))
    yc, s_fin = pl.pallas_call(
        _dn_rec_body,
        grid=(nt,),
        in_specs=[wide, wide, wide, kdt_spec, wide, e_spec,
                  pl.BlockSpec((DN_TB, DN_OUT), lambda i: (i, COL_C_GATE // DN_OUT)), row],
        out_specs=[wide, state],
        out_shape=[jax.ShapeDtypeStruct((t, DN_OUT), F32),
                   jax.ShapeDtypeStruct((DN_HEADS, DN_DK, DN_DV), F32)],
        scratch_shapes=[pltpu.VMEM((DN_HEADS, DN_DK, DN_DV), F32)],
        compiler_params=pltpu.CompilerParams(
            dimension_semantics=("arbitrary",), vmem_limit_bytes=V7X_VMEM_LIMIT_BYTES),
        name="dn_rec",
    )(u0, w, qd, kdt, qk, e, proj, o_norm_g.reshape(1, LANES))
    return yc, s_fin


BLK = CONV_CH // LANES
assert CONV_CH == ATT_OUT == DN_OUT == DN_QK == BLK * LANES and H_PER_GROUP == DN_HEADS == BLK


def _sample_mix_body(a_ref, p3_ref, ba_ref, cst_ref, cw_ref, cb_ref, lg_ref, lb_ref,
                     kv0_ref, kv1_ref, kv2_ref, qg_ref, kg_ref,
                     qkt_ref, sct_ref, wt_ref, scv_ref, wv_ref, s0_ref, al_ref, dtb_ref, og_ref,
                     ya_ref, glu_ref, yb_ref, yc_ref, n0_ref, n1_ref, n2_ref, s_ref):
    def seg(col0):
        return p3_ref[0, col0 // LANES:col0 // LANES + BLK, :]

    glu = a_ref[0, :, COL_A_VAL:COL_A_VAL + CONV_CH] * jax.nn.sigmoid(a_ref[0, :, COL_A_GLU:COL_A_GLU + CONV_CH])
    y = (jnp.sum(cst_ref[0] * cw_ref[0:CONV_K - 1, :], axis=0, keepdims=True)
         + glu * cw_ref[CONV_K - 1:CONV_K, :] + cb_ref[...])
    yc = y - jnp.mean(y, axis=-1, keepdims=True)
    yn = yc * lax.rsqrt(jnp.mean(yc * yc, axis=-1, keepdims=True) + EPS)
    ya_ref[0] = _silu(yn * lg_ref[...] + lb_ref[...]) * _silu(a_ref[0, :, COL_A_GATE:COL_A_GATE + CONV_CH])
    glu_ref[0] = glu

    scale = HEAD_DIM ** -0.5
    outs, lses = [], []
    for g, (kv_ref, new_ref) in enumerate(((kv0_ref, n0_ref), (kv1_ref, n1_ref), (kv2_ref, n2_ref))):
        q = _rms_rows(seg(COL_Q + g * ATT_OUT), qg_ref[...])
        k_new = _rms_rows(seg(COL_K + g * ATT_OUT), kg_ref[...])
        v_new = seg(COL_V + g * ATT_OUT)
        new_ref[0, 0] = k_new
        new_ref[0, 1] = v_new
        kc = kv_ref[0, 0]
        vc = kv_ref[0, 1]
        s = jnp.sum(kc * q[None], axis=-1, keepdims=True) * scale
        s_new = jnp.sum(k_new * q, axis=-1, keepdims=True) * scale
        m = jnp.maximum(jnp.max(s, axis=0), s_new)
        p = jnp.exp(s - m[None])
        p_new = jnp.exp(s_new - m)
        l = jnp.sum(p, axis=0) + p_new
        outs.append((jnp.sum(p * vc, axis=0) + p_new * v_new) / l)
        lses.append(m + jnp.log(l))
    m = jnp.maximum(jnp.maximum(lses[0], lses[1]), lses[2])
    e = [jnp.exp(x - m) for x in lses]
    yb = (e[0] * outs[0] + e[1] * outs[1] + e[2] * outs[2]) / (e[0] + e[1] + e[2])
    yb_ref[0] = yb * _silu(seg(COL_B_GATE))

    first = DN_SHORT_K - 1
    qk = sct_ref[0, 0] * wt_ref[0]
    for j in range(1, first):
        qk = qk + sct_ref[0, j] * wt_ref[j]
    qk = _silu(qk + qkt_ref[0] * wt_ref[first])
    qk = qk * lax.rsqrt(jnp.sum(qk * qk, axis=0, keepdims=True) + EPS)
    v = scv_ref[0, 0] * wv_ref[0]
    for j in range(1, first):
        v = v + scv_ref[0, j] * wv_ref[j]
    v = _silu(v + seg(COL_C_QKV + 2 * DN_QK) * wv_ref[first])
    hi = lax.broadcasted_iota(jnp.int32, (DN_HEADS, LANES), 0)
    li = lax.broadcasted_iota(jnp.int32, (DN_HEADS, LANES), 1)
    ba = jnp.broadcast_to(ba_ref[0], (DN_HEADS, LANES))
    beta = jax.nn.sigmoid(jnp.sum(jnp.where(li == hi, ba, 0.0), axis=-1, keepdims=True))
    z = jnp.sum(jnp.where(li == hi + DN_HEADS, ba, 0.0), axis=-1, keepdims=True) + dtb_ref[...]
    gdec = jnp.exp(-jnp.exp(al_ref[...]) * (jnp.maximum(z, 0.0) + jnp.log1p(jnp.exp(-jnp.abs(z)))))
    gate = seg(COL_C_GATE)
    for h in range(DN_HEADS):
        qc = qk[:, h:h + 1] * (DN_DK ** -0.5)
        kc = qk[:, DN_HEADS + h:DN_HEADS + h + 1]
        eg = gdec[h:h + 1, :]
        s0 = s0_ref[0, h]
        u = beta[h:h + 1, :] * (v[h:h + 1, :] - eg * jnp.sum(kc * s0, axis=0, keepdims=True))
        s1 = eg * s0 + kc * u
        s_ref[0, h] = s1
        o = jnp.sum(qc * s1, axis=0, keepdims=True)
        yc_ref[0, h:h + 1, :] = _rms_rows(o, og_ref[...]) * _silu(gate[h:h + 1, :])


def _sample_mix(proj2, layer, conv_bufs, kv_bufs, sc_buf, s0s, conv_w, conv_b, ln_g, ln_b, q_norm_g, k_norm_g,
                dconv_w, a_log, dt_bias, o_norm_g):
    b = proj2.shape[0]
    rows = proj2.reshape(b, 1, N_PROJ)
    p3 = proj2.reshape(b, N_PROJ // LANES, LANES)
    nq = 2 * DN_QK
    qkt = proj2[:, COL_C_QKV:COL_C_QKV + nq].reshape(b, 2 * DN_HEADS, DN_DK).transpose(0, 2, 1)
    sct = sc_buf[:, :, :nq].reshape(b, DN_SHORT_K - 1, 2 * DN_HEADS, DN_DK).transpose(0, 1, 3, 2)
    wt = dconv_w[:, :nq].reshape(DN_SHORT_K, 2 * DN_HEADS, DN_DK).transpose(0, 2, 1)
    scv = sc_buf[:, :, nq:].reshape(b, DN_SHORT_K - 1, DN_HEADS, DN_DV)
    wv = dconv_w[:, nq:].reshape(DN_SHORT_K, DN_HEADS, DN_DV)
    kv_views, kv_specs = [], []
    for kv, (window, d) in zip(kv_bufs, ATT_PATTERNS):
        assert kv.shape[1:] == (b, 2, window, H_PER_GROUP, HEAD_DIM) and window == ATT_SPAN * d
        kv_views.append(kv.reshape(kv.shape[0], b, 2, ATT_SPAN, d, H_PER_GROUP, HEAD_DIM))
        kv_specs.append(pl.BlockSpec((None, 1, 2, ATT_SPAN, None, H_PER_GROUP, HEAD_DIM),
                                     lambda i: (layer, i, 0, 0, 0, 0, 0)))

    def full(shape):
        return pl.BlockSpec(shape, lambda i: (0,) * len(shape))

    def per_b(shape):
        return pl.BlockSpec((1,) + shape, lambda i: (i,) + (0,) * len(shape))

    def per_lb(shape):
        return pl.BlockSpec((None, 1) + shape, lambda i: (layer, i) + (0,) * len(shape))

    col = lambda x: x.reshape(-1, 1)
    row = lambda x: x.reshape(1, -1)
    tile_out = jax.ShapeDtypeStruct((b, BLK, LANES), F32)
    new_out = jax.ShapeDtypeStruct((b, 2, H_PER_GROUP, HEAD_DIM), F32)
    outs = pl.pallas_call(
        _sample_mix_body,
        grid=(b,),
        in_specs=[per_b((1, N_PROJ)), per_b((N_PROJ // LANES, LANES)),
                  pl.BlockSpec((1, 1, LANES), lambda i: (i, 0, COL_BETA // LANES)),
                  per_lb((CONV_K - 1, CONV_CH)), full((CONV_K, CONV_CH)),
                  full((1, CONV_CH)), full((1, CONV_CH)), full((1, CONV_CH)),
                  *kv_specs, full((1, HEAD_DIM)), full((1, HEAD_DIM)),
                  per_b((DN_DK, 2 * DN_HEADS)), per_b((DN_SHORT_K - 1, DN_DK, 2 * DN_HEADS)),
                  full((DN_SHORT_K, DN_DK, 2 * DN_HEADS)),
                  per_b((DN_SHORT_K - 1, DN_HEADS, DN_DV)), full((DN_SHORT_K, DN_HEADS, DN_DV)),
                  per_lb((DN_HEADS, DN_DK, DN_DV)), full((DN_HEADS, 1)), full((DN_HEADS, 1)), full((1, DN_DV))],
        out_specs=[per_b((1, CONV_CH)), per_b((1, CONV_CH)), per_b((BLK, LANES)), per_b((BLK, LANES)),
                   per_b((2, H_PER_GROUP, HEAD_DIM)), per_b((2, H_PER_GROUP, HEAD_DIM)),
                   per_b((2, H_PER_GROUP, HEAD_DIM)), per_b((DN_HEADS, DN_DK, DN_DV))],
        out_shape=[jax.ShapeDtypeStruct((b, 1, CONV_CH), F32), jax.ShapeDtypeStruct((b, 1, CONV_CH), F32),
                   tile_out, tile_out, new_out, new_out, new_out,
                   jax.ShapeDtypeStruct((b, DN_HEADS, DN_DK, DN_DV), F32)],
        compiler_params=pltpu.CompilerParams(
            dimension_semantics=("parallel",), vmem_limit_bytes=V7X_VMEM_LIMIT_BYTES),
        name="sample_mix",
    )(rows, p3, rows, conv_bufs, conv_w, row(conv_b), row(ln_g), row(ln_b),
      *kv_views, row(q_norm_g), row(k_norm_g),
      qkt, sct, wt, scv, wv, s0s, col(a_log), col(dt_bias), row(o_norm_g))
    ya, glu, yb, yc, n0, n1, n2, s_new = outs
    return (ya.reshape(b, CONV_CH), glu, yb.reshape(b, ATT_OUT), yc.reshape(b, DN_OUT), (n0, n1, n2), s_new)


SHIFT_ROWS = 512


def _cache_shift_body(cur_ref, nxt_ref, new_ref, o_ref):
    r = cur_ref.shape[3]
    o_ref[0, 0, 0, 0:r - 1] = cur_ref[0, 0, 0, 1:r]
    is_last = pl.program_id(3) == pl.num_programs(3) - 1
    o_ref[0, 0, 0, r - 1] = jnp.where(is_last, new_ref[0, 0, 0], nxt_ref[0, 0, 0, 0])


def _cache_shift(cache, new_rows):
    nl, b, _, lb, h, dh = cache.shape
    r = min(SHIFT_ROWS, lb)
    assert lb % r == 0
    nchunk = lb // r
    return pl.pallas_call(
        _cache_shift_body,
        grid=(nl, b, 2, nchunk),
        in_specs=[pl.BlockSpec((1, 1, 1, r, h, dh), lambda l, i, k, c: (l, i, k, c, 0, 0)),
                  pl.BlockSpec((1, 1, 1, 1, h, dh),
                               lambda l, i, k, c: (l, i, k, jnp.minimum((c + 1) * r, lb - 1), 0, 0)),
                  pl.BlockSpec((1, 1, 1, h, dh), lambda l, i, k, c: (l, i, k, 0, 0))],
        out_specs=pl.BlockSpec((1, 1, 1, r, h, dh), lambda l, i, k, c: (l, i, k, c, 0, 0)),
        out_shape=jax.ShapeDtypeStruct(cache.shape, cache.dtype),
        compiler_params=pltpu.CompilerParams(
            dimension_semantics=("parallel", "parallel", "parallel", "arbitrary"),
            vmem_limit_bytes=V7X_VMEM_LIMIT_BYTES),
        name="cache_shift",
    )(cache, cache, new_rows)


def _rms_norm(x, g):
    y = x * lax.rsqrt(jnp.mean(x * x, axis=-1, keepdims=True) + EPS)
    return y * g


def _layer_norm(x, g, b):
    xc = x - jnp.mean(x, axis=-1, keepdims=True)
    y = xc * lax.rsqrt(jnp.mean(xc * xc, axis=-1, keepdims=True) + EPS)
    return y * g + b


def _l2_norm(x):
    return x * lax.rsqrt(jnp.sum(x * x, axis=-1, keepdims=True) + EPS)


def _causal_depthwise_conv(x, buf, w):
    xc = jnp.concatenate([buf, x], axis=1)
    y = lax.conv_general_dilated(xc, w[:, None, :], window_strides=(1,), padding='VALID',
                                 dimension_numbers=('NWC', 'WIO', 'NWC'), feature_group_count=x.shape[-1])
    return y, xc[:, xc.shape[1] - (w.shape[0] - 1):]


def _dilated_attn_prompt(q, k, v, window, dil):
    B, S, H, Dh = q.shape
    span = window // dil
    L = S // dil
    nb = -(-L // ATT_BLOCK)
    Lp = nb * ATT_BLOCK

    def by_residue(t):
        t = t.reshape(B, L, dil, H, Dh).transpose(0, 2, 1, 3, 4)
        t = jnp.pad(t, ((0, 0), (0, 0), (0, Lp - L), (0, 0), (0, 0)))
        return t.reshape(B, dil, nb, ATT_BLOCK, H, Dh)

    def with_prev_block(t):
        prev = jnp.pad(t, ((0, 0), (0, 0), (1, 0), (0, 0), (0, 0), (0, 0)))[:, :, :nb]
        return jnp.concatenate([prev, t], axis=3)

    qb = by_residue(q)
    kb = with_prev_block(by_residue(k))
    vb = with_prev_block(by_residue(v))
    s = jnp.einsum('brnqhd,brnkhd->brnhqk', qb, kb, preferred_element_type=F32) * (Dh ** -0.5)
    qi = jnp.arange(ATT_BLOCK)[:, None]
    kj = jnp.arange(2 * ATT_BLOCK)[None, :]
    dist = ATT_BLOCK + qi - kj
    kpos = (jnp.arange(nb)[:, None, None] - 1) * ATT_BLOCK + kj
    mask = (dist >= 0) & (dist <= span) & (kpos >= 0)
    s = jnp.where(mask[:, None], s, NEG_INF)
    m = jnp.max(s, axis=-1, keepdims=True)
    p = jnp.exp(s - m)
    l = jnp.sum(p, axis=-1, keepdims=True)
    o = jnp.einsum('brnhqk,brnkhd->brnqhd', p, vb) / jnp.swapaxes(l, 3, 4)
    lse = jnp.swapaxes((m + jnp.log(l))[..., 0], 3, 4)
    o = o.reshape(B, dil, Lp, H, Dh)[:, :, :L].transpose(0, 2, 1, 3, 4).reshape(B, S, H, Dh)
    lse = lse.reshape(B, dil, Lp, H)[:, :, :L].transpose(0, 2, 1, 3).reshape(B, S, H)
    return o, lse


def _dilated_attn_cached(q, k, v, kv_buf, window, dil):
    B, T, H, Dh = q.shape
    Lb = kv_buf.shape[2]
    span = window // dil
    kc = jnp.concatenate([kv_buf[:, 0], k], axis=1)
    vc = jnp.concatenate([kv_buf[:, 1], v], axis=1)
    idx = Lb + jnp.arange(T)[:, None] - dil * jnp.arange(span + 1)[None, :]
    valid = idx >= 0
    idx = jnp.maximum(idx, 0)
    kg = kc[:, idx]
    vg = vc[:, idx]
    s = jnp.einsum('bthd,btmhd->bthm', q, kg, preferred_element_type=F32) * (Dh ** -0.5)
    s = jnp.where(valid[:, None, :], s, NEG_INF)
    m = jnp.max(s, axis=-1, keepdims=True)
    p = jnp.exp(s - m)
    l = jnp.sum(p, axis=-1, keepdims=True)
    o = jnp.einsum('bthm,btmhd->bthd', p, vg) / l
    lse = (m + jnp.log(l))[..., 0]
    start = kc.shape[1] - Lb
    new_buf = jnp.stack([kc[:, start:], vc[:, start:]], axis=1)
    return o, lse, new_buf


def _gated_delta_rule(q, k, v, beta, g, s0):
    B, T, H, dk = q.shape
    dv = v.shape[-1]
    C = DN_CHUNK
    n = -(-T // C)
    pad = n * C - T

    def chunked(t):
        t = jnp.pad(t, ((0, 0), (0, pad)) + ((0, 0),) * (t.ndim - 2))
        t = t.reshape((B, n, C) + t.shape[2:])
        return t.transpose((1, 0, 3, 2) + tuple(range(4, t.ndim)))

    qc, kc, vc, bc, gc = (chunked(t) for t in (q, k, v, beta, g))
    G = jnp.cumsum(gc, axis=-1)
    causal = jnp.tril(jnp.ones((C, C), dtype=bool))
    decay = jnp.exp(jnp.where(causal, G[..., :, None] - G[..., None, :], -jnp.inf))
    kb = kc * bc[..., None]
    strict = causal & ~jnp.eye(C, dtype=bool)
    a = jnp.where(strict, jnp.einsum('nbhid,nbhjd->nbhij', kb, kc) * decay, 0.0)
    rhs = jnp.concatenate([vc * bc[..., None], kb * jnp.exp(G)[..., None]], axis=-1)
    sol = lax.linalg.triangular_solve(a, rhs, left_side=True, lower=True, unit_diagonal=True)
    u0, w = sol[..., :dv], sol[..., dv:]
    qk = jnp.einsum('nbhid,nbhjd->nbhij', qc, kc) * decay
    q_dec = qc * jnp.exp(G)[..., None]
    g_last = G[..., -1]
    k_dec = kc * jnp.exp(g_last[..., None] - G)[..., None]

    def step(S, xs):
        u0_i, w_i, qk_i, qd_i, kd_i, gl_i = xs
        u = u0_i - jnp.einsum('bhck,bhkv->bhcv', w_i, S)
        o = jnp.einsum('bhck,bhkv->bhcv', qd_i, S) + jnp.einsum('bhij,bhjv->bhiv', qk_i, u)
        S = jnp.exp(gl_i)[..., None, None] * S + jnp.einsum('bhck,bhcv->bhkv', kd_i, u)
        return S, o

    s_fin, o = lax.scan(step, s0, (u0, w, qk, q_dec, k_dec, g_last))
    o = o.transpose(1, 0, 3, 2, 4).reshape(B, n * C, H, dv)[:, :T]
    return o, s_fin


def _gated_deltanet(c_qkv, c_gate, c_beta, c_alpha, sc_buf, s0, dconv_w, a_log, dt_bias, o_norm_g):
    B, T, _ = c_qkv.shape
    y, sc_new = _causal_depthwise_conv(c_qkv, sc_buf, dconv_w)
    y = jax.nn.silu(y)
    q, k, v = jnp.split(y, [DN_QK, 2 * DN_QK], axis=-1)
    q = _l2_norm(q.reshape(B, T, DN_HEADS, DN_DK)) * (DN_DK ** -0.5)
    k = _l2_norm(k.reshape(B, T, DN_HEADS, DN_DK))
    v = v.reshape(B, T, DN_HEADS, DN_DV)
    beta = jax.nn.sigmoid(c_beta)
    g = -jnp.exp(a_log) * jax.nn.softplus(c_alpha + dt_bias)
    o, s_new = _gated_delta_rule(q, k, v, beta, g, s0)
    o = _rms_norm(o, o_norm_g) * jax.nn.silu(c_gate.reshape(B, T, DN_HEADS, DN_DV))
    return o.reshape(B, T, DN_OUT), sc_new, s_new


MERGE_TM = 1024
MERGE_TN = 256
assert COL_MLOG % MERGE_TN == 0 and D_MODEL % MERGE_TN == 0


def _merge_body(ya_ref, yb_ref, yc_ref, ga_ref, gb_ref, gc_ref, w_ref, o_ref, lhs_ref):
    @pl.when(pl.program_id(1) == 0)
    def _():
        for b, y_ref in enumerate((ya_ref, yb_ref, yc_ref)):
            lhs_ref[b] = y_ref[...].astype(BF16)

    acc = None
    for b, g_ref in enumerate((ga_ref, gb_ref, gc_ref)):
        part = jax.nn.sigmoid(g_ref[...]) * jnp.dot(
            lhs_ref[b], w_ref[b * CONV_CH:(b + 1) * CONV_CH, :], preferred_element_type=F32)
        acc = part if acc is None else acc + part
    o_ref[...] = acc


def _branch_merge(ya, yb, yc, proj, w_branch, layer):
    assert CONV_CH == ATT_OUT == DN_OUT
    m = ya.shape[0]
    tm = min(MERGE_TM, m)
    y_spec = pl.BlockSpec((tm, CONV_CH), lambda i, j: (i, 0))

    def gate_spec(b):
        blk0 = (COL_MLOG + b * D_MODEL) // MERGE_TN
        return pl.BlockSpec((tm, MERGE_TN), lambda i, j: (i, blk0 + j))

    return pl.pallas_call(
        _merge_body,
        grid=(_cdiv(m, tm), D_MODEL // MERGE_TN),
        in_specs=[y_spec, y_spec, y_spec, gate_spec(0), gate_spec(1), gate_spec(2),
                  _layer_w_spec(layer, N_BRANCH * CONV_CH, MERGE_TN)],
        out_specs=pl.BlockSpec((tm, MERGE_TN), lambda i, j: (i, j)),
        out_shape=jax.ShapeDtypeStruct((m, D_MODEL), F32),
        scratch_shapes=[pltpu.VMEM((N_BRANCH, tm, CONV_CH), BF16)],
        compiler_params=pltpu.CompilerParams(
            dimension_semantics=("parallel", "arbitrary"), vmem_limit_bytes=V7X_VMEM_LIMIT_BYTES),
        name="branch_merge",
    )(ya, yb, yc, proj, proj, proj, w_branch)


def _prompt_layer(x2, layer, caches, cache_bufs, norm_g, w_in_r, conv_w, conv_b, ln_g, ln_b, q_norm_g, k_norm_g,
                  dconv_w, a_log, dt_bias, o_norm_g, w_branch, w_out):
    t = x2.shape[0]
    proj, cache_bufs = _proj_shift(x2, w_in_r, norm_g, layer, caches, cache_bufs, tm=1024, tn=PROJ_TN)
    ya, conv_new = _conv_a_prompt(proj, conv_w, conv_b, ln_g, ln_b)
    yb, kv0, kv1, kv2 = _attn_prompt(proj, q_norm_g, k_norm_g)
    yc, s_new = _deltanet_prompt(proj, dconv_w, a_log, dt_bias, o_norm_g)
    y = _branch_merge(ya, yb, yc, proj, w_branch, layer)
    out = _proj(y, w_out, layer, resid=x2, tm=1024, tn=512)
    sc_new = proj[t - (DN_SHORT_K - 1):, COL_C_QKV:COL_C_QKV + DN_QKV]
    kv_new = [kv.reshape(1, 2, w, H_PER_GROUP, HEAD_DIM) for kv, (w, _) in zip((kv0, kv1, kv2), ATT_PATTERNS)]
    return out, conv_new[None], kv_new, sc_new[None], s_new[None], cache_bufs


def _sample_layer(x, layer, conv_bufs, kv_bufs, sc_buf, s0s, norm_g, w_in_r, conv_w, conv_b, ln_g, ln_b,
                  q_norm_g, k_norm_g, dconv_w, a_log, dt_bias, o_norm_g, w_branch, w_out):
    B, T, _ = x.shape
    assert T == 1
    x2 = x.reshape(B, D_MODEL)
    proj2 = _proj_few_rows(x2, w_in_r, layer, norm_g, tn=PROJ_TN)
    ya, glu, yb, yc, kv_new, s_new = _sample_mix(proj2, layer, conv_bufs, kv_bufs, sc_buf, s0s, conv_w, conv_b,
                                                 ln_g, ln_b, q_norm_g, k_norm_g, dconv_w, a_log, dt_bias, o_norm_g)
    y = _branch_merge(ya, yb, yc, proj2, w_branch, layer)
    out = _proj(y, w_out, layer, resid=x2, tm=B, tn=512).reshape(B, T, D_MODEL)
    conv_new = jnp.concatenate([conv_bufs[layer, :, 1:], glu], axis=1)
    sc_new = jnp.concatenate([sc_buf[:, 1:], proj2[:, None, COL_C_QKV:COL_C_QKV + DN_QKV]], axis=1)
    return out, conv_new, kv_new, sc_new, s_new


def kernel(x_prompt, x_sample, state_conv, cache_kv_w128, cache_kv_w512, cache_kv_w2048, state_short_conv, state_delta, norm_g, w_in, conv_w, conv_b, ln_g, ln_b, q_norm_g, k_norm_g, dconv_w, a_log, dt_bias, o_norm_g, w_branch, w_out):
    bp, t, _ = x_prompt.shape
    assert bp == 1
    xp, xs = x_prompt.reshape(t, D_MODEL), x_sample
    conv_p, conv_s, sc_p, sc_s, d_p, d_s = [], [], [], [], [], []
    kv_p = [[], [], []]
    kv_s = [[], [], []]
    caches = (cache_kv_w128, cache_kv_w512, cache_kv_w2048)
    cache_bufs = None
    w_in_r, w_branch_b, w_out_b = _prep_w_in(w_in), w_branch.astype(BF16), w_out.astype(BF16)
    for l in range(DEPTH):
        wts = (norm_g[l], w_in_r, conv_w[l], conv_b[l], ln_g[l], ln_b[l], q_norm_g[l], k_norm_g[l],
               dconv_w[l], a_log[l], dt_bias[l], o_norm_g[l], w_branch_b, w_out_b)
        xp, c_new, kv_new, sc_new, s_new, cache_bufs = _prompt_layer(xp, l, caches, cache_bufs, *wts)
        conv_p.append(c_new)
        sc_p.append(sc_new)
        d_p.append(s_new)
        for gi in range(N_ATT_GROUPS):
            kv_p[gi].append(kv_new[gi])
        xs, c_new, kv_new, sc_new, s_new = _sample_layer(
            xs, l, state_conv, (cache_kv_w128, cache_kv_w512, cache_kv_w2048),
            state_short_conv[l], state_delta, *wts)
        conv_s.append(c_new)
        sc_s.append(sc_new)
        d_s.append(s_new)
        for gi in range(N_ATT_GROUPS):
            kv_s[gi].append(kv_new[gi])
    kv_s = [[_cache_append(cache_bufs[gi], jnp.stack(kv_s[gi]))] for gi in range(N_ATT_GROUPS)]
    return (xp.reshape(bp, t, D_MODEL), xs,
            jnp.stack(conv_p), jnp.stack(conv_s),
            jnp.stack(kv_p[0]), kv_s[0][0],
            jnp.stack(kv_p[1]), kv_s[1][0],
            jnp.stack(kv_p[2]), kv_s[2][0],
            jnp.stack(sc_p), jnp.stack(sc_s),
            jnp.stack(d_p), jnp.stack(d_s))
```

```python
import functools

import jax
import jax.numpy as jnp
from jax import lax
from jax.experimental import pallas as pl
from jax.experimental.pallas import tpu as pltpu

D_MODEL = 2048
DEPTH = 2
CONV_CH = 1024
CONV_K = 31
ATT_PATTERNS = ((128, 1), (512, 4), (2048, 16))
N_ATT_GROUPS = 3
H_PER_GROUP = 8
HEAD_DIM = 128
ATT_QKV = N_ATT_GROUPS * H_PER_GROUP * HEAD_DIM
ATT_OUT = H_PER_GROUP * HEAD_DIM
ATT_BLOCK = 128
DN_HEADS = 8
DN_DK = 128
DN_DV = 128
DN_QK = DN_HEADS * DN_DK
DN_QKV = DN_HEADS * (2 * DN_DK + DN_DV)
DN_OUT = DN_HEADS * DN_DV
DN_SHORT_K = 4
DN_CHUNK = 64
N_BRANCH = 3
IN_SPLITS = (CONV_CH, CONV_CH, CONV_CH, ATT_QKV, ATT_QKV, ATT_QKV, ATT_OUT,
             DN_QKV, DN_OUT, DN_HEADS, DN_HEADS, N_BRANCH * D_MODEL)
D_IN = sum(IN_SPLITS)
EPS = 1e-6
NEG_INF = -1e30
F32 = jnp.float32
BF16 = jnp.bfloat16

V7X_VMEM_LIMIT_BYTES = 56 * 1024 * 1024


def _cdiv(a, b):
    return -(-a // b)


def _mm(lhs, w, w_t):
    if w_t:
        return lax.dot_general(lhs, w, (((1,), (1,)), ((), ())), preferred_element_type=F32)
    return jnp.dot(lhs, w, preferred_element_type=F32)


def _proj_body(*refs, normalize, residual):
    a_ref, g_ref, w_ref = refs[:3]
    r_ref = refs[3] if residual else None
    o_ref, lhs_ref = refs[-2:]

    @pl.when(pl.program_id(1) == 0)
    def _():
        a = a_ref[...]
        if normalize:
            a = a * lax.rsqrt(jnp.mean(a * a, axis=-1, keepdims=True) + EPS) * g_ref[...]
        lhs_ref[...] = a.astype(BF16)

    acc = _mm(lhs_ref[...], w_ref[...], False)
    if residual:
        acc = acc + r_ref[...]
    o_ref[...] = acc


def _layer_w_spec(layer, k, tn, w_t=False):
    if w_t:
        return pl.BlockSpec((None, tn, k), lambda i, j: (layer, j, 0))
    return pl.BlockSpec((None, k, tn), lambda i, j: (layer, 0, j))


def _proj(a, w, layer, gain=None, resid=None, *, tm, tn):
    m, k = a.shape
    n = w.shape[2]
    tm = min(tm, m)
    tn = min(tn, n)
    normalize = gain is not None
    if gain is None:
        gain = jnp.ones((k,), F32)
    in_specs = [
        pl.BlockSpec((tm, k), lambda i, j: (i, 0)),
        pl.BlockSpec((1, k), lambda i, j: (0, 0)),
        _layer_w_spec(layer, k, tn),
    ]
    args = [a, gain.reshape(1, k), w]
    if resid is not None:
        in_specs.append(pl.BlockSpec((tm, tn), lambda i, j: (i, j)))
        args.append(resid)
    return pl.pallas_call(
        functools.partial(_proj_body, normalize=normalize, residual=resid is not None),
        grid=(_cdiv(m, tm), _cdiv(n, tn)),
        in_specs=in_specs,
        out_specs=pl.BlockSpec((tm, tn), lambda i, j: (i, j)),
        out_shape=jax.ShapeDtypeStruct((m, n), F32),
        scratch_shapes=[pltpu.VMEM((tm, k), BF16)],
        compiler_params=pltpu.CompilerParams(
            dimension_semantics=("parallel", "arbitrary"),
            vmem_limit_bytes=V7X_VMEM_LIMIT_BYTES),
        name="proj",
    )(*args)


def _proj_few_rows_body(a_ref, g_ref, w_ref, o_ref, lhs_ref):
    @pl.when(pl.program_id(0) == 0)
    def _():
        a = a_ref[...]
        lhs_ref[...] = (a * lax.rsqrt(jnp.mean(a * a, axis=-1, keepdims=True) + EPS) * g_ref[...]).astype(BF16)

    o_ref[...] = _mm(w_ref[...], lhs_ref[...], True)


def _proj_few_rows(a, w, layer, gain, *, tn):
    b, k = a.shape
    n = w.shape[1]
    assert n % tn == 0
    out_t = pl.pallas_call(
        _proj_few_rows_body,
        grid=(n // tn,),
        in_specs=[pl.BlockSpec((b, k), lambda j: (0, 0)),
                  pl.BlockSpec((1, k), lambda j: (0, 0)),
                  pl.BlockSpec((None, tn, k), lambda j: (layer, j, 0))],
        out_specs=pl.BlockSpec((tn, b), lambda j: (j, 0)),
        out_shape=jax.ShapeDtypeStruct((n, b), F32),
        scratch_shapes=[pltpu.VMEM((b, k), BF16)],
        compiler_params=pltpu.CompilerParams(
            dimension_semantics=("arbitrary",), vmem_limit_bytes=V7X_VMEM_LIMIT_BYTES),
        name="proj_few_rows",
    )(a, gain.reshape(1, k), w)
    return out_t.T


COPY_PARTS = 3


def _shift_copies(layer, caches, outs, stage, sems, pair, part, to_vmem):
    b, kv = pair // 2, pair % 2
    w0, w1, w2 = (c.shape[3] for c in caches)
    half = w2 // 2

    def mk(g, src0, rows, dst0, slot0, si):
        buf = stage.at[part, pl.ds(slot0, rows)]
        if to_vmem:
            return pltpu.make_async_copy(caches[g].at[layer, b, kv, pl.ds(src0, rows)], buf, sems.at[0, part, si])
        return pltpu.make_async_copy(buf, outs[g].at[layer, b, kv, pl.ds(dst0, rows)], sems.at[1, part, si])

    if part == 0:
        return [mk(2, 1, half, 0, 0, 0)]
    if part == 1:
        return [mk(2, 1 + half, w2 - 1 - half, half, 0, 0)]
    return [mk(1, 1, w1 - 1, 0, 0, 0), mk(0, 1, w0 - 1, 0, w1, 1)]


def _proj_shift_body(*refs, layer, n_slabs):
    a_ref, g_ref, w_ref = refs[:3]
    caches = refs[3:6]
    o_ref = refs[-7]
    outs = refs[-6:-3]
    lhs_ref, stage, sems = refs[-3:]
    step = pl.program_id(0) * pl.num_programs(1) + pl.program_id(1)

    def slab_ops(slab, fn):
        for part in range(COPY_PARTS):
            @pl.when((slab >= 0) & (slab < n_slabs) & (slab % COPY_PARTS == part))
            def _():
                fn(slab // COPY_PARTS, part)

    def start_in(pair, part):
        for c in _shift_copies(layer, caches, outs, stage, sems, pair, part, True):
            c.start()

    def turn_around(pair, part):
        for c in _shift_copies(layer, caches, outs, stage, sems, pair, part, True):
            c.wait()
        for c in _shift_copies(layer, caches, outs, stage, sems, pair, part, False):
            c.start()

    def finish(pair, part):
        for c in _shift_copies(layer, caches, outs, stage, sems, pair, part, False):
            c.wait()

    slab_ops(step - 2, finish)
    slab_ops(step - 1, turn_around)
    slab_ops(step, start_in)

    @pl.when(pl.program_id(1) == 0)
    def _():
        a = a_ref[...]
        lhs_ref[...] = (a * lax.rsqrt(jnp.mean(a * a, axis=-1, keepdims=True) + EPS) * g_ref[...]).astype(BF16)

    o_ref[...] = _mm(lhs_ref[...], w_ref[...], True)


def _proj_shift(a, w, gain, layer, caches, bufs, *, tm, tn):
    m, k = a.shape
    n = w.shape[1]
    assert m % tm == 0 and n % tn == 0
    grid = (m // tm, n // tn)
    nb = caches[0].shape[1]
    n_slabs = nb * 2 * COPY_PARTS
    assert n_slabs + 2 <= grid[0] * grid[1]
    _, _, _, w2, heads, dh = caches[2].shape
    assert caches[1].shape[3] + caches[0].shape[3] <= w2 // 2
    any_spec = pl.BlockSpec(memory_space=pl.ANY)
    in_specs = [pl.BlockSpec((tm, k), lambda i, j: (i, 0)),
                pl.BlockSpec((1, k), lambda i, j: (0, 0)),
                _layer_w_spec(layer, k, tn, True),
                any_spec, any_spec, any_spec]
    args = [a, gain.reshape(1, k), w, *caches]
    aliases = {}
    if bufs is not None:
        in_specs += [any_spec] * 3
        aliases = {len(args) + g: 1 + g for g in range(3)}
        args += list(bufs)
    outs = pl.pallas_call(
        functools.partial(_proj_shift_body, layer=layer, n_slabs=n_slabs),
        grid=grid,
        in_specs=in_specs,
        out_specs=[pl.BlockSpec((tm, tn), lambda i, j: (i, j)), any_spec, any_spec, any_spec],
        out_shape=[jax.ShapeDtypeStruct((m, n), F32)] + [jax.ShapeDtypeStruct(c.shape, c.dtype) for c in caches],
        scratch_shapes=[pltpu.VMEM((tm, k), BF16),
                        pltpu.VMEM((COPY_PARTS, w2 // 2, heads, dh), F32),
                        pltpu.SemaphoreType.DMA((2, COPY_PARTS, 2))],
        input_output_aliases=aliases,
        compiler_params=pltpu.CompilerParams(
            dimension_semantics=("arbitrary", "arbitrary"),
            vmem_limit_bytes=V7X_VMEM_LIMIT_BYTES),
        name="proj_shift",
    )(*args)
    return outs[0], tuple(outs[1:])


def _cache_append_body(buf_ref, new_ref, o_ref):
    del buf_ref
    o_ref[0, :, :, 0] = new_ref[0]


def _cache_append(buf, new_rows):
    nl, b, _, lb, h, dh = buf.shape
    return pl.pallas_call(
        _cache_append_body,
        grid=(nl,),
        in_specs=[pl.BlockSpec(memory_space=pl.ANY),
                  pl.BlockSpec((1, b, 2, h, dh), lambda l: (l, 0, 0, 0, 0))],
        out_specs=pl.BlockSpec((1, b, 2, 1, h, dh), lambda l: (l, 0, 0, lb - 1, 0, 0)),
        out_shape=jax.ShapeDtypeStruct(buf.shape, buf.dtype),
        input_output_aliases={0: 0},
        compiler_params=pltpu.CompilerParams(
            dimension_semantics=("arbitrary",), vmem_limit_bytes=V7X_VMEM_LIMIT_BYTES),
        name="cache_append",
    )(buf, new_rows)


LANES = 128
COL_A_VAL = 0
COL_A_GLU = CONV_CH
COL_A_GATE = 2 * CONV_CH
COL_Q = 3 * CONV_CH
COL_K = COL_Q + ATT_QKV
COL_V = COL_K + ATT_QKV
COL_B_GATE = COL_V + ATT_QKV
COL_C_QKV = COL_B_GATE + ATT_OUT
COL_C_GATE = COL_C_QKV + DN_QKV
COL_MLOG = COL_C_GATE + DN_OUT
COL_BETA = COL_MLOG + N_BRANCH * D_MODEL
COL_ALPHA = COL_BETA + DN_HEADS
W_PREP_TN = 256
N_PROJ = _cdiv(D_IN, W_PREP_TN) * W_PREP_TN
PROJ_TN = 768
assert N_PROJ % PROJ_TN == 0
_SRC_BETA = COL_MLOG
_BA = 2 * DN_HEADS
_T_MLOG = COL_MLOG // W_PREP_TN
_T_BETA = COL_BETA // W_PREP_TN
assert COL_MLOG % W_PREP_TN == 0 and COL_BETA % W_PREP_TN == 0 and _T_BETA == N_PROJ // W_PREP_TN - 1
assert _BA % 16 == 0 and D_IN % _BA == 0


def _prep_w_in_body(src_ref, nxt_ref, ba_ref, o_ref):
    t = pl.program_id(1)
    k = src_ref.shape[1]

    @pl.when(t < _T_MLOG)
    def _():
        o_ref[...] = src_ref[...].astype(BF16)

    @pl.when((t >= _T_MLOG) & (t < _T_BETA))
    def _():
        o_ref[0:W_PREP_TN - _BA, :] = src_ref[_BA:W_PREP_TN, :].astype(BF16)
        o_ref[W_PREP_TN - _BA:W_PREP_TN, :] = nxt_ref[...].astype(BF16)

    @pl.when(t == _T_BETA)
    def _():
        o_ref[0:_BA, :] = ba_ref[...].astype(BF16)
        o_ref[_BA:W_PREP_TN, :] = jnp.zeros((W_PREP_TN - _BA, k), BF16)


def _prep_w_in(w_in):
    nl, k, d_in = w_in.shape
    assert d_in == D_IN
    w_t = jnp.swapaxes(w_in, 1, 2)
    per = W_PREP_TN // _BA
    last_src = (COL_BETA + _BA) // W_PREP_TN - 1
    last_nxt = D_IN // _BA - 1
    return pl.pallas_call(
        _prep_w_in_body,
        grid=(nl, N_PROJ // W_PREP_TN),
        in_specs=[pl.BlockSpec((None, W_PREP_TN, k), lambda l, t: (l, jnp.minimum(t, last_src), 0)),
                  pl.BlockSpec((None, _BA, k), lambda l, t: (l, jnp.minimum((t + 1) * per, last_nxt), 0)),
                  pl.BlockSpec((None, _BA, k), lambda l, t: (l, _SRC_BETA // _BA, 0))],
        out_specs=pl.BlockSpec((None, W_PREP_TN, k), lambda l, t: (l, t, 0)),
        out_shape=jax.ShapeDtypeStruct((nl, N_PROJ, k), BF16),
        compiler_params=pltpu.CompilerParams(
            dimension_semantics=("parallel", "arbitrary"), vmem_limit_bytes=V7X_VMEM_LIMIT_BYTES),
        name="w_prep",
    )(w_t, w_t, w_t)


ATT_SB = ATT_BLOCK * max(d for _, d in ATT_PATTERNS)
ATT_SPAN = ATT_BLOCK
assert all(w // d == ATT_SPAN for w, d in ATT_PATTERNS)


def _rows(start, size, stride):
    return pl.ds(start, size) if stride == 1 else pl.ds(start, size, stride=stride)


def _rms_rows(x, gain):
    return x * lax.rsqrt(jnp.mean(x * x, axis=-1, keepdims=True) + EPS) * gain


def _attn_prompt_body(q0, q1, q2, k0, k1, k2, v0, v1, v2, gate_ref, qg_ref, kg_ref,
                      yb_ref, kv0, kv1, kv2,
                      kr0, kr1, kr2, vr0, vr1, vr2, on0, on1, on2, ln0, ln1, ln2):
    q_refs, k_refs, v_refs = (q0, q1, q2), (k0, k1, k2), (v0, v1, v2)
    kv_refs = (kv0, kv1, kv2)
    kres, vres = (kr0, kr1, kr2), (vr0, vr1, vr2)
    onat, lnat = (on0, on1, on2), (ln0, ln1, ln2)
    i = pl.program_id(1)
    last = pl.num_programs(1) - 1
    qg = qg_ref[...]
    kg = kg_ref[...]
    scale = HEAD_DIM ** -0.5
    qi = lax.broadcasted_iota(jnp.int32, (ATT_BLOCK, 2 * ATT_BLOCK), 0)
    kj = lax.broadcasted_iota(jnp.int32, (ATT_BLOCK, 2 * ATT_BLOCK), 1)
    delta = kj - qi
    band = (delta >= 0) & (delta <= ATT_SPAN)

    for g, (_, d) in enumerate(ATT_PATTERNS):
        nblk = ATT_SB // (ATT_BLOCK * d)
        units = ATT_BLOCK * nblk

        @pl.when(i == 0)
        def _():
            kres[g][:, 0:ATT_BLOCK, :] = jnp.zeros((d, ATT_BLOCK, HEAD_DIM), BF16)
            vres[g][:, 0:ATT_BLOCK, :] = jnp.zeros((d, ATT_BLOCK, HEAD_DIM), BF16)

        @pl.when(i > 0)
        def _():
            kres[g][:, 0:ATT_BLOCK, :] = kres[g][:, units:units + ATT_BLOCK, :]
            vres[g][:, 0:ATT_BLOCK, :] = vres[g][:, units:units + ATT_BLOCK, :]

        for r in range(d):
            for c in range(nblk):
                rows = _rows(c * ATT_BLOCK * d + r, ATT_BLOCK, d)
                dst = slice(ATT_BLOCK + c * ATT_BLOCK, ATT_BLOCK + (c + 1) * ATT_BLOCK)
                kres[g][r, dst, :] = _rms_rows(k_refs[g][rows, :], kg).astype(BF16)
                vres[g][r, dst, :] = v_refs[g][rows, :].astype(BF16)

        for r in range(d):
            for c in range(nblk):
                rows = _rows(c * ATT_BLOCK * d + r, ATT_BLOCK, d)
                q = _rms_rows(q_refs[g][rows, :], qg).astype(BF16)
                kb = kres[g][r, c * ATT_BLOCK:(c + 2) * ATT_BLOCK, :]
                vb = vres[g][r, c * ATT_BLOCK:(c + 2) * ATT_BLOCK, :]
                s = lax.dot_general(q, kb, (((1,), (1,)), ((), ())), preferred_element_type=F32) * scale
                if c == 0:
                    first = jnp.where(i == 0, ATT_BLOCK, 0)
                    mask = band & (kj >= first)
                else:
                    mask = band
                s = jnp.where(mask, s, NEG_INF)
                m = jnp.max(s, axis=-1, keepdims=True)
                p = jnp.exp(s - m)
                l = jnp.sum(p, axis=-1, keepdims=True)
                o = jnp.dot(p.astype(BF16), vb, preferred_element_type=F32) / l
                onat[g][rows, :] = o
                lnat[g][rows, :] = jnp.broadcast_to(m + jnp.log(l), (ATT_BLOCK, HEAD_DIM))

    chunk = 256
    for c in range(ATT_SB // chunk):
        rows = slice(c * chunk, (c + 1) * chunk)
        l0, l1, l2 = lnat[0][rows, :], lnat[1][rows, :], lnat[2][rows, :]
        m = jnp.maximum(jnp.maximum(l0, l1), l2)
        e0, e1, e2 = jnp.exp(l0 - m), jnp.exp(l1 - m), jnp.exp(l2 - m)
        y = (e0 * onat[0][rows, :] + e1 * onat[1][rows, :] + e2 * onat[2][rows, :]) / (e0 + e1 + e2)
        gate = gate_ref[rows, :]
        yb_ref[rows, :] = y * (gate * jax.nn.sigmoid(gate))

    @pl.when(i == last)
    def _():
        for g, (window, _) in enumerate(ATT_PATTERNS):
            tail = slice(ATT_SB - window, ATT_SB)
            kv_refs[g][0, :, :] = _rms_rows(k_refs[g][tail, :], kg)
            kv_refs[g][1, :, :] = v_refs[g][tail, :]


def _attn_prompt(proj, q_norm_g, k_norm_g):
    t = proj.shape[0]
    assert t % ATT_SB == 0
    n_sb = t // ATT_SB

    def col_spec(col0, g):
        blk0 = col0 // LANES + g * H_PER_GROUP
        return pl.BlockSpec((ATT_SB, HEAD_DIM), lambda h, i: (i, blk0 + h))

    in_specs = ([col_spec(COL_Q, g) for g in range(N_ATT_GROUPS)]
                + [col_spec(COL_K, g) for g in range(N_ATT_GROUPS)]
                + [col_spec(COL_V, g) for g in range(N_ATT_GROUPS)]
                + [col_spec(COL_B_GATE, 0),
                   pl.BlockSpec((1, HEAD_DIM), lambda h, i: (0, 0)),
                   pl.BlockSpec((1, HEAD_DIM), lambda h, i: (0, 0))])
    out_specs = [pl.BlockSpec((ATT_SB, HEAD_DIM), lambda h, i: (i, h))]
    out_shape = [jax.ShapeDtypeStruct((t, ATT_OUT), F32)]
    scratch = []
    for window, _ in ATT_PATTERNS:
        out_specs.append(pl.BlockSpec((2, window, HEAD_DIM), lambda h, i: (0, 0, h)))
        out_shape.append(jax.ShapeDtypeStruct((2, window, ATT_OUT), F32))
    for _ in range(2):
        for _, d in ATT_PATTERNS:
            scratch.append(pltpu.VMEM((d, ATT_BLOCK + ATT_SB // d, HEAD_DIM), BF16))
    for _ in range(2 * N_ATT_GROUPS):
        scratch.append(pltpu.VMEM((ATT_SB, HEAD_DIM), F32))
    return pl.pallas_call(
        _attn_prompt_body,
        grid=(H_PER_GROUP, n_sb),
        in_specs=in_specs,
        out_specs=out_specs,
        out_shape=out_shape,
        scratch_shapes=scratch,
        compiler_params=pltpu.CompilerParams(
            dimension_semantics=("parallel", "arbitrary"),
            vmem_limit_bytes=V7X_VMEM_LIMIT_BYTES),
        name="attn_prompt",
    )(*([proj] * 10), q_norm_g.reshape(1, HEAD_DIM), k_norm_g.reshape(1, HEAD_DIM))


CONV_TT = 256
CONV_HALO = 32
CONV_RC = 64
CONV_LC = 256
SUBLANES = 8
CONV_SH_ROWS = 40
assert (CONV_HALO + CONV_TT - SUBLANES) % CONV_SH_ROWS == 0


def _silu(x):
    return x * jax.nn.sigmoid(x)


def _conv_a_body(aval_ref, aglu_ref, agate_ref, w_ref, b_ref, lg_ref, lb_ref,
                 ya_ref, tail_ref, ext_ref, y_ref, sh_ref):
    i = pl.program_id(0)

    @pl.when(i == 0)
    def _():
        ext_ref[0:CONV_HALO, :] = jnp.zeros((CONV_HALO, CONV_CH), F32)

    @pl.when(i > 0)
    def _():
        ext_ref[0:CONV_HALO, :] = ext_ref[CONV_TT:CONV_TT + CONV_HALO, :]

    for rc in range(CONV_TT // CONV_RC):
        rows = slice(rc * CONV_RC, (rc + 1) * CONV_RC)
        ext_ref[CONV_HALO + rc * CONV_RC:CONV_HALO + (rc + 1) * CONV_RC, :] = (
            aval_ref[rows, :] * jax.nn.sigmoid(aglu_ref[rows, :]))

    n_sh = CONV_HALO + CONV_TT - SUBLANES
    for s in range(1, SUBLANES):
        for r0 in range(0, n_sh, CONV_SH_ROWS):
            sh_ref[s, r0:r0 + CONV_SH_ROWS, :] = ext_ref[r0 + s:r0 + s + CONV_SH_ROWS, :]

    first = CONV_HALO - (CONV_K - 1)
    for lc in range(CONV_CH // CONV_LC):
        lanes = slice(lc * CONV_LC, (lc + 1) * CONV_LC)
        for rc in range(CONV_TT // CONV_RC):
            acc = jnp.broadcast_to(b_ref[:, lanes], (CONV_RC, CONV_LC))
            for k in range(CONV_K):
                s = (first + k) % SUBLANES
                r0 = first + k - s + rc * CONV_RC
                src = ext_ref[r0:r0 + CONV_RC, lanes] if s == 0 else sh_ref[s, r0:r0 + CONV_RC, lanes]
                acc = acc + w_ref[k:k + 1, lanes] * src
            y_ref[rc * CONV_RC:(rc + 1) * CONV_RC, lanes] = acc

    for rc in range(CONV_TT // CONV_RC):
        rows = slice(rc * CONV_RC, (rc + 1) * CONV_RC)
        y = y_ref[rows, :]
        yc = y - jnp.mean(y, axis=-1, keepdims=True)
        yn = yc * lax.rsqrt(jnp.mean(yc * yc, axis=-1, keepdims=True) + EPS)
        ya_ref[rows, :] = _silu(yn * lg_ref[...] + lb_ref[...]) * _silu(agate_ref[rows, :])

    @pl.when(i == pl.num_programs(0) - 1)
    def _():
        tail_ref[...] = ext_ref[CONV_HALO + CONV_TT - (CONV_K - 1):CONV_HALO + CONV_TT, :]


def _conv_a_prompt(proj, conv_w, conv_b, ln_g, ln_b):
    t = proj.shape[0]
    assert t % CONV_TT == 0
    nb = CONV_CH // LANES

    def col_spec(col0):
        return pl.BlockSpec((CONV_TT, CONV_CH), lambda i: (i, col0 // CONV_CH))

    def full(shape):
        return pl.BlockSpec(shape, lambda i: (0,) * len(shape))

    del nb
    return pl.pallas_call(
        _conv_a_body,
        grid=(t // CONV_TT,),
        in_specs=[col_spec(COL_A_VAL), col_spec(COL_A_GLU), col_spec(COL_A_GATE),
                  full((CONV_K, CONV_CH)), full((1, CONV_CH)), full((1, CONV_CH)), full((1, CONV_CH))],
        out_specs=[pl.BlockSpec((CONV_TT, CONV_CH), lambda i: (i, 0)), full((CONV_K - 1, CONV_CH))],
        out_shape=[jax.ShapeDtypeStruct((t, CONV_CH), F32),
                   jax.ShapeDtypeStruct((CONV_K - 1, CONV_CH), F32)],
        scratch_shapes=[pltpu.VMEM((CONV_HALO + CONV_TT, CONV_CH), F32),
                        pltpu.VMEM((CONV_TT, CONV_CH), F32),
                        pltpu.VMEM((SUBLANES, CONV_HALO + CONV_TT, CONV_CH), F32)],
        compiler_params=pltpu.CompilerParams(
            dimension_semantics=("arbitrary",), vmem_limit_bytes=V7X_VMEM_LIMIT_BYTES),
        name="conv_a",
    )(proj, proj, proj, conv_w, conv_b.reshape(1, CONV_CH), ln_g.reshape(1, CONV_CH), ln_b.reshape(1, CONV_CH))


DN_TB = 512
DN_NC = DN_TB // DN_CHUNK
DN_HALO = 8
assert DN_DK == DN_DV == LANES


def _split_bf16(x, parts):
    out = []
    for _ in range(parts):
        hi = x.astype(BF16)
        out.append(hi)
        x = x - hi.astype(F32)
    return out


def _dot_nt(a, b):
    return lax.dot_general(a, b, (((1,), (1,)), ((), ())), preferred_element_type=F32)


def _dn_prep_body(xq_ref, xk_ref, xv_ref, hq_ref, hk_ref, hv_ref, ba_ref, wq_ref, wk_ref, wv_ref,
                  pa_ref, pd_ref,
                  u0_ref, w_ref, qd_ref, kdt_ref, qk_ref, e_ref,
                  eq_ref, ek_ref, ev_ref):
    i = pl.program_id(0)
    for ext, halo, x in ((eq_ref, hq_ref, xq_ref), (ek_ref, hk_ref, xk_ref), (ev_ref, hv_ref, xv_ref)):
        ext[0:DN_HALO, :] = jnp.where(i == 0, 0.0, halo[...])
        ext[DN_HALO:DN_HALO + DN_TB, :] = x[...]

    ba = ba_ref[...]
    beta_all = jax.nn.sigmoid(ba)
    z = ba + pd_ref[...]
    g_all = -jnp.exp(pa_ref[...]) * (jnp.maximum(z, 0.0) + jnp.log1p(jnp.exp(-jnp.abs(z))))

    ri = lax.broadcasted_iota(jnp.int32, (DN_CHUNK, DN_CHUNK), 0)
    ci = lax.broadcasted_iota(jnp.int32, (DN_CHUNK, DN_CHUNK), 1)
    causal = ri >= ci
    strict = ri > ci
    tril = causal.astype(BF16)
    eye = (ri == ci).astype(F32)
    first = DN_HALO - (DN_SHORT_K - 1)

    gsums = []
    for c in range(DN_NC):
        g1, g2, g3 = _split_bf16(g_all[c * DN_CHUNK:(c + 1) * DN_CHUNK, :], 3)
        gsums.append(jnp.dot(tril, g1, preferred_element_type=F32) + jnp.dot(tril, g2, preferred_element_type=F32)
                     + jnp.dot(tril, g3, preferred_element_type=F32))

    for c in range(DN_NC):
        rows = slice(c * DN_CHUNK, (c + 1) * DN_CHUNK)
        gsum = gsums[c]
        a_all, rhs_all = [], []
        for h in range(DN_HEADS):
            lanes = slice(h * LANES, (h + 1) * LANES)

            def short_conv(ext, wref):
                acc = None
                for k in range(DN_SHORT_K):
                    r0 = first + c * DN_CHUNK + k
                    term = wref[k:k + 1, lanes] * ext[r0:r0 + DN_CHUNK, lanes]
                    acc = term if acc is None else acc + term
                return _silu(acc)

            q = short_conv(eq_ref, wq_ref)
            k = short_conv(ek_ref, wk_ref)
            v = short_conv(ev_ref, wv_ref)
            q = q * lax.rsqrt(jnp.sum(q * q, axis=-1, keepdims=True) + EPS) * (DN_DK ** -0.5)
            k = k * lax.rsqrt(jnp.sum(k * k, axis=-1, keepdims=True) + EPS)
            bcol = jnp.broadcast_to(beta_all[rows, h:h + 1], (DN_CHUNK, LANES))
            gcol = jnp.broadcast_to(gsum[:, DN_HEADS + h:DN_HEADS + h + 1], (DN_CHUNK, LANES))
            grow = jnp.transpose(gcol)[0:DN_CHUNK, :]
            decay = jnp.where(causal, jnp.exp(gcol[:, 0:DN_CHUNK] - grow), 0.0)
            kb = k * bcol
            kbf = k.astype(BF16)
            a = jnp.where(strict, _dot_nt(kb.astype(BF16), kbf) * decay, 0.0)
            qk = _dot_nt(q.astype(BF16), kbf) * decay
            eg = jnp.exp(gcol)
            a_all.append(a)
            rhs_all.append(jnp.concatenate([v * bcol, kb * eg], axis=1).astype(BF16))
            glast = gcol[DN_CHUNK - 1:DN_CHUNK, :]
            qd_ref[rows, lanes] = (q * eg).astype(BF16)
            kdt_ref[c, lanes, :] = jnp.transpose(k * jnp.exp(glast - gcol)).astype(BF16)
            qk_ref[rows, h * LANES:h * LANES + DN_CHUNK] = qk.astype(BF16)
            qk_ref[rows, h * LANES + DN_CHUNK:(h + 1) * LANES] = jnp.zeros((DN_CHUNK, LANES - DN_CHUNK), BF16)
            e_ref[c, h:h + 1, :] = jnp.exp(glast)

        tinv = [eye - a for a in a_all]
        pb = [a.astype(BF16) for a in a_all]
        for _ in range(5):
            pb = [jnp.dot(x, x, preferred_element_type=F32).astype(BF16) for x in pb]
            tinv = [t + jnp.dot(x, t.astype(BF16), preferred_element_type=F32) for x, t in zip(pb, tinv)]
        for h in range(DN_HEADS):
            lanes = slice(h * LANES, (h + 1) * LANES)
            sol = jnp.dot(tinv[h].astype(BF16), rhs_all[h], preferred_element_type=F32)
            u0_ref[rows, lanes] = sol[:, 0:DN_DV]
            w_ref[rows, lanes] = sol[:, DN_DV:].astype(BF16)


def _dn_rec_body(u0_ref, w_ref, qd_ref, kdt_ref, qk_ref, e_ref, gate_ref, og_ref,
                 yc_ref, sfin_ref, s_ref):
    i = pl.program_id(0)

    @pl.when(i == 0)
    def _():
        s_ref[...] = jnp.zeros(s_ref.shape, F32)

    og = og_ref[...]
    for c in range(DN_NC):
        rows = slice(c * DN_CHUNK, (c + 1) * DN_CHUNK)
        heads = range(DN_HEADS)
        lanes = [slice(h * LANES, (h + 1) * LANES) for h in heads]
        s = [s_ref[h] for h in heads]
        sb = [x.astype(BF16) for x in s]
        ub = [(u0_ref[rows, lanes[h]] - jnp.dot(w_ref[rows, lanes[h]], sb[h], preferred_element_type=F32)
               ).astype(BF16) for h in heads]
        for h in heads:
            s_ref[h] = (e_ref[c, h:h + 1, :] * s[h]
                        + jnp.dot(kdt_ref[c, lanes[h], :], ub[h], preferred_element_type=F32))
        for h in heads:
            o = (jnp.dot(qd_ref[rows, lanes[h]], sb[h], preferred_element_type=F32)
                 + jnp.dot(qk_ref[rows, h * LANES:h * LANES + DN_CHUNK], ub[h], preferred_element_type=F32))
            yc_ref[rows, lanes[h]] = _rms_rows(o, og) * _silu(gate_ref[rows, lanes[h]])

    @pl.when(i == pl.num_programs(0) - 1)
    def _():
        sfin_ref[...] = s_ref[...]


def _deltanet_prompt(proj, dconv_w, a_log, dt_bias, o_norm_g):
    t = proj.shape[0]
    assert t % DN_TB == 0
    nt = t // DN_TB
    qkv_blk = COL_C_QKV // DN_QK
    halo_per_tb = DN_TB // DN_HALO

    def x_spec(j):
        return pl.BlockSpec((DN_TB, DN_QK), lambda i: (i, qkv_blk + j))

    def halo_spec(j):
        return pl.BlockSpec((DN_HALO, DN_QK), lambda i: (jnp.maximum(i * halo_per_tb - 1, 0), qkv_blk + j))

    def w_spec(j):
        return pl.BlockSpec((DN_SHORT_K, DN_QK), lambda i: (0, j))

    row = pl.BlockSpec((1, LANES), lambda i: (0, 0))
    pad_a = jnp.zeros((1, LANES), F32).at[0, DN_HEADS:2 * DN_HEADS].set(a_log)
    pad_d = jnp.zeros((1, LANES), F32).at[0, DN_HEADS:2 * DN_HEADS].set(dt_bias)
    wide = pl.BlockSpec((DN_TB, DN_OUT), lambda i: (i, 0))
    kdt_spec = pl.BlockSpec((DN_NC, DN_OUT, DN_CHUNK), lambda i: (i, 0, 0))
    e_spec = pl.BlockSpec((DN_NC, DN_HEADS, LANES), lambda i: (i, 0, 0))
    u0, w, qd, kdt, qk, e = pl.pallas_call(
        _dn_prep_body,
        grid=(nt,),
        in_specs=[x_spec(0), x_spec(1), x_spec(2), halo_spec(0), halo_spec(1), halo_spec(2),
                  pl.BlockSpec((DN_TB, LANES), lambda i: (i, COL_BETA // LANES)),
                  w_spec(0), w_spec(1), w_spec(2), row, row],
        out_specs=[wide, wide, wide, kdt_spec, wide, e_spec],
        out_shape=[jax.ShapeDtypeStruct((t, DN_OUT), F32),
                   jax.ShapeDtypeStruct((t, DN_OUT), BF16),
                   jax.ShapeDtypeStruct((t, DN_OUT), BF16),
                   jax.ShapeDtypeStruct((t // DN_CHUNK, DN_OUT, DN_CHUNK), BF16),
                   jax.ShapeDtypeStruct((t, DN_OUT), BF16),
                   jax.ShapeDtypeStruct((t // DN_CHUNK, DN_HEADS, LANES), F32)],
        scratch_shapes=[pltpu.VMEM((DN_HALO + DN_TB, DN_QK), F32) for _ in range(3)],
        compiler_params=pltpu.CompilerParams(
            dimension_semantics=("parallel",), vmem_limit_bytes=V7X_VMEM_LIMIT_BYTES),
        name="dn_prep",
    )(proj, proj, proj, proj, proj, proj, proj, dconv_w, dconv_w, dconv_w, pad_a, pad_d)

    state = pl.BlockSpec((DN_HEADS, DN_DK, DN_DV), lambda i: (0, 0, 0))
    yc, s_fin = pl.pallas_call(
        _dn_rec_body,
        grid=(nt,),
        in_specs=[wide, wide, wide, kdt_spec, wide, e_spec,
                  pl.BlockSpec((DN_TB, DN_OUT), lambda i: (i, COL_C_GATE // DN_OUT)), row],
        out_specs=[wide, state],
        out_shape=[jax.ShapeDtypeStruct((t, DN_OUT), F32),
                   jax.ShapeDtypeStruct((DN_HEADS, DN_DK, DN_DV), F32)],
        scratch_shapes=[pltpu.VMEM((DN_HEADS, DN_DK, DN_DV), F32)],
        compiler_params=pltpu.CompilerParams(
            dimension_semantics=("arbitrary",), vmem_limit_bytes=V7X_VMEM_LIMIT_BYTES),
        name="dn_rec",
    )(u0, w, qd, kdt, qk, e, proj, o_norm_g.reshape(1, LANES))
    return yc, s_fin


BLK = CONV_CH // LANES
assert CONV_CH == ATT_OUT == DN_OUT == DN_QK == BLK * LANES and H_PER_GROUP == DN_HEADS == BLK


def _sample_mix_body(a_ref, p3_ref, ba_ref, cst_ref, cw_ref, cb_ref, lg_ref, lb_ref,
                     kv0_ref, kv1_ref, kv2_ref, qg_ref, kg_ref,
                     qkt_ref, sct_ref, wt_ref, scv_ref, wv_ref, s0_ref, al_ref, dtb_ref, og_ref,
                     ya_ref, glu_ref, yb_ref, yc_ref, n0_ref, n1_ref, n2_ref, s_ref):
    def seg(col0):
        return p3_ref[0, col0 // LANES:col0 // LANES + BLK, :]

    glu = a_ref[0, :, COL_A_VAL:COL_A_VAL + CONV_CH] * jax.nn.sigmoid(a_ref[0, :, COL_A_GLU:COL_A_GLU + CONV_CH])
    y = (jnp.sum(cst_ref[0] * cw_ref[0:CONV_K - 1, :], axis=0, keepdims=True)
         + glu * cw_ref[CONV_K - 1:CONV_K, :] + cb_ref[...])
    yc = y - jnp.mean(y, axis=-1, keepdims=True)
    yn = yc * lax.rsqrt(jnp.mean(yc * yc, axis=-1, keepdims=True) + EPS)
    ya_ref[0] = _silu(yn * lg_ref[...] + lb_ref[...]) * _silu(a_ref[0, :, COL_A_GATE:COL_A_GATE + CONV_CH])
    glu_ref[0] = glu

    scale = HEAD_DIM ** -0.5
    outs, lses = [], []
    for g, (kv_ref, new_ref) in enumerate(((kv0_ref, n0_ref), (kv1_ref, n1_ref), (kv2_ref, n2_ref))):
        q = _rms_rows(seg(COL_Q + g * ATT_OUT), qg_ref[...])
        k_new = _rms_rows(seg(COL_K + g * ATT_OUT), kg_ref[...])
        v_new = seg(COL_V + g * ATT_OUT)
        new_ref[0, 0] = k_new
        new_ref[0, 1] = v_new
        kc = kv_ref[0, 0]
        vc = kv_ref[0, 1]
        s = jnp.sum(kc * q[None], axis=-1, keepdims=True) * scale
        s_new = jnp.sum(k_new * q, axis=-1, keepdims=True) * scale
        m = jnp.maximum(jnp.max(s, axis=0), s_new)
        p = jnp.exp(s - m[None])
        p_new = jnp.exp(s_new - m)
        l = jnp.sum(p, axis=0) + p_new
        outs.append((jnp.sum(p * vc, axis=0) + p_new * v_new) / l)
        lses.append(m + jnp.log(l))
    m = jnp.maximum(jnp.maximum(lses[0], lses[1]), lses[2])
    e = [jnp.exp(x - m) for x in lses]
    yb = (e[0] * outs[0] + e[1] * outs[1] + e[2] * outs[2]) / (e[0] + e[1] + e[2])
    yb_ref[0] = yb * _silu(seg(COL_B_GATE))

    first = DN_SHORT_K - 1
    qk = sct_ref[0, 0] * wt_ref[0]
    for j in range(1, first):
        qk = qk + sct_ref[0, j] * wt_ref[j]
    qk = _silu(qk + qkt_ref[0] * wt_ref[first])
    qk = qk * lax.rsqrt(jnp.sum(qk * qk, axis=0, keepdims=True) + EPS)
    v = scv_ref[0, 0] * wv_ref[0]
    for j in range(1, first):
        v = v + scv_ref[0, j] * wv_ref[j]
    v = _silu(v + seg(COL_C_QKV + 2 * DN_QK) * wv_ref[first])
    hi = lax.broadcasted_iota(jnp.int32, (DN_HEADS, LANES), 0)
    li = lax.broadcasted_iota(jnp.int32, (DN_HEADS, LANES), 1)
    ba = jnp.broadcast_to(ba_ref[0], (DN_HEADS, LANES))
    beta = jax.nn.sigmoid(jnp.sum(jnp.where(li == hi, ba, 0.0), axis=-1, keepdims=True))
    z = jnp.sum(jnp.where(li == hi + DN_HEADS, ba, 0.0), axis=-1, keepdims=True) + dtb_ref[...]
    gdec = jnp.exp(-jnp.exp(al_ref[...]) * (jnp.maximum(z, 0.0) + jnp.log1p(jnp.exp(-jnp.abs(z)))))
    gate = seg(COL_C_GATE)
    for h in range(DN_HEADS):
        qc = qk[:, h:h + 1] * (DN_DK ** -0.5)
        kc = qk[:, DN_HEADS + h:DN_HEADS + h + 1]
        eg = gdec[h:h + 1, :]
        s0 = s0_ref[0, h]
        u = beta[h:h + 1, :] * (v[h:h + 1, :] - eg * jnp.sum(kc * s0, axis=0, keepdims=True))
        s1 = eg * s0 + kc * u
        s_ref[0, h] = s1
        o = jnp.sum(qc * s1, axis=0, keepdims=True)
        yc_ref[0, h:h + 1, :] = _rms_rows(o, og_ref[...]) * _silu(gate[h:h + 1, :])


def _sample_mix(proj2, layer, conv_bufs, kv_bufs, sc_buf, s0s, conv_w, conv_b, ln_g, ln_b, q_norm_g, k_norm_g,
                dconv_w, a_log, dt_bias, o_norm_g):
    b = proj2.shape[0]
    rows = proj2.reshape(b, 1, N_PROJ)
    p3 = proj2.reshape(b, N_PROJ // LANES, LANES)
    nq = 2 * DN_QK
    qkt = proj2[:, COL_C_QKV:COL_C_QKV + nq].reshape(b, 2 * DN_HEADS, DN_DK).transpose(0, 2, 1)
    sct = sc_buf[:, :, :nq].reshape(b, DN_SHORT_K - 1, 2 * DN_HEADS, DN_DK).transpose(0, 1, 3, 2)
    wt = dconv_w[:, :nq].reshape(DN_SHORT_K, 2 * DN_HEADS, DN_DK).transpose(0, 2, 1)
    scv = sc_buf[:, :, nq:].reshape(b, DN_SHORT_K - 1, DN_HEADS, DN_DV)
    wv = dconv_w[:, nq:].reshape(DN_SHORT_K, DN_HEADS, DN_DV)
    kv_views, kv_specs = [], []
    for kv, (window, d) in zip(kv_bufs, ATT_PATTERNS):
        assert kv.shape[1:] == (b, 2, window, H_PER_GROUP, HEAD_DIM) and window == ATT_SPAN * d
        kv_views.append(kv.reshape(kv.shape[0], b, 2, ATT_SPAN, d, H_PER_GROUP, HEAD_DIM))
        kv_specs.append(pl.BlockSpec((None, 1, 2, ATT_SPAN, None, H_PER_GROUP, HEAD_DIM),
                                     lambda i: (layer, i, 0, 0, 0, 0, 0)))

    def full(shape):
        return pl.BlockSpec(shape, lambda i: (0,) * len(shape))

    def per_b(shape):
        return pl.BlockSpec((1,) + shape, lambda i: (i,) + (0,) * len(shape))

    def per_lb(shape):
        return pl.BlockSpec((None, 1) + shape, lambda i: (layer, i) + (0,) * len(shape))

    col = lambda x: x.reshape(-1, 1)
    row = lambda x: x.reshape(1, -1)
    tile_out = jax.ShapeDtypeStruct((b, BLK, LANES), F32)
    new_out = jax.ShapeDtypeStruct((b, 2, H_PER_GROUP, HEAD_DIM), F32)
    outs = pl.pallas_call(
        _sample_mix_body,
        grid=(b,),
        in_specs=[per_b((1, N_PROJ)), per_b((N_PROJ // LANES, LANES)),
                  pl.BlockSpec((1, 1, LANES), lambda i: (i, 0, COL_BETA // LANES)),
                  per_lb((CONV_K - 1, CONV_CH)), full((CONV_K, CONV_CH)),
                  full((1, CONV_CH)), full((1, CONV_CH)), full((1, CONV_CH)),
                  *kv_specs, full((1, HEAD_DIM)), full((1, HEAD_DIM)),
                  per_b((DN_DK, 2 * DN_HEADS)), per_b((DN_SHORT_K - 1, DN_DK, 2 * DN_HEADS)),
                  full((DN_SHORT_K, DN_DK, 2 * DN_HEADS)),
                  per_b((DN_SHORT_K - 1, DN_HEADS, DN_DV)), full((DN_SHORT_K, DN_HEADS, DN_DV)),
                  per_lb((DN_HEADS, DN_DK, DN_DV)), full((DN_HEADS, 1)), full((DN_HEADS, 1)), full((1, DN_DV))],
        out_specs=[per_b((1, CONV_CH)), per_b((1, CONV_CH)), per_b((BLK, LANES)), per_b((BLK, LANES)),
                   per_b((2, H_PER_GROUP, HEAD_DIM)), per_b((2, H_PER_GROUP, HEAD_DIM)),
                   per_b((2, H_PER_GROUP, HEAD_DIM)), per_b((DN_HEADS, DN_DK, DN_DV))],
        out_shape=[jax.ShapeDtypeStruct((b, 1, CONV_CH), F32), jax.ShapeDtypeStruct((b, 1, CONV_CH), F32),
                   tile_out, tile_out, new_out, new_out, new_out,
                   jax.ShapeDtypeStruct((b, DN_HEADS, DN_DK, DN_DV), F32)],
        compiler_params=pltpu.CompilerParams(
            dimension_semantics=("parallel",), vmem_limit_bytes=V7X_VMEM_LIMIT_BYTES),
        name="sample_mix",
    )(rows, p3, rows, conv_bufs, conv_w, row(conv_b), row(ln_g), row(ln_b),
      *kv_views, row(q_norm_g), row(k_norm_g),
      qkt, sct, wt, scv, wv, s0s, col(a_log), col(dt_bias), row(o_norm_g))
    ya, glu, yb, yc, n0, n1, n2, s_new = outs
    return (ya.reshape(b, CONV_CH), glu, yb.reshape(b, ATT_OUT), yc.reshape(b, DN_OUT), (n0, n1, n2), s_new)


MERGE_TM = 1024
MERGE_TN = 256
assert COL_MLOG % MERGE_TN == 0 and D_MODEL % MERGE_TN == 0


def _merge_body(ya_ref, yb_ref, yc_ref, ga_ref, gb_ref, gc_ref, w_ref, o_ref, lhs_ref):
    @pl.when(pl.program_id(1) == 0)
    def _():
        for b, y_ref in enumerate((ya_ref, yb_ref, yc_ref)):
            lhs_ref[b] = y_ref[...].astype(BF16)

    acc = None
    for b, g_ref in enumerate((ga_ref, gb_ref, gc_ref)):
        part = jax.nn.sigmoid(g_ref[...]) * jnp.dot(
            lhs_ref[b], w_ref[b * CONV_CH:(b + 1) * CONV_CH, :], preferred_element_type=F32)
        acc = part if acc is None else acc + part
    o_ref[...] = acc


def _branch_merge(ya, yb, yc, proj, w_branch, layer):
    assert CONV_CH == ATT_OUT == DN_OUT
    m = ya.shape[0]
    tm = min(MERGE_TM, m)
    y_spec = pl.BlockSpec((tm, CONV_CH), lambda i, j: (i, 0))

    def gate_spec(b):
        blk0 = (COL_MLOG + b * D_MODEL) // MERGE_TN
        return pl.BlockSpec((tm, MERGE_TN), lambda i, j: (i, blk0 + j))

    return pl.pallas_call(
        _merge_body,
        grid=(_cdiv(m, tm), D_MODEL // MERGE_TN),
        in_specs=[y_spec, y_spec, y_spec, gate_spec(0), gate_spec(1), gate_spec(2),
                  _layer_w_spec(layer, N_BRANCH * CONV_CH, MERGE_TN)],
        out_specs=pl.BlockSpec((tm, MERGE_TN), lambda i, j: (i, j)),
        out_shape=jax.ShapeDtypeStruct((m, D_MODEL), F32),
        scratch_shapes=[pltpu.VMEM((N_BRANCH, tm, CONV_CH), BF16)],
        compiler_params=pltpu.CompilerParams(
            dimension_semantics=("parallel", "arbitrary"), vmem_limit_bytes=V7X_VMEM_LIMIT_BYTES),
        name="branch_merge",
    )(ya, yb, yc, proj, proj, proj, w_branch)


def _prompt_layer(x2, layer, caches, cache_bufs, norm_g, w_in_r, conv_w, conv_b, ln_g, ln_b, q_norm_g, k_norm_g,
                  dconv_w, a_log, dt_bias, o_norm_g, w_branch, w_out):
    t = x2.shape[0]
    proj, cache_bufs = _proj_shift(x2, w_in_r, norm_g, layer, caches, cache_bufs, tm=1024, tn=PROJ_TN)
    ya, conv_new = _conv_a_prompt(proj, conv_w, conv_b, ln_g, ln_b)
    yb, kv0, kv1, kv2 = _attn_prompt(proj, q_norm_g, k_norm_g)
    yc, s_new = _deltanet_prompt(proj, dconv_w, a_log, dt_bias, o_norm_g)
    y = _branch_merge(ya, yb, yc, proj, w_branch, layer)
    out = _proj(y, w_out, layer, resid=x2, tm=1024, tn=512)
    sc_new = proj[t - (DN_SHORT_K - 1):, COL_C_QKV:COL_C_QKV + DN_QKV]
    kv_new = [kv.reshape(1, 2, w, H_PER_GROUP, HEAD_DIM) for kv, (w, _) in zip((kv0, kv1, kv2), ATT_PATTERNS)]
    return out, conv_new[None], kv_new, sc_new[None], s_new[None], cache_bufs


def _sample_layer(x, layer, conv_bufs, kv_bufs, sc_buf, s0s, norm_g, w_in_r, conv_w, conv_b, ln_g, ln_b,
                  q_norm_g, k_norm_g, dconv_w, a_log, dt_bias, o_norm_g, w_branch, w_out):
    B, T, _ = x.shape
    assert T == 1
    x2 = x.reshape(B, D_MODEL)
    proj2 = _proj_few_rows(x2, w_in_r, layer, norm_g, tn=PROJ_TN)
    ya, glu, yb, yc, kv_new, s_new = _sample_mix(proj2, layer, conv_bufs, kv_bufs, sc_buf, s0s, conv_w, conv_b,
                                                 ln_g, ln_b, q_norm_g, k_norm_g, dconv_w, a_log, dt_bias, o_norm_g)
    y = _branch_merge(ya, yb, yc, proj2, w_branch, layer)
    out = _proj(y, w_out, layer, resid=x2, tm=B, tn=512).reshape(B, T, D_MODEL)
    conv_new = jnp.concatenate([conv_bufs[layer, :, 1:], glu], axis=1)
    sc_new = jnp.concatenate([sc_buf[:, 1:], proj2[:, None, COL_C_QKV:COL_C_QKV + DN_QKV]], axis=1)
    return out, conv_new, kv_new, sc_new, s_new


def kernel(x_prompt, x_sample, state_conv, cache_kv_w128, cache_kv_w512, cache_kv_w2048, state_short_conv, state_delta, norm_g, w_in, conv_w, conv_b, ln_g, ln_b, q_norm_g, k_norm_g, dconv_w, a_log, dt_bias, o_norm_g, w_branch, w_out):
    bp, t, _ = x_prompt.shape
    assert bp == 1
    xp, xs = x_prompt.reshape(t, D_MODEL), x_sample
    conv_p, conv_s, sc_p, sc_s, d_p, d_s = [], [], [], [], [], []
    kv_p = [[], [], []]
    kv_s = [[], [], []]
    caches = (cache_kv_w128, cache_kv_w512, cache_kv_w2048)
    cache_bufs = None
    w_in_r, w_branch_b, w_out_b = _prep_w_in(w_in), w_branch.astype(BF16), w_out.astype(BF16)
    for l in range(DEPTH):
        wts = (norm_g[l], w_in_r, conv_w[l], conv_b[l], ln_g[l], ln_b[l], q_norm_g[l], k_norm_g[l],
               dconv_w[l], a_log[l], dt_bias[l], o_norm_g[l], w_branch_b, w_out_b)
        xp, c_new, kv_new, sc_new, s_new, cache_bufs = _prompt_layer(xp, l, caches, cache_bufs, *wts)
        conv_p.append(c_new)
        sc_p.append(sc_new)
        d_p.append(s_new)
        for gi in range(N_ATT_GROUPS):
            kv_p[gi].append(kv_new[gi])
        xs, c_new, kv_new, sc_new, s_new = _sample_layer(
            xs, l, state_conv, (cache_kv_w128, cache_kv_w512, cache_kv_w2048),
            state_short_conv[l], state_delta, *wts)
        conv_s.append(c_new)
        sc_s.append(sc_new)
        d_s.append(s_new)
        for gi in range(N_ATT_GROUPS):
            kv_s[gi].append(kv_new[gi])
    kv_s = [[_cache_append(cache_bufs[gi], jnp.stack(kv_s[gi]))] for gi in range(N_ATT_GROUPS)]
    return (xp.reshape(bp, t, D_MODEL), xs,
            jnp.stack(conv_p), jnp.stack(conv_s),
            jnp.stack(kv_p[0]), kv_s[0][0],
            jnp.stack(kv_p[1]), kv_s[1][0],
            jnp.stack(kv_p[2]), kv_s[2][0],
            jnp.stack(sc_p), jnp.stack(sc_s),
            jnp.stack(d_p), jnp.stack(d_s))
```

```python
import functools

import jax
import jax.numpy as jnp
from jax import lax
from jax.experimental import pallas as pl
from jax.experimental.pallas import tpu as pltpu

D_MODEL = 2048
DEPTH = 2
CONV_CH = 1024
CONV_K = 31
ATT_PATTERNS = ((128, 1), (512, 4), (2048, 16))
N_ATT_GROUPS = 3
H_PER_GROUP = 8
HEAD_DIM = 128
ATT_QKV = N_ATT_GROUPS * H_PER_GROUP * HEAD_DIM
ATT_OUT = H_PER_GROUP * HEAD_DIM
ATT_BLOCK = 128
DN_HEADS = 8
DN_DK = 128
DN_DV = 128
DN_QK = DN_HEADS * DN_DK
DN_QKV = DN_HEADS * (2 * DN_DK + DN_DV)
DN_OUT = DN_HEADS * DN_DV
DN_SHORT_K = 4
DN_CHUNK = 64
N_BRANCH = 3
IN_SPLITS = (CONV_CH, CONV_CH, CONV_CH, ATT_QKV, ATT_QKV, ATT_QKV, ATT_OUT,
             DN_QKV, DN_OUT, DN_HEADS, DN_HEADS, N_BRANCH * D_MODEL)
D_IN = sum(IN_SPLITS)
EPS = 1e-6
NEG_INF = -1e30
F32 = jnp.float32
BF16 = jnp.bfloat16

V7X_VMEM_LIMIT_BYTES = 56 * 1024 * 1024


def _cdiv(a, b):
    return -(-a // b)


def _mm(lhs, w, w_t):
    if w_t:
        return lax.dot_general(lhs, w, (((1,), (1,)), ((), ())), preferred_element_type=F32)
    return jnp.dot(lhs, w, preferred_element_type=F32)


def _proj_body(*refs, normalize, residual):
    a_ref, g_ref, w_ref = refs[:3]
    r_ref = refs[3] if residual else None
    o_ref, lhs_ref = refs[-2:]

    @pl.when(pl.program_id(1) == 0)
    def _():
        a = a_ref[...]
        if normalize:
            a = a * lax.rsqrt(jnp.mean(a * a, axis=-1, keepdims=True) + EPS) * g_ref[...]
        lhs_ref[...] = a.astype(BF16)

    acc = _mm(lhs_ref[...], w_ref[...], False)
    if residual:
        acc = acc + r_ref[...]
    o_ref[...] = acc


def _layer_w_spec(layer, k, tn, w_t=False):
    if w_t:
        return pl.BlockSpec((None, tn, k), lambda i, j: (layer, j, 0))
    return pl.BlockSpec((None, k, tn), lambda i, j: (layer, 0, j))


def _proj(a, w, layer, gain=None, resid=None, *, tm, tn):
    m, k = a.shape
    n = w.shape[2]
    tm = min(tm, m)
    tn = min(tn, n)
    normalize = gain is not None
    if gain is None:
        gain = jnp.ones((k,), F32)
    in_specs = [
        pl.BlockSpec((tm, k), lambda i, j: (i, 0)),
        pl.BlockSpec((1, k), lambda i, j: (0, 0)),
        _layer_w_spec(layer, k, tn),
    ]
    args = [a, gain.reshape(1, k), w]
    if resid is not None:
        in_specs.append(pl.BlockSpec((tm, tn), lambda i, j: (i, j)))
        args.append(resid)
    return pl.pallas_call(
        functools.partial(_proj_body, normalize=normalize, residual=resid is not None),
        grid=(_cdiv(m, tm), _cdiv(n, tn)),
        in_specs=in_specs,
        out_specs=pl.BlockSpec((tm, tn), lambda i, j: (i, j)),
        out_shape=jax.ShapeDtypeStruct((m, n), F32),
        scratch_shapes=[pltpu.VMEM((tm, k), BF16)],
        compiler_params=pltpu.CompilerParams(
            dimension_semantics=("parallel", "arbitrary"),
            vmem_limit_bytes=V7X_VMEM_LIMIT_BYTES),
        name="proj",
    )(*args)


def _proj_few_rows_body(a_ref, g_ref, w_ref, o_ref, lhs_ref):
    @pl.when(pl.program_id(0) == 0)
    def _():
        a = a_ref[...]
        lhs_ref[...] = (a * lax.rsqrt(jnp.mean(a * a, axis=-1, keepdims=True) + EPS) * g_ref[...]).astype(BF16)

    o_ref[...] = _mm(w_ref[...], lhs_ref[...], True)


def _proj_few_rows(a, w, layer, gain, *, tn):
    b, k = a.shape
    n = w.shape[1]
    assert n % tn == 0
    out_t = pl.pallas_call(
        _proj_few_rows_body,
        grid=(n // tn,),
        in_specs=[pl.BlockSpec((b, k), lambda j: (0, 0)),
                  pl.BlockSpec((1, k), lambda j: (0, 0)),
                  pl.BlockSpec((None, tn, k), lambda j: (layer, j, 0))],
        out_specs=pl.BlockSpec((tn, b), lambda j: (j, 0)),
        out_shape=jax.ShapeDtypeStruct((n, b), F32),
        scratch_shapes=[pltpu.VMEM((b, k), BF16)],
        compiler_params=pltpu.CompilerParams(
            dimension_semantics=("arbitrary",), vmem_limit_bytes=V7X_VMEM_LIMIT_BYTES),
        name="proj_few_rows",
    )(a, gain.reshape(1, k), w)
    return out_t.T


COPY_PARTS = 3


def _shift_copies(layer, caches, outs, stage, sems, pair, part, to_vmem):
    b, kv = pair // 2, pair % 2
    w0, w1, w2 = (c.shape[3] for c in caches)
    half = w2 // 2

    def mk(g, src0, rows, dst0, slot0, si):
        buf = stage.at[part, pl.ds(slot0, rows)]
        if to_vmem:
            return pltpu.make_async_copy(caches[g].at[layer, b, kv, pl.ds(src0, rows)], buf, sems.at[0, part, si])
        return pltpu.make_async_copy(buf, outs[g].at[layer, b, kv, pl.ds(dst0, rows)], sems.at[1, part, si])

    if part == 0:
        return [mk(2, 1, half, 0, 0, 0)]
    if part == 1:
        return [mk(2, 1 + half, w2 - 1 - half, half, 0, 0)]
    return [mk(1, 1, w1 - 1, 0, 0, 0), mk(0, 1, w0 - 1, 0, w1, 1)]


def _proj_shift_body(*refs, layer, n_slabs):
    a_ref, g_ref, w_ref = refs[:3]
    caches = refs[3:6]
    o_ref = refs[-7]
    outs = refs[-6:-3]
    lhs_ref, stage, sems = refs[-3:]
    step = pl.program_id(0) * pl.num_programs(1) + pl.program_id(1)

    def slab_ops(slab, fn):
        for part in range(COPY_PARTS):
            @pl.when((slab >= 0) & (slab < n_slabs) & (slab % COPY_PARTS == part))
            def _():
                fn(slab // COPY_PARTS, part)

    def start_in(pair, part):
        for c in _shift_copies(layer, caches, outs, stage, sems, pair, part, True):
            c.start()

    def turn_around(pair, part):
        for c in _shift_copies(layer, caches, outs, stage, sems, pair, part, True):
            c.wait()
        for c in _shift_copies(layer, caches, outs, stage, sems, pair, part, False):
            c.start()

    def finish(pair, part):
        for c in _shift_copies(layer, caches, outs, stage, sems, pair, part, False):
            c.wait()

    slab_ops(step - 2, finish)
    slab_ops(step - 1, turn_around)
    slab_ops(step, start_in)

    @pl.when(pl.program_id(1) == 0)
    def _():
        a = a_ref[...]
        lhs_ref[...] = (a * lax.rsqrt(jnp.mean(a * a, axis=-1, keepdims=True) + EPS) * g_ref[...]).astype(BF16)

    o_ref[...] = _mm(lhs_ref[...], w_ref[...], True)


def _proj_shift(a, w, gain, layer, caches, bufs, *, tm, tn):
    m, k = a.shape
    n = w.shape[1]
    assert m % tm == 0 and n % tn == 0
    grid = (m // tm, n // tn)
    nb = caches[0].shape[1]
    n_slabs = nb * 2 * COPY_PARTS
    assert n_slabs + 2 <= grid[0] * grid[1]
    _, _, _, w2, heads, dh = caches[2].shape
    assert caches[1].shape[3] + caches[0].shape[3] <= w2 // 2
    any_spec = pl.BlockSpec(memory_space=pl.ANY)
    in_specs = [pl.BlockSpec((tm, k), lambda i, j: (i, 0)),
                pl.BlockSpec((1, k), lambda i, j: (0, 0)),
                _layer_w_spec(layer, k, tn, True),
                any_spec, any_spec, any_spec]
    args = [a, gain.reshape(1, k), w, *caches]
    aliases = {}
    if bufs is not None:
        in_specs += [any_spec] * 3
        aliases = {len(args) + g: 1 + g for g in range(3)}
        args += list(bufs)
    outs = pl.pallas_call(
        functools.partial(_proj_shift_body, layer=layer, n_slabs=n_slabs),
        grid=grid,
        in_specs=in_specs,
        out_specs=[pl.BlockSpec((tm, tn), lambda i, j: (i, j)), any_spec, any_spec, any_spec],
        out_shape=[jax.ShapeDtypeStruct((m, n), F32)] + [jax.ShapeDtypeStruct(c.shape, c.dtype) for c in caches],
        scratch_shapes=[pltpu.VMEM((tm, k), BF16),
                        pltpu.VMEM((COPY_PARTS, w2 // 2, heads, dh), F32),
                        pltpu.SemaphoreType.DMA((2, COPY_PARTS, 2))],
        input_output_aliases=aliases,
        compiler_params=pltpu.CompilerParams(
            dimension_semantics=("arbitrary", "arbitrary"),
            vmem_limit_bytes=V7X_VMEM_LIMIT_BYTES),
        name="proj_shift",
    )(*args)
    return outs[0], tuple(outs[1:])


def _cache_append_body(buf_ref, new_ref, o_ref):
    del buf_ref
    o_ref[0, :, :, 0] = new_ref[0]


def _cache_append(buf, new_rows):
    nl, b, _, lb, h, dh = buf.shape
    return pl.pallas_call(
        _cache_append_body,
        grid=(nl,),
        in_specs=[pl.BlockSpec(memory_space=pl.ANY),
                  pl.BlockSpec((1, b, 2, h, dh), lambda l: (l, 0, 0, 0, 0))],
        out_specs=pl.BlockSpec((1, b, 2, 1, h, dh), lambda l: (l, 0, 0, lb - 1, 0, 0)),
        out_shape=jax.ShapeDtypeStruct(buf.shape, buf.dtype),
        input_output_aliases={0: 0},
        compiler_params=pltpu.CompilerParams(
            dimension_semantics=("arbitrary",), vmem_limit_bytes=V7X_VMEM_LIMIT_BYTES),
        name="cache_append",
    )(buf, new_rows)


LANES = 128
COL_A_VAL = 0
COL_A_GLU = CONV_CH
COL_A_GATE = 2 * CONV_CH
COL_Q = 3 * CONV_CH
COL_K = COL_Q + ATT_QKV
COL_V = COL_K + ATT_QKV
COL_B_GATE = COL_V + ATT_QKV
COL_C_QKV = COL_B_GATE + ATT_OUT
COL_C_GATE = COL_C_QKV + DN_QKV
COL_MLOG = COL_C_GATE + DN_OUT
COL_BETA = COL_MLOG + N_BRANCH * D_MODEL
COL_ALPHA = COL_BETA + DN_HEADS
W_PREP_TN = 256
N_PROJ = _cdiv(D_IN, W_PREP_TN) * W_PREP_TN
PROJ_TN = 768
assert N_PROJ % PROJ_TN == 0
_SRC_BETA = COL_MLOG
_BA = 2 * DN_HEADS
_T_MLOG = COL_MLOG // W_PREP_TN
_T_BETA = COL_BETA // W_PREP_TN
assert COL_MLOG % W_PREP_TN == 0 and COL_BETA % W_PREP_TN == 0 and _T_BETA == N_PROJ // W_PREP_TN - 1
assert _BA % 16 == 0 and D_IN % _BA == 0


def _prep_w_in_body(src_ref, nxt_ref, ba_ref, o_ref):
    t = pl.program_id(1)
    k = src_ref.shape[1]

    @pl.when(t < _T_MLOG)
    def _():
        o_ref[...] = src_ref[...].astype(BF16)

    @pl.when((t >= _T_MLOG) & (t < _T_BETA))
    def _():
        o_ref[0:W_PREP_TN - _BA, :] = src_ref[_BA:W_PREP_TN, :].astype(BF16)
        o_ref[W_PREP_TN - _BA:W_PREP_TN, :] = nxt_ref[...].astype(BF16)

    @pl.when(t == _T_BETA)
    def _():
        o_ref[0:_BA, :] = ba_ref[...].astype(BF16)
        o_ref[_BA:W_PREP_TN, :] = jnp.zeros((W_PREP_TN - _BA, k), BF16)


def _prep_w_in(w_in):
    nl, k, d_in = w_in.shape
    assert d_in == D_IN
    w_t = jnp.swapaxes(w_in, 1, 2)
    per = W_PREP_TN // _BA
    last_src = (COL_BETA + _BA) // W_PREP_TN - 1
    last_nxt = D_IN // _BA - 1
    return pl.pallas_call(
        _prep_w_in_body,
        grid=(nl, N_PROJ // W_PREP_TN),
        in_specs=[pl.BlockSpec((None, W_PREP_TN, k), lambda l, t: (l, jnp.minimum(t, last_src), 0)),
                  pl.BlockSpec((None, _BA, k), lambda l, t: (l, jnp.minimum((t + 1) * per, last_nxt), 0)),
                  pl.BlockSpec((None, _BA, k), lambda l, t: (l, _SRC_BETA // _BA, 0))],
        out_specs=pl.BlockSpec((None, W_PREP_TN, k), lambda l, t: (l, t, 0)),
        out_shape=jax.ShapeDtypeStruct((nl, N_PROJ, k), BF16),
        compiler_params=pltpu.CompilerParams(
            dimension_semantics=("parallel", "arbitrary"), vmem_limit_bytes=V7X_VMEM_LIMIT_BYTES),
        name="w_prep",
    )(w_t, w_t, w_t)


ATT_SB = ATT_BLOCK * max(d for _, d in ATT_PATTERNS)
ATT_SPAN = ATT_BLOCK
assert all(w // d == ATT_SPAN for w, d in ATT_PATTERNS)


def _rows(start, size, stride):
    return pl.ds(start, size) if stride == 1 else pl.ds(start, size, stride=stride)


def _rms_rows(x, gain):
    return x * lax.rsqrt(jnp.mean(x * x, axis=-1, keepdims=True) + EPS) * gain


def _attn_prompt_body(q0, q1, q2, k0, k1, k2, v0, v1, v2, gate_ref, qg_ref, kg_ref,
                      yb_ref, kv0, kv1, kv2,
                      kr0, kr1, kr2, vr0, vr1, vr2, on0, on1, on2, ln0, ln1, ln2):
    q_refs, k_refs, v_refs = (q0, q1, q2), (k0, k1, k2), (v0, v1, v2)
    kv_refs = (kv0, kv1, kv2)
    kres, vres = (kr0, kr1, kr2), (vr0, vr1, vr2)
    onat, lnat = (on0, on1, on2), (ln0, ln1, ln2)
    i = pl.program_id(1)
    last = pl.num_programs(1) - 1
    qg = qg_ref[...]
    kg = kg_ref[...]
    scale = HEAD_DIM ** -0.5
    qi = lax.broadcasted_iota(jnp.int32, (ATT_BLOCK, 2 * ATT_BLOCK), 0)
    kj = lax.broadcasted_iota(jnp.int32, (ATT_BLOCK, 2 * ATT_BLOCK), 1)
    delta = kj - qi
    band = (delta >= 0) & (delta <= ATT_SPAN)

    for g, (_, d) in enumerate(ATT_PATTERNS):
        nblk = ATT_SB // (ATT_BLOCK * d)
        units = ATT_BLOCK * nblk

        @pl.when(i == 0)
        def _():
            kres[g][:, 0:ATT_BLOCK, :] = jnp.zeros((d, ATT_BLOCK, HEAD_DIM), BF16)
            vres[g][:, 0:ATT_BLOCK, :] = jnp.zeros((d, ATT_BLOCK, HEAD_DIM), BF16)

        @pl.when(i > 0)
        def _():
            kres[g][:, 0:ATT_BLOCK, :] = kres[g][:, units:units + ATT_BLOCK, :]
            vres[g][:, 0:ATT_BLOCK, :] = vres[g][:, units:units + ATT_BLOCK, :]

        for r in range(d):
            for c in range(nblk):
                rows = _rows(c * ATT_BLOCK * d + r, ATT_BLOCK, d)
                dst = slice(ATT_BLOCK + c * ATT_BLOCK, ATT_BLOCK + (c + 1) * ATT_BLOCK)
                kres[g][r, dst, :] = _rms_rows(k_refs[g][rows, :], kg).astype(BF16)
                vres[g][r, dst, :] = v_refs[g][rows, :].astype(BF16)

        for r in range(d):
            for c in range(nblk):
                rows = _rows(c * ATT_BLOCK * d + r, ATT_BLOCK, d)
                q = _rms_rows(q_refs[g][rows, :], qg).astype(BF16)
                kb = kres[g][r, c * ATT_BLOCK:(c + 2) * ATT_BLOCK, :]
                vb = vres[g][r, c * ATT_BLOCK:(c + 2) * ATT_BLOCK, :]
                s = lax.dot_general(q, kb, (((1,), (1,)), ((), ())), preferred_element_type=F32) * scale
                if c == 0:
                    first = jnp.where(i == 0, ATT_BLOCK, 0)
                    mask = band & (kj >= first)
                else:
                    mask = band
                s = jnp.where(mask, s, NEG_INF)
                m = jnp.max(s, axis=-1, keepdims=True)
                p = jnp.exp(s - m)
                l = jnp.sum(p, axis=-1, keepdims=True)
                o = jnp.dot(p.astype(BF16), vb, preferred_element_type=F32) / l
                onat[g][rows, :] = o
                lnat[g][rows, :] = jnp.broadcast_to(m + jnp.log(l), (ATT_BLOCK, HEAD_DIM))

    chunk = 256
    for c in range(ATT_SB // chunk):
        rows = slice(c * chunk, (c + 1) * chunk)
        l0, l1, l2 = lnat[0][rows, :], lnat[1][rows, :], lnat[2][rows, :]
        m = jnp.maximum(jnp.maximum(l0, l1), l2)
        e0, e1, e2 = jnp.exp(l0 - m), jnp.exp(l1 - m), jnp.exp(l2 - m)
        y = (e0 * onat[0][rows, :] + e1 * onat[1][rows, :] + e2 * onat[2][rows, :]) / (e0 + e1 + e2)
        gate = gate_ref[rows, :]
        yb_ref[rows, :] = y * (gate * jax.nn.sigmoid(gate))

    @pl.when(i == last)
    def _():
        for g, (window, _) in enumerate(ATT_PATTERNS):
            tail = slice(ATT_SB - window, ATT_SB)
            kv_refs[g][0, :, :] = _rms_rows(k_refs[g][tail, :], kg)
            kv_refs[g][1, :, :] = v_refs[g][tail, :]


def _attn_prompt(proj, q_norm_g, k_norm_g):
    t = proj.shape[0]
    assert t % ATT_SB == 0
    n_sb = t // ATT_SB

    def col_spec(col0, g):
        blk0 = col0 // LANES + g * H_PER_GROUP
        return pl.BlockSpec((ATT_SB, HEAD_DIM), lambda h, i: (i, blk0 + h))

    in_specs = ([col_spec(COL_Q, g) for g in range(N_ATT_GROUPS)]
                + [col_spec(COL_K, g) for g in range(N_ATT_GROUPS)]
                + [col_spec(COL_V, g) for g in range(N_ATT_GROUPS)]
                + [col_spec(COL_B_GATE, 0),
                   pl.BlockSpec((1, HEAD_DIM), lambda h, i: (0, 0)),
                   pl.BlockSpec((1, HEAD_DIM), lambda h, i: (0, 0))])
    out_specs = [pl.BlockSpec((ATT_SB, HEAD_DIM), lambda h, i: (i, h))]
    out_shape = [jax.ShapeDtypeStruct((t, ATT_OUT), F32)]
    scratch = []
    for window, _ in ATT_PATTERNS:
        out_specs.append(pl.BlockSpec((2, window, HEAD_DIM), lambda h, i: (0, 0, h)))
        out_shape.append(jax.ShapeDtypeStruct((2, window, ATT_OUT), F32))
    for _ in range(2):
        for _, d in ATT_PATTERNS:
            scratch.append(pltpu.VMEM((d, ATT_BLOCK + ATT_SB // d, HEAD_DIM), BF16))
    for _ in range(2 * N_ATT_GROUPS):
        scratch.append(pltpu.VMEM((ATT_SB, HEAD_DIM), F32))
    return pl.pallas_call(
        _attn_prompt_body,
        grid=(H_PER_GROUP, n_sb),
        in_specs=in_specs,
        out_specs=out_specs,
        out_shape=out_shape,
        scratch_shapes=scratch,
        compiler_params=pltpu.CompilerParams(
            dimension_semantics=("parallel", "arbitrary"),
            vmem_limit_bytes=V7X_VMEM_LIMIT_BYTES),
        name="attn_prompt",
    )(*([proj] * 10), q_norm_g.reshape(1, HEAD_DIM), k_norm_g.reshape(1, HEAD_DIM))


CONV_TT = 256
CONV_HALO = 32
CONV_RC = 64
CONV_LC = 256
SUBLANES = 8
CONV_SH_ROWS = 40
assert (CONV_HALO + CONV_TT - SUBLANES) % CONV_SH_ROWS == 0


def _silu(x):
    return x * jax.nn.sigmoid(x)


def _conv_a_body(aval_ref, aglu_ref, agate_ref, w_ref, b_ref, lg_ref, lb_ref,
                 ya_ref, tail_ref, ext_ref, y_ref, sh_ref):
    i = pl.program_id(0)

    @pl.when(i == 0)
    def _():
        ext_ref[0:CONV_HALO, :] = jnp.zeros((CONV_HALO, CONV_CH), F32)

    @pl.when(i > 0)
    def _():
        ext_ref[0:CONV_HALO, :] = ext_ref[CONV_TT:CONV_TT + CONV_HALO, :]

    for rc in range(CONV_TT // CONV_RC):
        rows = slice(rc * CONV_RC, (rc + 1) * CONV_RC)
        ext_ref[CONV_HALO + rc * CONV_RC:CONV_HALO + (rc + 1) * CONV_RC, :] = (
            aval_ref[rows, :] * jax.nn.sigmoid(aglu_ref[rows, :]))

    n_sh = CONV_HALO + CONV_TT - SUBLANES
    for s in range(1, SUBLANES):
        for r0 in range(0, n_sh, CONV_SH_ROWS):
            sh_ref[s, r0:r0 + CONV_SH_ROWS, :] = ext_ref[r0 + s:r0 + s + CONV_SH_ROWS, :]

    first = CONV_HALO - (CONV_K - 1)
    for lc in range(CONV_CH // CONV_LC):
        lanes = slice(lc * CONV_LC, (lc + 1) * CONV_LC)
        for rc in range(CONV_TT // CONV_RC):
            acc = jnp.broadcast_to(b_ref[:, lanes], (CONV_RC, CONV_LC))
            for k in range(CONV_K):
                s = (first + k) % SUBLANES
                r0 = first + k - s + rc * CONV_RC
                src = ext_ref[r0:r0 + CONV_RC, lanes] if s == 0 else sh_ref[s, r0:r0 + CONV_RC, lanes]
                acc = acc + w_ref[k:k + 1, lanes] * src
            y_ref[rc * CONV_RC:(rc + 1) * CONV_RC, lanes] = acc

    for rc in range(CONV_TT // CONV_RC):
        rows = slice(rc * CONV_RC, (rc + 1) * CONV_RC)
        y = y_ref[rows, :]
        yc = y - jnp.mean(y, axis=-1, keepdims=True)
        yn = yc * lax.rsqrt(jnp.mean(yc * yc, axis=-1, keepdims=True) + EPS)
        ya_ref[rows, :] = _silu(yn * lg_ref[...] + lb_ref[...]) * _silu(agate_ref[rows, :])

    @pl.when(i == pl.num_programs(0) - 1)
    def _():
        tail_ref[...] = ext_ref[CONV_HALO + CONV_TT - (CONV_K - 1):CONV_HALO + CONV_TT, :]


def _conv_a_prompt(proj, conv_w, conv_b, ln_g, ln_b):
    t = proj.shape[0]
    assert t % CONV_TT == 0
    nb = CONV_CH // LANES

    def col_spec(col0):
        return pl.BlockSpec((CONV_TT, CONV_CH), lambda i: (i, col0 // CONV_CH))

    def full(shape):
        return pl.BlockSpec(shape, lambda i: (0,) * len(shape))

    del nb
    return pl.pallas_call(
        _conv_a_body,
        grid=(t // CONV_TT,),
        in_specs=[col_spec(COL_A_VAL), col_spec(COL_A_GLU), col_spec(COL_A_GATE),
                  full((CONV_K, CONV_CH)), full((1, CONV_CH)), full((1, CONV_CH)), full((1, CONV_CH))],
        out_specs=[pl.BlockSpec((CONV_TT, CONV_CH), lambda i: (i, 0)), full((CONV_K - 1, CONV_CH))],
        out_shape=[jax.ShapeDtypeStruct((t, CONV_CH), F32),
                   jax.ShapeDtypeStruct((CONV_K - 1, CONV_CH), F32)],
        scratch_shapes=[pltpu.VMEM((CONV_HALO + CONV_TT, CONV_CH), F32),
                        pltpu.VMEM((CONV_TT, CONV_CH), F32),
                        pltpu.VMEM((SUBLANES, CONV_HALO + CONV_TT, CONV_CH), F32)],
        compiler_params=pltpu.CompilerParams(
            dimension_semantics=("arbitrary",), vmem_limit_bytes=V7X_VMEM_LIMIT_BYTES),
        name="conv_a",
    )(proj, proj, proj, conv_w, conv_b.reshape(1, CONV_CH), ln_g.reshape(1, CONV_CH), ln_b.reshape(1, CONV_CH))


DN_TB = 512
DN_NC = DN_TB // DN_CHUNK
DN_HALO = 8
assert DN_DK == DN_DV == LANES


def _split_bf16(x, parts):
    out = []
    for _ in range(parts):
        hi = x.astype(BF16)
        out.append(hi)
        x = x - hi.astype(F32)
    return out


def _dot_nt(a, b):
    return lax.dot_general(a, b, (((1,), (1,)), ((), ())), preferred_element_type=F32)


def _dn_prep_body(xq_ref, xk_ref, xv_ref, hq_ref, hk_ref, hv_ref, ba_ref, wq_ref, wk_ref, wv_ref,
                  pa_ref, pd_ref,
                  u0_ref, w_ref, qd_ref, kdt_ref, qk_ref, e_ref,
                  eq_ref, ek_ref, ev_ref):
    i = pl.program_id(0)
    for ext, halo, x in ((eq_ref, hq_ref, xq_ref), (ek_ref, hk_ref, xk_ref), (ev_ref, hv_ref, xv_ref)):
        ext[0:DN_HALO, :] = jnp.where(i == 0, 0.0, halo[...])
        ext[DN_HALO:DN_HALO + DN_TB, :] = x[...]

    ba = ba_ref[...]
    beta_all = jax.nn.sigmoid(ba)
    z = ba + pd_ref[...]
    g_all = -jnp.exp(pa_ref[...]) * (jnp.maximum(z, 0.0) + jnp.log1p(jnp.exp(-jnp.abs(z))))

    ri = lax.broadcasted_iota(jnp.int32, (DN_CHUNK, DN_CHUNK), 0)
    ci = lax.broadcasted_iota(jnp.int32, (DN_CHUNK, DN_CHUNK), 1)
    causal = ri >= ci
    strict = ri > ci
    tril = causal.astype(BF16)
    eye = (ri == ci).astype(F32)
    first = DN_HALO - (DN_SHORT_K - 1)

    gsums = []
    for c in range(DN_NC):
        g1, g2, g3 = _split_bf16(g_all[c * DN_CHUNK:(c + 1) * DN_CHUNK, :], 3)
        gsums.append(jnp.dot(tril, g1, preferred_element_type=F32) + jnp.dot(tril, g2, preferred_element_type=F32)
                     + jnp.dot(tril, g3, preferred_element_type=F32))

    for c in range(DN_NC):
        rows = slice(c * DN_CHUNK, (c + 1) * DN_CHUNK)
        gsum = gsums[c]
        a_all, rhs_all = [], []
        for h in range(DN_HEADS):
            lanes = slice(h * LANES, (h + 1) * LANES)

            def short_conv(ext, wref):
                acc = None
                for k in range(DN_SHORT_K):
                    r0 = first + c * DN_CHUNK + k
                    term = wref[k:k + 1, lanes] * ext[r0:r0 + DN_CHUNK, lanes]
                    acc = term if acc is None else acc + term
                return _silu(acc)

            q = short_conv(eq_ref, wq_ref)
            k = short_conv(ek_ref, wk_ref)
            v = short_conv(ev_ref, wv_ref)
            q = q * lax.rsqrt(jnp.sum(q * q, axis=-1, keepdims=True) + EPS) * (DN_DK ** -0.5)
            k = k * lax.rsqrt(jnp.sum(k * k, axis=-1, keepdims=True) + EPS)
            bcol = jnp.broadcast_to(beta_all[rows, h:h + 1], (DN_CHUNK, LANES))
            gcol = jnp.broadcast_to(gsum[:, DN_HEADS + h:DN_HEADS + h + 1], (DN_CHUNK, LANES))
            grow = jnp.transpose(gcol)[0:DN_CHUNK, :]
            decay = jnp.where(causal, jnp.exp(gcol[:, 0:DN_CHUNK] - grow), 0.0)
            kb = k * bcol
            kbf = k.astype(BF16)
            a = jnp.where(strict, _dot_nt(kb.astype(BF16), kbf) * decay, 0.0)
            qk = _dot_nt(q.astype(BF16), kbf) * decay
            eg = jnp.exp(gcol)
            a_all.append(a)
            rhs_all.append(jnp.concatenate([v * bcol, kb * eg], axis=1).astype(BF16))
            glast = gcol[DN_CHUNK - 1:DN_CHUNK, :]
            qd_ref[rows, lanes] = (q * eg).astype(BF16)
            kdt_ref[c, lanes, :] = jnp.transpose(k * jnp.exp(glast - gcol)).astype(BF16)
            qk_ref[rows, h * LANES:h * LANES + DN_CHUNK] = qk.astype(BF16)
            qk_ref[rows, h * LANES + DN_CHUNK:(h + 1) * LANES] = jnp.zeros((DN_CHUNK, LANES - DN_CHUNK), BF16)
            e_ref[c, h:h + 1, :] = jnp.exp(glast)

        tinv = [eye - a for a in a_all]
        pb = [a.astype(BF16) for a in a_all]
        for _ in range(5):
            pb = [jnp.dot(x, x, preferred_element_type=F32).astype(BF16) for x in pb]
            tinv = [t + jnp.dot(x, t.astype(BF16), preferred_element_type=F32) for x, t in zip(pb, tinv)]
        for h in range(DN_HEADS):
            lanes = slice(h * LANES, (h + 1) * LANES)
            sol = jnp.dot(tinv[h].astype(BF16), rhs_all[h], preferred_element_type=F32)
            u0_ref[rows, lanes] = sol[:, 0:DN_DV]
            w_ref[rows, lanes] = sol[:, DN_DV:].astype(BF16)


def _dn_rec_body(u0_ref, w_ref, qd_ref, kdt_ref, qk_ref, e_ref, gate_ref, og_ref,
                 yc_ref, sfin_ref, s_ref):
    i = pl.program_id(0)

    @pl.when(i == 0)
    def _():
        s_ref[...] = jnp.zeros(s_ref.shape, F32)

    og = og_ref[...]
    for c in range(DN_NC):
        rows = slice(c * DN_CHUNK, (c + 1) * DN_CHUNK)
        heads = range(DN_HEADS)
        lanes = [slice(h * LANES, (h + 1) * LANES) for h in heads]
        s = [s_ref[h] for h in heads]
        sb = [x.astype(BF16) for x in s]
        ub = [(u0_ref[rows, lanes[h]] - jnp.dot(w_ref[rows, lanes[h]], sb[h], preferred_element_type=F32)
               ).astype(BF16) for h in heads]
        for h in heads:
            s_ref[h] = (e_ref[c, h:h + 1, :] * s[h]
                        + jnp.dot(kdt_ref[c, lanes[h], :], ub[h], preferred_element_type=F32))
        for h in heads:
            o = (jnp.dot(qd_ref[rows, lanes[h]], sb[h], preferred_element_type=F32)
                 + jnp.dot(qk_ref[rows, h * LANES:h * LANES + DN_CHUNK], ub[h], preferred_element_type=F32))
            yc_ref[rows, lanes[h]] = _rms_rows(o, og) * _silu(gate_ref[rows, lanes[h]])

    @pl.when(i == pl.num_programs(0) - 1)
    def _():
        sfin_ref[...] = s_ref[...]


def _deltanet_prompt(proj, dconv_w, a_log, dt_bias, o_norm_g):
    t = proj.shape[0]
    assert t % DN_TB == 0
    nt = t // DN_TB
    qkv_blk = COL_C_QKV // DN_QK
    halo_per_tb = DN_TB // DN_HALO

    def x_spec(j):
        return pl.BlockSpec((DN_TB, DN_QK), lambda i: (i, qkv_blk + j))

    def halo_spec(j):
        return pl.BlockSpec((DN_HALO, DN_QK), lambda i: (jnp.maximum(i * halo_per_tb - 1, 0), qkv_blk + j))

    def w_spec(j):
        return pl.BlockSpec((DN_SHORT_K, DN_QK), lambda i: (0, j))

    row = pl.BlockSpec((1, LANES), lambda i: (0, 0))
    pad_a = jnp.zeros((1, LANES), F32).at[0, DN_HEADS:2 * DN_HEADS].set(a_log)
    pad_d = jnp.zeros((1, LANES), F32).at[0, DN_HEADS:2 * DN_HEADS].set(dt_bias)
    wide = pl.BlockSpec((DN_TB, DN_OUT), lambda i: (i, 0))
    kdt_spec = pl.BlockSpec((DN_NC, DN_OUT, DN_CHUNK), lambda i: (i, 0, 0))
    e_spec = pl.BlockSpec((DN_NC, DN_HEADS, LANES), lambda i: (i, 0, 0))
    u0, w, qd, kdt, qk, e = pl.pallas_call(
        _dn_prep_body,
        grid=(nt,),
        in_specs=[x_spec(0), x_spec(1), x_spec(2), halo_spec(0), halo_spec(1), halo_spec(2),
                  pl.BlockSpec((DN_TB, LANES), lambda i: (i, COL_BETA // LANES)),
                  w_spec(0), w_spec(1), w_spec(2), row, row],
        out_specs=[wide, wide, wide, kdt_spec, wide, e_spec],
        out_shape=[jax.ShapeDtypeStruct((t, DN_OUT), F32),
                   jax.ShapeDtypeStruct((t, DN_OUT), BF16),
                   jax.ShapeDtypeStruct((t, DN_OUT), BF16),
                   jax.ShapeDtypeStruct((t // DN_CHUNK, DN_OUT, DN_CHUNK), BF16),
                   jax.ShapeDtypeStruct((t, DN_OUT), BF16),
                   jax.ShapeDtypeStruct((t // DN_CHUNK, DN_HEADS, LANES), F32)],
        scratch_shapes=[pltpu.VMEM((DN_HALO + DN_TB, DN_QK), F32) for _ in range(3)],
        compiler_params=pltpu.CompilerParams(
            dimension_semantics=("parallel",), vmem_limit_bytes=V7X_VMEM_LIMIT_BYTES),
        name="dn_prep",
    )(proj, proj, proj, proj, proj, proj, proj, dconv_w, dconv_w, dconv_w, pad_a, pad_d)

    state = pl.BlockSpec((DN_HEADS, DN_DK, DN_DV), lambda i: (0, 0, 0))
    yc, s_fin = pl.pallas_call(
        _dn_rec_body,
        grid=(nt,),
        in_specs=[wide, wide, wide, kdt_spec, wide, e_spec,
                  pl.BlockSpec((DN_TB, DN_OUT), lambda i: (i, COL_C_GATE // DN_OUT)), row],
        out_specs=[wide, state],
        out_shape=[jax.ShapeDtypeStruct((t, DN_OUT), F32),
                   jax.ShapeDtypeStruct((DN_HEADS, DN_DK, DN_DV), F32)],
        scratch_shapes=[pltpu.VMEM((DN_HEADS, DN_DK, DN_DV), F32)],
        compiler_params=pltpu.CompilerParams(
            dimension_semantics=("arbitrary",), vmem_limit_bytes=V7X_VMEM_LIMIT_BYTES),
        name="dn_rec",
    )(u0, w, qd, kdt, qk, e, proj, o_norm_g.reshape(1, LANES))
    return yc, s_fin


BLK = CONV_CH // LANES
assert CONV_CH == ATT_OUT == DN_OUT == DN_QK == BLK * LANES and H_PER_GROUP == DN_HEADS == BLK


def _sample_mix_body(a_ref, p3_ref, ba_ref, cst_ref, cw_ref, cb_ref, lg_ref, lb_ref,
                     kv0_ref, kv1_ref, kv2_ref, qg_ref, kg_ref,
                     qkt_ref, sct_ref, wt_ref, scv_ref, wv_ref, s0_ref, al_ref, dtb_ref, og_ref,
                     ya_ref, glu_ref, yb_ref, yc_ref, n0_ref, n1_ref, n2_ref, s_ref):
    def seg(col0):
        return p3_ref[0, col0 // LANES:col0 // LANES + BLK, :]

    glu = a_ref[0, :, COL_A_VAL:COL_A_VAL + CONV_CH] * jax.nn.sigmoid(a_ref[0, :, COL_A_GLU:COL_A_GLU + CONV_CH])
    y = (jnp.sum(cst_ref[0] * cw_ref[0:CONV_K - 1, :], axis=0, keepdims=True)
         + glu * cw_ref[CONV_K - 1:CONV_K, :] + cb_ref[...])
    yc = y - jnp.mean(y, axis=-1, keepdims=True)
    yn = yc * lax.rsqrt(jnp.mean(yc * yc, axis=-1, keepdims=True) + EPS)
    ya_ref[0] = _silu(yn * lg_ref[...] + lb_ref[...]) * _silu(a_ref[0, :, COL_A_GATE:COL_A_GATE + CONV_CH])
    glu_ref[0] = glu

    scale = HEAD_DIM ** -0.5
    outs, lses = [], []
    for g, (kv_ref, new_ref) in enumerate(((kv0_ref, n0_ref), (kv1_ref, n1_ref), (kv2_ref, n2_ref))):
        q = _rms_rows(seg(COL_Q + g * ATT_OUT), qg_ref[...])
        k_new = _rms_rows(seg(COL_K + g * ATT_OUT), kg_ref[...])
        v_new = seg(COL_V + g * ATT_OUT)
        new_ref[0, 0] = k_new
        new_ref[0, 1] = v_new
        kc = kv_ref[0, 0]
        vc = kv_ref[0, 1]
        s = jnp.sum(kc * q[None], axis=-1, keepdims=True) * scale
        s_new = jnp.sum(k_new * q, axis=-1, keepdims=True) * scale
        m = jnp.maximum(jnp.max(s, axis=0), s_new)
        p = jnp.exp(s - m[None])
        p_new = jnp.exp(s_new - m)
        l = jnp.sum(p, axis=0) + p_new
        outs.append((jnp.sum(p * vc, axis=0) + p_new * v_new) / l)
        lses.append(m + jnp.log(l))
    m = jnp.maximum(jnp.maximum(lses[0], lses[1]), lses[2])
    e = [jnp.exp(x - m) for x in lses]
    yb = (e[0] * outs[0] + e[1] * outs[1] + e[2] * outs[2]) / (e[0] + e[1] + e[2])
    yb_ref[0] = yb * _silu(seg(COL_B_GATE))

    first = DN_SHORT_K - 1
    qk = sct_ref[0, 0] * wt_ref[0]
    for j in range(1, first):
        qk = qk + sct_ref[0, j] * wt_ref[j]
    qk = _silu(qk + qkt_ref[0] * wt_ref[first])
    qk = qk * lax.rsqrt(jnp.sum(qk * qk, axis=0, keepdims=True) + EPS)
    v = scv_ref[0, 0] * wv_ref[0]
    for j in range(1, first):
        v = v + scv_ref[0, j] * wv_ref[j]
    v = _silu(v + seg(COL_C_QKV + 2 * DN_QK) * wv_ref[first])
    hi = lax.broadcasted_iota(jnp.int32, (DN_HEADS, LANES), 0)
    li = lax.broadcasted_iota(jnp.int32, (DN_HEADS, LANES), 1)
    ba = jnp.broadcast_to(ba_ref[0], (DN_HEADS, LANES))
    beta = jax.nn.sigmoid(jnp.sum(jnp.where(li == hi, ba, 0.0), axis=-1, keepdims=True))
    z = jnp.sum(jnp.where(li == hi + DN_HEADS, ba, 0.0), axis=-1, keepdims=True) + dtb_ref[...]
    gdec = jnp.exp(-jnp.exp(al_ref[...]) * (jnp.maximum(z, 0.0) + jnp.log1p(jnp.exp(-jnp.abs(z)))))
    gate = seg(COL_C_GATE)
    for h in range(DN_HEADS):
        qc = qk[:, h:h + 1] * (DN_DK ** -0.5)
        kc = qk[:, DN_HEADS + h:DN_HEADS + h + 1]
        eg = gdec[h:h + 1, :]
        s0 = s0_ref[0, h]
        u = beta[h:h + 1, :] * (v[h:h + 1, :] - eg * jnp.sum(kc * s0, axis=0, keepdims=True))
        s1 = eg * s0 + kc * u
        s_ref[0, h] = s1
        o = jnp.sum(qc * s1, axis=0, keepdims=True)
        yc_ref[0, h:h + 1, :] = _rms_rows(o, og_ref[...]) * _silu(gate[h:h + 1, :])


def _sample_mix(proj2, layer, conv_bufs, kv_bufs, sc_buf, s0s, conv_w, conv_b, ln_g, ln_b, q_norm_g, k_norm_g,
                dconv_w, a_log, dt_bias, o_norm_g):
    b = proj2.shape[0]
    rows = proj2.reshape(b, 1, N_PROJ)
    p3 = proj2.reshape(b, N_PROJ // LANES, LANES)
    nq = 2 * DN_QK
    qkt = proj2[:, COL_C_QKV:COL_C_QKV + nq].reshape(b, 2 * DN_HEADS, DN_DK).transpose(0, 2, 1)
    sct = sc_buf[:, :, :nq].reshape(b, DN_SHORT_K - 1, 2 * DN_HEADS, DN_DK).transpose(0, 1, 3, 2)
    wt = dconv_w[:, :nq].reshape(DN_SHORT_K, 2 * DN_HEADS, DN_DK).transpose(0, 2, 1)
    scv = sc_buf[:, :, nq:].reshape(b, DN_SHORT_K - 1, DN_HEADS, DN_DV)
    wv = dconv_w[:, nq:].reshape(DN_SHORT_K, DN_HEADS, DN_DV)
    kv_views, kv_specs = [], []
    for kv, (window, d) in zip(kv_bufs, ATT_PATTERNS):
        assert kv.shape[1:] == (b, 2, window, H_PER_GROUP, HEAD_DIM) and window == ATT_SPAN * d
        kv_views.append(kv.reshape(kv.shape[0], b, 2, ATT_SPAN, d, H_PER_GROUP, HEAD_DIM))
        kv_specs.append(pl.BlockSpec((None, 1, 2, ATT_SPAN, None, H_PER_GROUP, HEAD_DIM),
                                     lambda i: (layer, i, 0, 0, 0, 0, 0)))

    def full(shape):
        return pl.BlockSpec(shape, lambda i: (0,) * len(shape))

    def per_b(shape):
        return pl.BlockSpec((1,) + shape, lambda i: (i,) + (0,) * len(shape))

    def per_lb(shape):
        return pl.BlockSpec((None, 1) + shape, lambda i: (layer, i) + (0,) * len(shape))

    col = lambda x: x.reshape(-1, 1)
    row = lambda x: x.reshape(1, -1)
    tile_out = jax.ShapeDtypeStruct((b, BLK, LANES), F32)
    new_out = jax.ShapeDtypeStruct((b, 2, H_PER_GROUP, HEAD_DIM), F32)
    outs = pl.pallas_call(
        _sample_mix_body,
        grid=(b,),
        in_specs=[per_b((1, N_PROJ)), per_b((N_PROJ // LANES, LANES)),
                  pl.BlockSpec((1, 1, LANES), lambda i: (i, 0, COL_BETA // LANES)),
                  per_lb((CONV_K - 1, CONV_CH)), full((CONV_K, CONV_CH)),
                  full((1, CONV_CH)), full((1, CONV_CH)), full((1, CONV_CH)),
                  *kv_specs, full((1, HEAD_DIM)), full((1, HEAD_DIM)),
                  per_b((DN_DK, 2 * DN_HEADS)), per_b((DN_SHORT_K - 1, DN_DK, 2 * DN_HEADS)),
                  full((DN_SHORT_K, DN_DK, 2 * DN_HEADS)),
                  per_b((DN_SHORT_K - 1, DN_HEADS, DN_DV)), full((DN_SHORT_K, DN_HEADS, DN_DV)),
                  per_lb((DN_HEADS, DN_DK, DN_DV)), full((DN_HEADS, 1)), full((DN_HEADS, 1)), full((1, DN_DV))],
        out_specs=[per_b((1, CONV_CH)), per_b((1, CONV_CH)), per_b((BLK, LANES)), per_b((BLK, LANES)),
                   per_b((2, H_PER_GROUP, HEAD_DIM)), per_b((2, H_PER_GROUP, HEAD_DIM)),
                   per_b((2, H_PER_GROUP, HEAD_DIM)), per_b((DN_HEADS, DN_DK, DN_DV))],
        out_shape=[jax.ShapeDtypeStruct((b, 1, CONV_CH), F32), jax.ShapeDtypeStruct((b, 1, CONV_CH), F32),
                   tile_out, tile_out, new_out, new_out, new_out,
                   jax.ShapeDtypeStruct((b, DN_HEADS, DN_DK, DN_DV), F32)],
        compiler_params=pltpu.CompilerParams(
            dimension_semantics=("parallel",), vmem_limit_bytes=V7X_VMEM_LIMIT_BYTES),
        name="sample_mix",
    )(rows, p3, rows, conv_bufs, conv_w, row(conv_b), row(ln_g), row(ln_b),
      *kv_views, row(q_norm_g), row(k_norm_g),
      qkt, sct, wt, scv, wv, s0s, col(a_log), col(dt_bias), row(o_norm_g))
    ya, glu, yb, yc, n0, n1, n2, s_new = outs
    return (ya.reshape(b, CONV_CH), glu, yb.reshape(b, ATT_OUT), yc.reshape(b, DN_OUT), (n0, n1, n2), s_new)


MERGE_TM = 1024
MERGE_TN = 512
assert COL_MLOG % MERGE_TN == 0 and D_MODEL % MERGE_TN == 0


def _merge_body(ya_ref, yb_ref, yc_ref, ga_ref, gb_ref, gc_ref, w_ref, o_ref, lhs_ref):
    @pl.when(pl.program_id(1) == 0)
    def _():
        for b, y_ref in enumerate((ya_ref, yb_ref, yc_ref)):
            lhs_ref[b] = y_ref[...].astype(BF16)

    acc = None
    for b, g_ref in enumerate((ga_ref, gb_ref, gc_ref)):
        part = jax.nn.sigmoid(g_ref[...]) * jnp.dot(
            lhs_ref[b], w_ref[b * CONV_CH:(b + 1) * CONV_CH, :], preferred_element_type=F32)
        acc = part if acc is None else acc + part
    o_ref[...] = acc


def _branch_merge(ya, yb, yc, proj, w_branch, layer):
    assert CONV_CH == ATT_OUT == DN_OUT
    m = ya.shape[0]
    tm = min(MERGE_TM, m)
    y_spec = pl.BlockSpec((tm, CONV_CH), lambda i, j: (i, 0))

    def gate_spec(b):
        blk0 = (COL_MLOG + b * D_MODEL) // MERGE_TN
        return pl.BlockSpec((tm, MERGE_TN), lambda i, j: (i, blk0 + j))

    return pl.pallas_call(
        _merge_body,
        grid=(_cdiv(m, tm), D_MODEL // MERGE_TN),
        in_specs=[y_spec, y_spec, y_spec, gate_spec(0), gate_spec(1), gate_spec(2),
                  _layer_w_spec(layer, N_BRANCH * CONV_CH, MERGE_TN)],
        out_specs=pl.BlockSpec((tm, MERGE_TN), lambda i, j: (i, j)),
        out_shape=jax.ShapeDtypeStruct((m, D_MODEL), F32),
        scratch_shapes=[pltpu.VMEM((N_BRANCH, tm, CONV_CH), BF16)],
        compiler_params=pltpu.CompilerParams(
            dimension_semantics=("parallel", "arbitrary"), vmem_limit_bytes=V7X_VMEM_LIMIT_BYTES),
        name="branch_merge",
    )(ya, yb, yc, proj, proj, proj, w_branch)


def _prompt_layer(x2, layer, caches, cache_bufs, norm_g, w_in_r, conv_w, conv_b, ln_g, ln_b, q_norm_g, k_norm_g,
                  dconv_w, a_log, dt_bias, o_norm_g, w_branch, w_out):
    t = x2.shape[0]
    proj, cache_bufs = _proj_shift(x2, w_in_r, norm_g, layer, caches, cache_bufs, tm=1024, tn=PROJ_TN)
    ya, conv_new = _conv_a_prompt(proj, conv_w, conv_b, ln_g, ln_b)
    yb, kv0, kv1, kv2 = _attn_prompt(proj, q_norm_g, k_norm_g)
    yc, s_new = _deltanet_prompt(proj, dconv_w, a_log, dt_bias, o_norm_g)
    y = _branch_merge(ya, yb, yc, proj, w_branch, layer)
    out = _proj(y, w_out, layer, resid=x2, tm=1024, tn=1024)
    sc_new = proj[t - (DN_SHORT_K - 1):, COL_C_QKV:COL_C_QKV + DN_QKV]
    kv_new = [kv.reshape(1, 2, w, H_PER_GROUP, HEAD_DIM) for kv, (w, _) in zip((kv0, kv1, kv2), ATT_PATTERNS)]
    return out, conv_new[None], kv_new, sc_new[None], s_new[None], cache_bufs


def _sample_layer(x, layer, conv_bufs, kv_bufs, sc_buf, s0s, norm_g, w_in_r, conv_w, conv_b, ln_g, ln_b,
                  q_norm_g, k_norm_g, dconv_w, a_log, dt_bias, o_norm_g, w_branch, w_out):
    B, T, _ = x.shape
    assert T == 1
    x2 = x.reshape(B, D_MODEL)
    proj2 = _proj_few_rows(x2, w_in_r, layer, norm_g, tn=PROJ_TN)
    ya, glu, yb, yc, kv_new, s_new = _sample_mix(proj2, layer, conv_bufs, kv_bufs, sc_buf, s0s, conv_w, conv_b,
                                                 ln_g, ln_b, q_norm_g, k_norm_g, dconv_w, a_log, dt_bias, o_norm_g)
    y = _branch_merge(ya, yb, yc, proj2, w_branch, layer)
    out = _proj(y, w_out, layer, resid=x2, tm=B, tn=512).reshape(B, T, D_MODEL)
    conv_new = jnp.concatenate([conv_bufs[layer, :, 1:], glu], axis=1)
    sc_new = jnp.concatenate([sc_buf[:, 1:], proj2[:, None, COL_C_QKV:COL_C_QKV + DN_QKV]], axis=1)
    return out, conv_new, kv_new, sc_new, s_new


def kernel(x_prompt, x_sample, state_conv, cache_kv_w128, cache_kv_w512, cache_kv_w2048, state_short_conv, state_delta, norm_g, w_in, conv_w, conv_b, ln_g, ln_b, q_norm_g, k_norm_g, dconv_w, a_log, dt_bias, o_norm_g, w_branch, w_out):
    bp, t, _ = x_prompt.shape
    assert bp == 1
    xp, xs = x_prompt.reshape(t, D_MODEL), x_sample
    conv_p, conv_s, sc_p, sc_s, d_p, d_s = [], [], [], [], [], []
    kv_p = [[], [], []]
    kv_s = [[], [], []]
    caches = (cache_kv_w128, cache_kv_w512, cache_kv_w2048)
    cache_bufs = None
    w_in_r, w_branch_b, w_out_b = _prep_w_in(w_in), w_branch.astype(BF16), w_out.astype(BF16)
    for l in range(DEPTH):
        wts = (norm_g[l], w_in_r, conv_w[l], conv_b[l], ln_g[l], ln_b[l], q_norm_g[l], k_norm_g[l],
               dconv_w[l], a_log[l], dt_bias[l], o_norm_g[l], w_branch_b, w_out_b)
        xp, c_new, kv_new, sc_new, s_new, cache_bufs = _prompt_layer(xp, l, caches, cache_bufs, *wts)
        conv_p.append(c_new)
        sc_p.append(sc_new)
        d_p.append(s_new)
        for gi in range(N_ATT_GROUPS):
            kv_p[gi].append(kv_new[gi])
        xs, c_new, kv_new, sc_new, s_new = _sample_layer(
            xs, l, state_conv, (cache_kv_w128, cache_kv_w512, cache_kv_w2048),
            state_short_conv[l], state_delta, *wts)
        conv_s.append(c_new)
        sc_s.append(sc_new)
        d_s.append(s_new)
        for gi in range(N_ATT_GROUPS):
            kv_s[gi].append(kv_new[gi])
    kv_s = [[_cache_append(cache_bufs[gi], jnp.stack(kv_s[gi]))] for gi in range(N_ATT_GROUPS)]
    return (xp.reshape(bp, t, D_MODEL), xs,
            jnp.stack(conv_p), jnp.stack(conv_s),
            jnp.stack(kv_p[0]), kv_s[0][0],
            jnp.stack(kv_p[1]), kv_s[1][0],
            jnp.stack(kv_p[2]), kv_s[2][0],
            jnp.stack(sc_p), jnp.stack(sc_s),
            jnp.stack(d_p), jnp.stack(d_s))
```

```python
import functools

import jax
import jax.numpy as jnp
from jax import lax
from jax.experimental import pallas as pl
from jax.experimental.pallas import tpu as pltpu

D_MODEL = 2048
DEPTH = 2
CONV_CH = 1024
CONV_K = 31
ATT_PATTERNS = ((128, 1), (512, 4), (2048, 16))
N_ATT_GROUPS = 3
H_PER_GROUP = 8
HEAD_DIM = 128
ATT_QKV = N_ATT_GROUPS * H_PER_GROUP * HEAD_DIM
ATT_OUT = H_PER_GROUP * HEAD_DIM
ATT_BLOCK = 128
DN_HEADS = 8
DN_DK = 128
DN_DV = 128
DN_QK = DN_HEADS * DN_DK
DN_QKV = DN_HEADS * (2 * DN_DK + DN_DV)
DN_OUT = DN_HEADS * DN_DV
DN_SHORT_K = 4
DN_CHUNK = 64
N_BRANCH = 3
IN_SPLITS = (CONV_CH, CONV_CH, CONV_CH, ATT_QKV, ATT_QKV, ATT_QKV, ATT_OUT,
             DN_QKV, DN_OUT, DN_HEADS, DN_HEADS, N_BRANCH * D_MODEL)
D_IN = sum(IN_SPLITS)
EPS = 1e-6
NEG_INF = -1e30
F32 = jnp.float32
BF16 = jnp.bfloat16

V7X_VMEM_LIMIT_BYTES = 56 * 1024 * 1024


def _cdiv(a, b):
    return -(-a // b)


def _mm(lhs, w, w_t):
    if w_t:
        return lax.dot_general(lhs, w, (((1,), (1,)), ((), ())), preferred_element_type=F32)
    return jnp.dot(lhs, w, preferred_element_type=F32)


def _proj_body(*refs, normalize, residual):
    a_ref, g_ref, w_ref = refs[:3]
    r_ref = refs[3] if residual else None
    o_ref, lhs_ref = refs[-2:]

    @pl.when(pl.program_id(1) == 0)
    def _():
        a = a_ref[...]
        if normalize:
            a = a * lax.rsqrt(jnp.mean(a * a, axis=-1, keepdims=True) + EPS) * g_ref[...]
        lhs_ref[...] = a.astype(BF16)

    acc = _mm(lhs_ref[...], w_ref[...], False)
    if residual:
        acc = acc + r_ref[...]
    o_ref[...] = acc


def _layer_w_spec(layer, k, tn, w_t=False):
    if w_t:
        return pl.BlockSpec((None, tn, k), lambda i, j: (layer, j, 0))
    return pl.BlockSpec((None, k, tn), lambda i, j: (layer, 0, j))


def _proj(a, w, layer, gain=None, resid=None, *, tm, tn):
    m, k = a.shape
    n = w.shape[2]
    tm = min(tm, m)
    tn = min(tn, n)
    normalize = gain is not None
    if gain is None:
        gain = jnp.ones((k,), F32)
    in_specs = [
        pl.BlockSpec((tm, k), lambda i, j: (i, 0)),
        pl.BlockSpec((1, k), lambda i, j: (0, 0)),
        _layer_w_spec(layer, k, tn),
    ]
    args = [a, gain.reshape(1, k), w]
    if resid is not None:
        in_specs.append(pl.BlockSpec((tm, tn), lambda i, j: (i, j)))
        args.append(resid)
    return pl.pallas_call(
        functools.partial(_proj_body, normalize=normalize, residual=resid is not None),
        grid=(_cdiv(m, tm), _cdiv(n, tn)),
        in_specs=in_specs,
        out_specs=pl.BlockSpec((tm, tn), lambda i, j: (i, j)),
        out_shape=jax.ShapeDtypeStruct((m, n), F32),
        scratch_shapes=[pltpu.VMEM((tm, k), BF16)],
        compiler_params=pltpu.CompilerParams(
            dimension_semantics=("parallel", "arbitrary"),
            vmem_limit_bytes=V7X_VMEM_LIMIT_BYTES),
        name="proj",
    )(*args)


def _proj_few_rows_body(a_ref, g_ref, w_ref, o_ref, lhs_ref):
    @pl.when(pl.program_id(0) == 0)
    def _():
        a = a_ref[...]
        lhs_ref[...] = (a * lax.rsqrt(jnp.mean(a * a, axis=-1, keepdims=True) + EPS) * g_ref[...]).astype(BF16)

    o_ref[...] = _mm(w_ref[...], lhs_ref[...], True)


def _proj_few_rows(a, w, layer, gain, *, tn):
    b, k = a.shape
    n = w.shape[1]
    assert n % tn == 0
    out_t = pl.pallas_call(
        _proj_few_rows_body,
        grid=(n // tn,),
        in_specs=[pl.BlockSpec((b, k), lambda j: (0, 0)),
                  pl.BlockSpec((1, k), lambda j: (0, 0)),
                  pl.BlockSpec((None, tn, k), lambda j: (layer, j, 0))],
        out_specs=pl.BlockSpec((tn, b), lambda j: (j, 0)),
        out_shape=jax.ShapeDtypeStruct((n, b), F32),
        scratch_shapes=[pltpu.VMEM((b, k), BF16)],
        compiler_params=pltpu.CompilerParams(
            dimension_semantics=("arbitrary",), vmem_limit_bytes=V7X_VMEM_LIMIT_BYTES),
        name="proj_few_rows",
    )(a, gain.reshape(1, k), w)
    return out_t.T


COPY_PARTS = 3


def _shift_copies(layer, caches, outs, stage, sems, pair, part, to_vmem):
    b, kv = pair // 2, pair % 2
    w0, w1, w2 = (c.shape[3] for c in caches)
    half = w2 // 2

    def mk(g, src0, rows, dst0, slot0, si):
        buf = stage.at[part, pl.ds(slot0, rows)]
        if to_vmem:
            return pltpu.make_async_copy(caches[g].at[layer, b, kv, pl.ds(src0, rows)], buf, sems.at[0, part, si])
        return pltpu.make_async_copy(buf, outs[g].at[layer, b, kv, pl.ds(dst0, rows)], sems.at[1, part, si])

    if part == 0:
        return [mk(2, 1, half, 0, 0, 0)]
    if part == 1:
        return [mk(2, 1 + half, w2 - 1 - half, half, 0, 0)]
    return [mk(1, 1, w1 - 1, 0, 0, 0), mk(0, 1, w0 - 1, 0, w1, 1)]


def _proj_shift_body(*refs, layer, n_slabs, qk_tiles):
    a_ref, g_ref, w_ref, qg_ref, kg_ref = refs[:5]
    caches = refs[5:8]
    o_ref = refs[-7]
    outs = refs[-6:-3]
    lhs_ref, stage, sems = refs[-3:]
    step = pl.program_id(0) * pl.num_programs(1) + pl.program_id(1)

    def slab_ops(slab, fn):
        for part in range(COPY_PARTS):
            @pl.when((slab >= 0) & (slab < n_slabs) & (slab % COPY_PARTS == part))
            def _():
                fn(slab // COPY_PARTS, part)

    def start_in(pair, part):
        for c in _shift_copies(layer, caches, outs, stage, sems, pair, part, True):
            c.start()

    def turn_around(pair, part):
        for c in _shift_copies(layer, caches, outs, stage, sems, pair, part, True):
            c.wait()
        for c in _shift_copies(layer, caches, outs, stage, sems, pair, part, False):
            c.start()

    def finish(pair, part):
        for c in _shift_copies(layer, caches, outs, stage, sems, pair, part, False):
            c.wait()

    slab_ops(step - 2, finish)
    slab_ops(step - 1, turn_around)
    slab_ops(step, start_in)

    @pl.when(pl.program_id(1) == 0)
    def _():
        a = a_ref[...]
        lhs_ref[...] = (a * lax.rsqrt(jnp.mean(a * a, axis=-1, keepdims=True) + EPS) * g_ref[...]).astype(BF16)

    o_ref[...] = _mm(lhs_ref[...], w_ref[...], True)

    j = pl.program_id(1)
    q_lo, k_lo, k_hi = qk_tiles

    @pl.when((j >= q_lo) & (j < k_hi))
    def _():
        gain = jnp.where(j < k_lo, qg_ref[...], kg_ref[...])
        tm, tn = o_ref.shape
        for c0 in range(0, tn, HEAD_DIM):
            for r0 in range(0, tm, QK_NORM_ROWS):
                x = o_ref[r0:r0 + QK_NORM_ROWS, c0:c0 + HEAD_DIM]
                o_ref[r0:r0 + QK_NORM_ROWS, c0:c0 + HEAD_DIM] = (
                    x * lax.rsqrt(jnp.mean(x * x, axis=-1, keepdims=True) + EPS) * gain)


QK_NORM_ROWS = 256


def _proj_shift(a, w, gain, q_gain, k_gain, layer, caches, bufs, *, tm, tn):
    m, k = a.shape
    n = w.shape[1]
    assert m % tm == 0 and n % tn == 0
    grid = (m // tm, n // tn)
    nb = caches[0].shape[1]
    n_slabs = nb * 2 * COPY_PARTS
    assert n_slabs + 2 <= grid[0] * grid[1]
    _, _, _, w2, heads, dh = caches[2].shape
    assert caches[1].shape[3] + caches[0].shape[3] <= w2 // 2
    assert COL_Q % tn == 0 and COL_K % tn == 0 and COL_V % tn == 0 and tn % HEAD_DIM == 0
    qk_tiles = (COL_Q // tn, COL_K // tn, COL_V // tn)
    any_spec = pl.BlockSpec(memory_space=pl.ANY)
    head_row = pl.BlockSpec((1, HEAD_DIM), lambda i, j: (0, 0))
    in_specs = [pl.BlockSpec((tm, k), lambda i, j: (i, 0)),
                pl.BlockSpec((1, k), lambda i, j: (0, 0)),
                _layer_w_spec(layer, k, tn, True),
                head_row, head_row,
                any_spec, any_spec, any_spec]
    args = [a, gain.reshape(1, k), w, q_gain.reshape(1, HEAD_DIM), k_gain.reshape(1, HEAD_DIM), *caches]
    aliases = {}
    if bufs is not None:
        in_specs += [any_spec] * 3
        aliases = {len(args) + g: 1 + g for g in range(3)}
        args += list(bufs)
    outs = pl.pallas_call(
        functools.partial(_proj_shift_body, layer=layer, n_slabs=n_slabs, qk_tiles=qk_tiles),
        grid=grid,
        in_specs=in_specs,
        out_specs=[pl.BlockSpec((tm, tn), lambda i, j: (i, j)), any_spec, any_spec, any_spec],
        out_shape=[jax.ShapeDtypeStruct((m, n), F32)] + [jax.ShapeDtypeStruct(c.shape, c.dtype) for c in caches],
        scratch_shapes=[pltpu.VMEM((tm, k), BF16),
                        pltpu.VMEM((COPY_PARTS, w2 // 2, heads, dh), F32),
                        pltpu.SemaphoreType.DMA((2, COPY_PARTS, 2))],
        input_output_aliases=aliases,
        compiler_params=pltpu.CompilerParams(
            dimension_semantics=("arbitrary", "arbitrary"),
            vmem_limit_bytes=V7X_VMEM_LIMIT_BYTES),
        name="proj_shift",
    )(*args)
    return outs[0], tuple(outs[1:])


def _cache_append_body(buf_ref, new_ref, o_ref):
    del buf_ref
    o_ref[0, :, :, 0] = new_ref[0]


def _cache_append(buf, new_rows):
    nl, b, _, lb, h, dh = buf.shape
    return pl.pallas_call(
        _cache_append_body,
        grid=(nl,),
        in_specs=[pl.BlockSpec(memory_space=pl.ANY),
                  pl.BlockSpec((1, b, 2, h, dh), lambda l: (l, 0, 0, 0, 0))],
        out_specs=pl.BlockSpec((1, b, 2, 1, h, dh), lambda l: (l, 0, 0, lb - 1, 0, 0)),
        out_shape=jax.ShapeDtypeStruct(buf.shape, buf.dtype),
        input_output_aliases={0: 0},
        compiler_params=pltpu.CompilerParams(
            dimension_semantics=("arbitrary",), vmem_limit_bytes=V7X_VMEM_LIMIT_BYTES),
        name="cache_append",
    )(buf, new_rows)


LANES = 128
COL_A_VAL = 0
COL_A_GLU = CONV_CH
COL_A_GATE = 2 * CONV_CH
COL_Q = 3 * CONV_CH
COL_K = COL_Q + ATT_QKV
COL_V = COL_K + ATT_QKV
COL_B_GATE = COL_V + ATT_QKV
COL_C_QKV = COL_B_GATE + ATT_OUT
COL_C_GATE = COL_C_QKV + DN_QKV
COL_MLOG = COL_C_GATE + DN_OUT
COL_BETA = COL_MLOG + N_BRANCH * D_MODEL
COL_ALPHA = COL_BETA + DN_HEADS
W_PREP_TN = 256
N_PROJ = _cdiv(D_IN, W_PREP_TN) * W_PREP_TN
PROJ_TN = 768
FEW_ROWS_TN = 1984
assert N_PROJ % PROJ_TN == 0 and N_PROJ % FEW_ROWS_TN == 0
_SRC_BETA = COL_MLOG
_BA = 2 * DN_HEADS
_T_MLOG = COL_MLOG // W_PREP_TN
_T_BETA = COL_BETA // W_PREP_TN
assert COL_MLOG % W_PREP_TN == 0 and COL_BETA % W_PREP_TN == 0 and _T_BETA == N_PROJ // W_PREP_TN - 1
assert _BA % 16 == 0 and D_IN % _BA == 0


def _prep_w_in_body(src_ref, nxt_ref, ba_ref, o_ref):
    t = pl.program_id(1)
    k = src_ref.shape[1]

    @pl.when(t < _T_MLOG)
    def _():
        o_ref[...] = src_ref[...].astype(BF16)

    @pl.when((t >= _T_MLOG) & (t < _T_BETA))
    def _():
        o_ref[0:W_PREP_TN - _BA, :] = src_ref[_BA:W_PREP_TN, :].astype(BF16)
        o_ref[W_PREP_TN - _BA:W_PREP_TN, :] = nxt_ref[...].astype(BF16)

    @pl.when(t == _T_BETA)
    def _():
        o_ref[0:_BA, :] = ba_ref[...].astype(BF16)
        o_ref[_BA:W_PREP_TN, :] = jnp.zeros((W_PREP_TN - _BA, k), BF16)


def _prep_w_in(w_in):
    nl, k, d_in = w_in.shape
    assert d_in == D_IN
    w_t = jnp.swapaxes(w_in, 1, 2)
    per = W_PREP_TN // _BA
    last_src = (COL_BETA + _BA) // W_PREP_TN - 1
    last_nxt = D_IN // _BA - 1
    return pl.pallas_call(
        _prep_w_in_body,
        grid=(nl, N_PROJ // W_PREP_TN),
        in_specs=[pl.BlockSpec((None, W_PREP_TN, k), lambda l, t: (l, jnp.minimum(t, last_src), 0)),
                  pl.BlockSpec((None, _BA, k), lambda l, t: (l, jnp.minimum((t + 1) * per, last_nxt), 0)),
                  pl.BlockSpec((None, _BA, k), lambda l, t: (l, _SRC_BETA // _BA, 0))],
        out_specs=pl.BlockSpec((None, W_PREP_TN, k), lambda l, t: (l, t, 0)),
        out_shape=jax.ShapeDtypeStruct((nl, N_PROJ, k), BF16),
        compiler_params=pltpu.CompilerParams(
            dimension_semantics=("parallel", "arbitrary"), vmem_limit_bytes=V7X_VMEM_LIMIT_BYTES),
        name="w_prep",
    )(w_t, w_t, w_t)


ATT_SB = ATT_BLOCK * max(d for _, d in ATT_PATTERNS)
ATT_SPAN = ATT_BLOCK
assert all(w // d == ATT_SPAN for w, d in ATT_PATTERNS)


def _rows(start, size, stride):
    return pl.ds(start, size) if stride == 1 else pl.ds(start, size, stride=stride)


def _rms_rows(x, gain):
    return x * lax.rsqrt(jnp.mean(x * x, axis=-1, keepdims=True) + EPS) * gain


def _attn_prompt_body(q0, q1, q2, k0, k1, k2, v0, v1, v2, gate_ref,
                      yb_ref, kv0, kv1, kv2,
                      kr0, kr1, kr2, vr0, vr1, vr2, on0, on1, on2, ln0, ln1, ln2):
    q_refs, k_refs, v_refs = (q0, q1, q2), (k0, k1, k2), (v0, v1, v2)
    kv_refs = (kv0, kv1, kv2)
    kres, vres = (kr0, kr1, kr2), (vr0, vr1, vr2)
    onat, lnat = (on0, on1, on2), (ln0, ln1, ln2)
    i = pl.program_id(1)
    last = pl.num_programs(1) - 1
    scale = HEAD_DIM ** -0.5
    qi = lax.broadcasted_iota(jnp.int32, (ATT_BLOCK, 2 * ATT_BLOCK), 0)
    kj = lax.broadcasted_iota(jnp.int32, (ATT_BLOCK, 2 * ATT_BLOCK), 1)
    delta = kj - qi
    band = (delta >= 0) & (delta <= ATT_SPAN)

    for g, (_, d) in enumerate(ATT_PATTERNS):
        nblk = ATT_SB // (ATT_BLOCK * d)
        units = ATT_BLOCK * nblk

        @pl.when(i == 0)
        def _():
            kres[g][:, 0:ATT_BLOCK, :] = jnp.zeros((d, ATT_BLOCK, HEAD_DIM), BF16)
            vres[g][:, 0:ATT_BLOCK, :] = jnp.zeros((d, ATT_BLOCK, HEAD_DIM), BF16)

        @pl.when(i > 0)
        def _():
            kres[g][:, 0:ATT_BLOCK, :] = kres[g][:, units:units + ATT_BLOCK, :]
            vres[g][:, 0:ATT_BLOCK, :] = vres[g][:, units:units + ATT_BLOCK, :]

        for r in range(d):
            for c in range(nblk):
                rows = _rows(c * ATT_BLOCK * d + r, ATT_BLOCK, d)
                dst = slice(ATT_BLOCK + c * ATT_BLOCK, ATT_BLOCK + (c + 1) * ATT_BLOCK)
                kres[g][r, dst, :] = k_refs[g][rows, :].astype(BF16)
                vres[g][r, dst, :] = v_refs[g][rows, :].astype(BF16)

        for r in range(d):
            for c in range(nblk):
                rows = _rows(c * ATT_BLOCK * d + r, ATT_BLOCK, d)
                q = q_refs[g][rows, :].astype(BF16)
                kb = kres[g][r, c * ATT_BLOCK:(c + 2) * ATT_BLOCK, :]
                vb = vres[g][r, c * ATT_BLOCK:(c + 2) * ATT_BLOCK, :]
                s = lax.dot_general(q, kb, (((1,), (1,)), ((), ())), preferred_element_type=F32) * scale
                if c == 0:
                    first = jnp.where(i == 0, ATT_BLOCK, 0)
                    mask = band & (kj >= first)
                else:
                    mask = band
                s = jnp.where(mask, s, NEG_INF)
                m = jnp.max(s, axis=-1, keepdims=True)
                p = jnp.exp(s - m)
                l = jnp.sum(p, axis=-1, keepdims=True)
                o = jnp.dot(p.astype(BF16), vb, preferred_element_type=F32) / l
                onat[g][rows, :] = o
                lnat[g][rows, :] = jnp.broadcast_to(m + jnp.log(l), (ATT_BLOCK, HEAD_DIM))

    chunk = 256
    for c in range(ATT_SB // chunk):
        rows = slice(c * chunk, (c + 1) * chunk)
        l0, l1, l2 = lnat[0][rows, :], lnat[1][rows, :], lnat[2][rows, :]
        m = jnp.maximum(jnp.maximum(l0, l1), l2)
        e0, e1, e2 = jnp.exp(l0 - m), jnp.exp(l1 - m), jnp.exp(l2 - m)
        y = (e0 * onat[0][rows, :] + e1 * onat[1][rows, :] + e2 * onat[2][rows, :]) / (e0 + e1 + e2)
        gate = gate_ref[rows, :]
        yb_ref[rows, :] = y * (gate * jax.nn.sigmoid(gate))

    @pl.when(i == last)
    def _():
        for g, (window, _) in enumerate(ATT_PATTERNS):
            tail = slice(ATT_SB - window, ATT_SB)
            kv_refs[g][0, :, :] = k_refs[g][tail, :]
            kv_refs[g][1, :, :] = v_refs[g][tail, :]


def _attn_prompt(proj):
    t = proj.shape[0]
    assert t % ATT_SB == 0
    n_sb = t // ATT_SB

    def col_spec(col0, g):
        blk0 = col0 // LANES + g * H_PER_GROUP
        return pl.BlockSpec((ATT_SB, HEAD_DIM), lambda h, i: (i, blk0 + h))

    in_specs = ([col_spec(COL_Q, g) for g in range(N_ATT_GROUPS)]
                + [col_spec(COL_K, g) for g in range(N_ATT_GROUPS)]
                + [col_spec(COL_V, g) for g in range(N_ATT_GROUPS)]
                + [col_spec(COL_B_GATE, 0)])
    out_specs = [pl.BlockSpec((ATT_SB, HEAD_DIM), lambda h, i: (i, h))]
    out_shape = [jax.ShapeDtypeStruct((t, ATT_OUT), F32)]
    scratch = []
    for window, _ in ATT_PATTERNS:
        out_specs.append(pl.BlockSpec((2, window, HEAD_DIM), lambda h, i: (0, 0, h)))
        out_shape.append(jax.ShapeDtypeStruct((2, window, ATT_OUT), F32))
    for _ in range(2):
        for _, d in ATT_PATTERNS:
            scratch.append(pltpu.VMEM((d, ATT_BLOCK + ATT_SB // d, HEAD_DIM), BF16))
    for _ in range(2 * N_ATT_GROUPS):
        scratch.append(pltpu.VMEM((ATT_SB, HEAD_DIM), F32))
    return pl.pallas_call(
        _attn_prompt_body,
        grid=(H_PER_GROUP, n_sb),
        in_specs=in_specs,
        out_specs=out_specs,
        out_shape=out_shape,
        scratch_shapes=scratch,
        compiler_params=pltpu.CompilerParams(
            dimension_semantics=("parallel", "arbitrary"),
            vmem_limit_bytes=V7X_VMEM_LIMIT_BYTES),
        name="attn_prompt",
    )(*([proj] * 10))


CONV_TT = 256
CONV_HALO = 32
CONV_RC = 64
CONV_LC = 256
SUBLANES = 8
CONV_SH_ROWS = 40
assert (CONV_HALO + CONV_TT - SUBLANES) % CONV_SH_ROWS == 0


def _silu(x):
    return x * jax.nn.sigmoid(x)


def _conv_a_body(aval_ref, aglu_ref, agate_ref, w_ref, b_ref, lg_ref, lb_ref,
                 ya_ref, tail_ref, ext_ref, y_ref, sh_ref):
    i = pl.program_id(0)

    @pl.when(i == 0)
    def _():
        ext_ref[0:CONV_HALO, :] = jnp.zeros((CONV_HALO, CONV_CH), F32)

    @pl.when(i > 0)
    def _():
        ext_ref[0:CONV_HALO, :] = ext_ref[CONV_TT:CONV_TT + CONV_HALO, :]

    for rc in range(CONV_TT // CONV_RC):
        rows = slice(rc * CONV_RC, (rc + 1) * CONV_RC)
        ext_ref[CONV_HALO + rc * CONV_RC:CONV_HALO + (rc + 1) * CONV_RC, :] = (
            aval_ref[rows, :] * jax.nn.sigmoid(aglu_ref[rows, :]))

    n_sh = CONV_HALO + CONV_TT - SUBLANES
    for s in range(1, SUBLANES):
        for r0 in range(0, n_sh, CONV_SH_ROWS):
            sh_ref[s, r0:r0 + CONV_SH_ROWS, :] = ext_ref[r0 + s:r0 + s + CONV_SH_ROWS, :]

    first = CONV_HALO - (CONV_K - 1)
    for lc in range(CONV_CH // CONV_LC):
        lanes = slice(lc * CONV_LC, (lc + 1) * CONV_LC)
        for rc in range(CONV_TT // CONV_RC):
            acc = jnp.broadcast_to(b_ref[:, lanes], (CONV_RC, CONV_LC))
            for k in range(CONV_K):
                s = (first + k) % SUBLANES
                r0 = first + k - s + rc * CONV_RC
                src = ext_ref[r0:r0 + CONV_RC, lanes] if s == 0 else sh_ref[s, r0:r0 + CONV_RC, lanes]
                acc = acc + w_ref[k:k + 1, lanes] * src
            y_ref[rc * CONV_RC:(rc + 1) * CONV_RC, lanes] = acc

    for rc in range(CONV_TT // CONV_RC):
        rows = slice(rc * CONV_RC, (rc + 1) * CONV_RC)
        y = y_ref[rows, :]
        yc = y - jnp.mean(y, axis=-1, keepdims=True)
        yn = yc * lax.rsqrt(jnp.mean(yc * yc, axis=-1, keepdims=True) + EPS)
        ya_ref[rows, :] = _silu(yn * lg_ref[...] + lb_ref[...]) * _silu(agate_ref[rows, :])

    @pl.when(i == pl.num_programs(0) - 1)
    def _():
        tail_ref[...] = ext_ref[CONV_HALO + CONV_TT - (CONV_K - 1):CONV_HALO + CONV_TT, :]


def _conv_a_prompt(proj, conv_w, conv_b, ln_g, ln_b):
    t = proj.shape[0]
    assert t % CONV_TT == 0
    nb = CONV_CH // LANES

    def col_spec(col0):
        return pl.BlockSpec((CONV_TT, CONV_CH), lambda i: (i, col0 // CONV_CH))

    def full(shape):
        return pl.BlockSpec(shape, lambda i: (0,) * len(shape))

    del nb
    return pl.pallas_call(
        _conv_a_body,
        grid=(t // CONV_TT,),
        in_specs=[col_spec(COL_A_VAL), col_spec(COL_A_GLU), col_spec(COL_A_GATE),
                  full((CONV_K, CONV_CH)), full((1, CONV_CH)), full((1, CONV_CH)), full((1, CONV_CH))],
        out_specs=[pl.BlockSpec((CONV_TT, CONV_CH), lambda i: (i, 0)), full((CONV_K - 1, CONV_CH))],
        out_shape=[jax.ShapeDtypeStruct((t, CONV_CH), F32),
                   jax.ShapeDtypeStruct((CONV_K - 1, CONV_CH), F32)],
        scratch_shapes=[pltpu.VMEM((CONV_HALO + CONV_TT, CONV_CH), F32),
                        pltpu.VMEM((CONV_TT, CONV_CH), F32),
                        pltpu.VMEM((SUBLANES, CONV_HALO + CONV_TT, CONV_CH), F32)],
        compiler_params=pltpu.CompilerParams(
            dimension_semantics=("arbitrary",), vmem_limit_bytes=V7X_VMEM_LIMIT_BYTES),
        name="conv_a",
    )(proj, proj, proj, conv_w, conv_b.reshape(1, CONV_CH), ln_g.reshape(1, CONV_CH), ln_b.reshape(1, CONV_CH))


DN_TB = 512
DN_NC = DN_TB // DN_CHUNK
DN_HALO = 8
assert DN_DK == DN_DV == LANES


def _split_bf16(x, parts):
    out = []
    for _ in range(parts):
        hi = x.astype(BF16)
        out.append(hi)
        x = x - hi.astype(F32)
    return out


def _dot_nt(a, b):
    return lax.dot_general(a, b, (((1,), (1,)), ((), ())), preferred_element_type=F32)


def _dn_prep_body(xq_ref, xk_ref, xv_ref, hq_ref, hk_ref, hv_ref, ba_ref, wq_ref, wk_ref, wv_ref,
                  pa_ref, pd_ref,
                  u0_ref, w_ref, qd_ref, kdt_ref, qk_ref, e_ref,
                  eq_ref, ek_ref, ev_ref):
    i = pl.program_id(0)
    for ext, halo, x in ((eq_ref, hq_ref, xq_ref), (ek_ref, hk_ref, xk_ref), (ev_ref, hv_ref, xv_ref)):
        ext[0:DN_HALO, :] = jnp.where(i == 0, 0.0, halo[...])
        ext[DN_HALO:DN_HALO + DN_TB, :] = x[...]

    ba = ba_ref[...]
    beta_all = jax.nn.sigmoid(ba)
    z = ba + pd_ref[...]
    g_all = -jnp.exp(pa_ref[...]) * (jnp.maximum(z, 0.0) + jnp.log1p(jnp.exp(-jnp.abs(z))))

    ri = lax.broadcasted_iota(jnp.int32, (DN_CHUNK, DN_CHUNK), 0)
    ci = lax.broadcasted_iota(jnp.int32, (DN_CHUNK, DN_CHUNK), 1)
    causal = ri >= ci
    strict = ri > ci
    tril = causal.astype(BF16)
    eye = (ri == ci).astype(F32)
    first = DN_HALO - (DN_SHORT_K - 1)

    gsums = []
    for c in range(DN_NC):
        g1, g2, g3 = _split_bf16(g_all[c * DN_CHUNK:(c + 1) * DN_CHUNK, :], 3)
        gsums.append(jnp.dot(tril, g1, preferred_element_type=F32) + jnp.dot(tril, g2, preferred_element_type=F32)
                     + jnp.dot(tril, g3, preferred_element_type=F32))

    for c in range(DN_NC):
        rows = slice(c * DN_CHUNK, (c + 1) * DN_CHUNK)
        gsum = gsums[c]
        a_all, rhs_all = [], []
        for h in range(DN_HEADS):
            lanes = slice(h * LANES, (h + 1) * LANES)

            def short_conv(ext, wref):
                acc = None
                for k in range(DN_SHORT_K):
                    r0 = first + c * DN_CHUNK + k
                    term = wref[k:k + 1, lanes] * ext[r0:r0 + DN_CHUNK, lanes]
                    acc = term if acc is None else acc + term
                return _silu(acc)

            q = short_conv(eq_ref, wq_ref)
            k = short_conv(ek_ref, wk_ref)
            v = short_conv(ev_ref, wv_ref)
            q = q * lax.rsqrt(jnp.sum(q * q, axis=-1, keepdims=True) + EPS) * (DN_DK ** -0.5)
            k = k * lax.rsqrt(jnp.sum(k * k, axis=-1, keepdims=True) + EPS)
            bcol = jnp.broadcast_to(beta_all[rows, h:h + 1], (DN_CHUNK, LANES))
            gcol = jnp.broadcast_to(gsum[:, DN_HEADS + h:DN_HEADS + h + 1], (DN_CHUNK, LANES))
            grow = jnp.transpose(gcol)[0:DN_CHUNK, :]
            decay = jnp.where(causal, jnp.exp(gcol[:, 0:DN_CHUNK] - grow), 0.0)
            kb = k * bcol
            kbf = k.astype(BF16)
            a = jnp.where(strict, _dot_nt(kb.astype(BF16), kbf) * decay, 0.0)
            qk = _dot_nt(q.astype(BF16), kbf) * decay
            eg = jnp.exp(gcol)
            a_all.append(a)
            rhs_all.append(jnp.concatenate([v * bcol, kb * eg], axis=1).astype(BF16))
            glast = gcol[DN_CHUNK - 1:DN_CHUNK, :]
            qd_ref[rows, lanes] = (q * eg).astype(BF16)
            kdt_ref[c, lanes, :] = jnp.transpose(k * jnp.exp(glast - gcol)).astype(BF16)
            qk_ref[rows, h * LANES:h * LANES + DN_CHUNK] = qk.astype(BF16)
            qk_ref[rows, h * LANES + DN_CHUNK:(h + 1) * LANES] = jnp.zeros((DN_CHUNK, LANES - DN_CHUNK), BF16)
            e_ref[c, h:h + 1, :] = jnp.exp(glast)

        tinv = [eye - a for a in a_all]
        pb = [a.astype(BF16) for a in a_all]
        for _ in range(5):
            pb = [jnp.dot(x, x, preferred_element_type=F32).astype(BF16) for x in pb]
            tinv = [t + jnp.dot(x, t.astype(BF16), preferred_element_type=F32) for x, t in zip(pb, tinv)]
        for h in range(DN_HEADS):
            lanes = slice(h * LANES, (h + 1) * LANES)
            sol = jnp.dot(tinv[h].astype(BF16), rhs_all[h], preferred_element_type=F32)
            u0_ref[rows, lanes] = sol[:, 0:DN_DV]
            w_ref[rows, lanes] = sol[:, DN_DV:].astype(BF16)


def _dn_rec_body(u0_ref, w_ref, qd_ref, kdt_ref, qk_ref, e_ref, gate_ref, og_ref,
                 yc_ref, sfin_ref, s_ref):
    i = pl.program_id(0)

    @pl.when(i == 0)
    def _():
        s_ref[...] = jnp.zeros(s_ref.shape, F32)

    og = og_ref[...]
    for c in range(DN_NC):
        rows = slice(c * DN_CHUNK, (c + 1) * DN_CHUNK)
        heads = range(DN_HEADS)
        lanes = [slice(h * LANES, (h + 1) * LANES) for h in heads]
        s = [s_ref[h] for h in heads]
        sb = [x.astype(BF16) for x in s]
        ub = [(u0_ref[rows, lanes[h]] - jnp.dot(w_ref[rows, lanes[h]], sb[h], preferred_element_type=F32)
               ).astype(BF16) for h in heads]
        for h in heads:
            s_ref[h] = (e_ref[c, h:h + 1, :] * s[h]
                        + jnp.dot(kdt_ref[c, lanes[h], :], ub[h], preferred_element_type=F32))
        for h in heads:
            o = (jnp.dot(qd_ref[rows, lanes[h]], sb[h], preferred_element_type=F32)
                 + jnp.dot(qk_ref[rows, h * LANES:h * LANES + DN_CHUNK], ub[h], preferred_element_type=F32))
            yc_ref[rows, lanes[h]] = _rms_rows(o, og) * _silu(gate_ref[rows, lanes[h]])

    @pl.when(i == pl.num_programs(0) - 1)
    def _():
        sfin_ref[...] = s_ref[...]


def _deltanet_prompt(proj, dconv_w, a_log, dt_bias, o_norm_g):
    t = proj.shape[0]
    assert t % DN_TB == 0
    nt = t // DN_TB
    qkv_blk = COL_C_QKV // DN_QK
    halo_per_tb = DN_TB // DN_HALO

    def x_spec(j):
        return pl.BlockSpec((DN_TB, DN_QK), lambda i: (i, qkv_blk + j))

    def halo_spec(j):
        return pl.BlockSpec((DN_HALO, DN_QK), lambda i: (jnp.maximum(i * halo_per_tb - 1, 0), qkv_blk + j))

    def w_spec(j):
        return pl.BlockSpec((DN_SHORT_K, DN_QK), lambda i: (0, j))

    row = pl.BlockSpec((1, LANES), lambda i: (0, 0))
    pad_a = jnp.zeros((1, LANES), F32).at[0, DN_HEADS:2 * DN_HEADS].set(a_log)
    pad_d = jnp.zeros((1, LANES), F32).at[0, DN_HEADS:2 * DN_HEADS].set(dt_bias)
    wide = pl.BlockSpec((DN_TB, DN_OUT), lambda i: (i, 0))
    kdt_spec = pl.BlockSpec((DN_NC, DN_OUT, DN_CHUNK), lambda i: (i, 0, 0))
    e_spec = pl.BlockSpec((DN_NC, DN_HEADS, LANES), lambda i: (i, 0, 0))
    u0, w, qd, kdt, qk, e = pl.pallas_call(
        _dn_prep_body,
        grid=(nt,),
        in_specs=[x_spec(0), x_spec(1), x_spec(2), halo_spec(0), halo_spec(1), halo_spec(2),
                  pl.BlockSpec((DN_TB, LANES), lambda i: (i, COL_BETA // LANES)),
                  w_spec(0), w_spec(1), w_spec(2), row, row],
        out_specs=[wide, wide, wide, kdt_spec, wide, e_spec],
        out_shape=[jax.ShapeDtypeStruct((t, DN_OUT), F32),
                   jax.ShapeDtypeStruct((t, DN_OUT), BF16),
                   jax.ShapeDtypeStruct((t, DN_OUT), BF16),
                   jax.ShapeDtypeStruct((t // DN_CHUNK, DN_OUT, DN_CHUNK), BF16),
                   jax.ShapeDtypeStruct((t, DN_OUT), BF16),
                   jax.ShapeDtypeStruct((t // DN_CHUNK, DN_HEADS, LANES), F32)],
        scratch_shapes=[pltpu.VMEM((DN_HALO + DN_TB, DN_QK), F32) for _ in range(3)],
        compiler_params=pltpu.CompilerParams(
            dimension_semantics=("parallel",), vmem_limit_bytes=V7X_VMEM_LIMIT_BYTES),
        name="dn_prep",
    )(proj, proj, proj, proj, proj, proj, proj, dconv_w, dconv_w, dconv_w, pad_a, pad_d)

    state = pl.BlockSpec((DN_HEADS, DN_DK, DN_DV), lambda i: (0, 0, 0))
    yc, s_fin = pl.pallas_call(
        _dn_rec_body,
        grid=(nt,),
        in_specs=[wide, wide, wide, kdt_spec, wide, e_spec,
                  pl.BlockSpec((DN_TB, DN_OUT), lambda i: (i, COL_C_GATE // DN_OUT)), row],
        out_specs=[wide, state],
        out_shape=[jax.ShapeDtypeStruct((t, DN_OUT), F32),
                   jax.ShapeDtypeStruct((DN_HEADS, DN_DK, DN_DV), F32)],
        scratch_shapes=[pltpu.VMEM((DN_HEADS, DN_DK, DN_DV), F32)],
        compiler_params=pltpu.CompilerParams(
            dimension_semantics=("arbitrary",), vmem_limit_bytes=V7X_VMEM_LIMIT_BYTES),
        name="dn_rec",
    )(u0, w, qd, kdt, qk, e, proj, o_norm_g.reshape(1, LANES))
    return yc, s_fin


BLK = CONV_CH // LANES
assert CONV_CH == ATT_OUT == DN_OUT == DN_QK == BLK * LANES and H_PER_GROUP == DN_HEADS == BLK


def _sample_mix_body(a_ref, p3_ref, ba_ref, cst_ref, cw_ref, cb_ref, lg_ref, lb_ref,
                     kv0_ref, kv1_ref, kv2_ref, qg_ref, kg_ref,
                     qkt_ref, sct_ref, wt_ref, scv_ref, wv_ref, s0_ref, al_ref, dtb_ref, og_ref,
                     ya_ref, glu_ref, yb_ref, yc_ref, n0_ref, n1_ref, n2_ref, s_ref):
    def seg(col0):
        return p3_ref[0, col0 // LANES:col0 // LANES + BLK, :]

    glu = a_ref[0, :, COL_A_VAL:COL_A_VAL + CONV_CH] * jax.nn.sigmoid(a_ref[0, :, COL_A_GLU:COL_A_GLU + CONV_CH])
    y = (jnp.sum(cst_ref[0] * cw_ref[0:CONV_K - 1, :], axis=0, keepdims=True)
         + glu * cw_ref[CONV_K - 1:CONV_K, :] + cb_ref[...])
    yc = y - jnp.mean(y, axis=-1, keepdims=True)
    yn = yc * lax.rsqrt(jnp.mean(yc * yc, axis=-1, keepdims=True) + EPS)
    ya_ref[0] = _silu(yn * lg_ref[...] + lb_ref[...]) * _silu(a_ref[0, :, COL_A_GATE:COL_A_GATE + CONV_CH])
    glu_ref[0] = glu

    scale = HEAD_DIM ** -0.5
    outs, lses = [], []
    for g, (kv_ref, new_ref) in enumerate(((kv0_ref, n0_ref), (kv1_ref, n1_ref), (kv2_ref, n2_ref))):
        q = _rms_rows(seg(COL_Q + g * ATT_OUT), qg_ref[...])
        k_new = _rms_rows(seg(COL_K + g * ATT_OUT), kg_ref[...])
        v_new = seg(COL_V + g * ATT_OUT)
        new_ref[0, 0] = k_new
        new_ref[0, 1] = v_new
        kc = kv_ref[0, 0]
        vc = kv_ref[0, 1]
        s = jnp.sum(kc * q[None], axis=-1, keepdims=True) * scale
        s_new = jnp.sum(k_new * q, axis=-1, keepdims=True) * scale
        m = jnp.maximum(jnp.max(s, axis=0), s_new)
        p = jnp.exp(s - m[None])
        p_new = jnp.exp(s_new - m)
        l = jnp.sum(p, axis=0) + p_new
        outs.append((jnp.sum(p * vc, axis=0) + p_new * v_new) / l)
        lses.append(m + jnp.log(l))
    m = jnp.maximum(jnp.maximum(lses[0], lses[1]), lses[2])
    e = [jnp.exp(x - m) for x in lses]
    yb = (e[0] * outs[0] + e[1] * outs[1] + e[2] * outs[2]) / (e[0] + e[1] + e[2])
    yb_ref[0] = yb * _silu(seg(COL_B_GATE))

    first = DN_SHORT_K - 1
    qk = sct_ref[0, 0] * wt_ref[0]
    for j in range(1, first):
        qk = qk + sct_ref[0, j] * wt_ref[j]
    qk = _silu(qk + qkt_ref[0] * wt_ref[first])
    qk = qk * lax.rsqrt(jnp.sum(qk * qk, axis=0, keepdims=True) + EPS)
    v = scv_ref[0, 0] * wv_ref[0]
    for j in range(1, first):
        v = v + scv_ref[0, j] * wv_ref[j]
    v = _silu(v + seg(COL_C_QKV + 2 * DN_QK) * wv_ref[first])
    hi = lax.broadcasted_iota(jnp.int32, (DN_HEADS, LANES), 0)
    li = lax.broadcasted_iota(jnp.int32, (DN_HEADS, LANES), 1)
    ba = jnp.broadcast_to(ba_ref[0], (DN_HEADS, LANES))
    beta = jax.nn.sigmoid(jnp.sum(jnp.where(li == hi, ba, 0.0), axis=-1, keepdims=True))
    z = jnp.sum(jnp.where(li == hi + DN_HEADS, ba, 0.0), axis=-1, keepdims=True) + dtb_ref[...]
    gdec = jnp.exp(-jnp.exp(al_ref[...]) * (jnp.maximum(z, 0.0) + jnp.log1p(jnp.exp(-jnp.abs(z)))))
    gate = seg(COL_C_GATE)
    for h in range(DN_HEADS):
        qc = qk[:, h:h + 1] * (DN_DK ** -0.5)
        kc = qk[:, DN_HEADS + h:DN_HEADS + h + 1]
        eg = gdec[h:h + 1, :]
        s0 = s0_ref[0, h]
        u = beta[h:h + 1, :] * (v[h:h + 1, :] - eg * jnp.sum(kc * s0, axis=0, keepdims=True))
        s1 = eg * s0 + kc * u
        s_ref[0, h] = s1
        o = jnp.sum(qc * s1, axis=0, keepdims=True)
        yc_ref[0, h:h + 1, :] = _rms_rows(o, og_ref[...]) * _silu(gate[h:h + 1, :])


def _sample_mix(proj2, layer, conv_bufs, kv_bufs, sc_buf, s0s, conv_w, conv_b, ln_g, ln_b, q_norm_g, k_norm_g,
                dconv_w, a_log, dt_bias, o_norm_g):
    b = proj2.shape[0]
    rows = proj2.reshape(b, 1, N_PROJ)
    p3 = proj2.reshape(b, N_PROJ // LANES, LANES)
    nq = 2 * DN_QK
    qkt = proj2[:, COL_C_QKV:COL_C_QKV + nq].reshape(b, 2 * DN_HEADS, DN_DK).transpose(0, 2, 1)
    sct = sc_buf[:, :, :nq].reshape(b, DN_SHORT_K - 1, 2 * DN_HEADS, DN_DK).transpose(0, 1, 3, 2)
    wt = dconv_w[:, :nq].reshape(DN_SHORT_K, 2 * DN_HEADS, DN_DK).transpose(0, 2, 1)
    scv = sc_buf[:, :, nq:].reshape(b, DN_SHORT_K - 1, DN_HEADS, DN_DV)
    wv = dconv_w[:, nq:].reshape(DN_SHORT_K, DN_HEADS, DN_DV)
    kv_views, kv_specs = [], []
    for kv, (window, d) in zip(kv_bufs, ATT_PATTERNS):
        assert kv.shape[1:] == (b, 2, window, H_PER_GROUP, HEAD_DIM) and window == ATT_SPAN * d
        kv_views.append(kv.reshape(kv.shape[0], b, 2, ATT_SPAN, d, H_PER_GROUP, HEAD_DIM))
        kv_specs.append(pl.BlockSpec((None, 1, 2, ATT_SPAN, None, H_PER_GROUP, HEAD_DIM),
                                     lambda i: (layer, i, 0, 0, 0, 0, 0)))

    def full(shape):
        return pl.BlockSpec(shape, lambda i: (0,) * len(shape))

    def per_b(shape):
        return pl.BlockSpec((1,) + shape, lambda i: (i,) + (0,) * len(shape))

    def per_lb(shape):
        return pl.BlockSpec((None, 1) + shape, lambda i: (layer, i) + (0,) * len(shape))

    col = lambda x: x.reshape(-1, 1)
    row = lambda x: x.reshape(1, -1)
    tile_out = jax.ShapeDtypeStruct((b, BLK, LANES), F32)
    new_out = jax.ShapeDtypeStruct((b, 2, H_PER_GROUP, HEAD_DIM), F32)
    outs = pl.pallas_call(
        _sample_mix_body,
        grid=(b,),
        in_specs=[per_b((1, N_PROJ)), per_b((N_PROJ // LANES, LANES)),
                  pl.BlockSpec((1, 1, LANES), lambda i: (i, 0, COL_BETA // LANES)),
                  per_lb((CONV_K - 1, CONV_CH)), full((CONV_K, CONV_CH)),
                  full((1, CONV_CH)), full((1, CONV_CH)), full((1, CONV_CH)),
                  *kv_specs, full((1, HEAD_DIM)), full((1, HEAD_DIM)),
                  per_b((DN_DK, 2 * DN_HEADS)), per_b((DN_SHORT_K - 1, DN_DK, 2 * DN_HEADS)),
                  full((DN_SHORT_K, DN_DK, 2 * DN_HEADS)),
                  per_b((DN_SHORT_K - 1, DN_HEADS, DN_DV)), full((DN_SHORT_K, DN_HEADS, DN_DV)),
                  per_lb((DN_HEADS, DN_DK, DN_DV)), full((DN_HEADS, 1)), full((DN_HEADS, 1)), full((1, DN_DV))],
        out_specs=[per_b((1, CONV_CH)), per_b((1, CONV_CH)), per_b((BLK, LANES)), per_b((BLK, LANES)),
                   per_b((2, H_PER_GROUP, HEAD_DIM)), per_b((2, H_PER_GROUP, HEAD_DIM)),
                   per_b((2, H_PER_GROUP, HEAD_DIM)), per_b((DN_HEADS, DN_DK, DN_DV))],
        out_shape=[jax.ShapeDtypeStruct((b, 1, CONV_CH), F32), jax.ShapeDtypeStruct((b, 1, CONV_CH), F32),
                   tile_out, tile_out, new_out, new_out, new_out,
                   jax.ShapeDtypeStruct((b, DN_HEADS, DN_DK, DN_DV), F32)],
        compiler_params=pltpu.CompilerParams(
            dimension_semantics=("parallel",), vmem_limit_bytes=V7X_VMEM_LIMIT_BYTES),
        name="sample_mix",
    )(rows, p3, rows, conv_bufs, conv_w, row(conv_b), row(ln_g), row(ln_b),
      *kv_views, row(q_norm_g), row(k_norm_g),
      qkt, sct, wt, scv, wv, s0s, col(a_log), col(dt_bias), row(o_norm_g))
    ya, glu, yb, yc, n0, n1, n2, s_new = outs
    return (ya.reshape(b, CONV_CH), glu, yb.reshape(b, ATT_OUT), yc.reshape(b, DN_OUT), (n0, n1, n2), s_new)


MERGE_TM = 1024
MERGE_TN = 512
assert COL_MLOG % MERGE_TN == 0 and D_MODEL % MERGE_TN == 0


def _merge_body(ya_ref, yb_ref, yc_ref, ga_ref, gb_ref, gc_ref, w_ref, o_ref, lhs_ref):
    @pl.when(pl.program_id(1) == 0)
    def _():
        for b, y_ref in enumerate((ya_ref, yb_ref, yc_ref)):
            lhs_ref[b] = y_ref[...].astype(BF16)

    acc = None
    for b, g_ref in enumerate((ga_ref, gb_ref, gc_ref)):
        part = jax.nn.sigmoid(g_ref[...]) * jnp.dot(
            lhs_ref[b], w_ref[b * CONV_CH:(b + 1) * CONV_CH, :], preferred_element_type=F32)
        acc = part if acc is None else acc + part
    o_ref[...] = acc


def _branch_merge(ya, yb, yc, proj, w_branch, layer):
    assert CONV_CH == ATT_OUT == DN_OUT
    m = ya.shape[0]
    tm = min(MERGE_TM, m)
    y_spec = pl.BlockSpec((tm, CONV_CH), lambda i, j: (i, 0))

    def gate_spec(b):
        blk0 = (COL_MLOG + b * D_MODEL) // MERGE_TN
        return pl.BlockSpec((tm, MERGE_TN), lambda i, j: (i, blk0 + j))

    return pl.pallas_call(
        _merge_body,
        grid=(_cdiv(m, tm), D_MODEL // MERGE_TN),
        in_specs=[y_spec, y_spec, y_spec, gate_spec(0), gate_spec(1), gate_spec(2),
                  _layer_w_spec(layer, N_BRANCH * CONV_CH, MERGE_TN)],
        out_specs=pl.BlockSpec((tm, MERGE_TN), lambda i, j: (i, j)),
        out_shape=jax.ShapeDtypeStruct((m, D_MODEL), F32),
        scratch_shapes=[pltpu.VMEM((N_BRANCH, tm, CONV_CH), BF16)],
        compiler_params=pltpu.CompilerParams(
            dimension_semantics=("parallel", "arbitrary"), vmem_limit_bytes=V7X_VMEM_LIMIT_BYTES),
        name="branch_merge",
    )(ya, yb, yc, proj, proj, proj, w_branch)


def _prompt_layer(x2, layer, caches, cache_bufs, norm_g, w_in_r, conv_w, conv_b, ln_g, ln_b, q_norm_g, k_norm_g,
                  dconv_w, a_log, dt_bias, o_norm_g, w_branch, w_out):
    t = x2.shape[0]
    proj, cache_bufs = _proj_shift(x2, w_in_r, norm_g, q_norm_g, k_norm_g, layer, caches, cache_bufs,
                                   tm=1024, tn=PROJ_TN)
    ya, conv_new = _conv_a_prompt(proj, conv_w, conv_b, ln_g, ln_b)
    yb, kv0, kv1, kv2 = _attn_prompt(proj)
    yc, s_new = _deltanet_prompt(proj, dconv_w, a_log, dt_bias, o_norm_g)
    y = _branch_merge(ya, yb, yc, proj, w_branch, layer)
    out = _proj(y, w_out, layer, resid=x2, tm=1024, tn=1024)
    sc_new = proj[t - (DN_SHORT_K - 1):, COL_C_QKV:COL_C_QKV + DN_QKV]
    kv_new = [kv.reshape(1, 2, w, H_PER_GROUP, HEAD_DIM) for kv, (w, _) in zip((kv0, kv1, kv2), ATT_PATTERNS)]
    return out, conv_new[None], kv_new, sc_new[None], s_new[None], cache_bufs


def _sample_layer(x, layer, conv_bufs, kv_bufs, sc_buf, s0s, norm_g, w_in_r, conv_w, conv_b, ln_g, ln_b,
                  q_norm_g, k_norm_g, dconv_w, a_log, dt_bias, o_norm_g, w_branch, w_out):
    B, T, _ = x.shape
    assert T == 1
    x2 = x.reshape(B, D_MODEL)
    proj2 = _proj_few_rows(x2, w_in_r, layer, norm_g, tn=FEW_ROWS_TN)
    ya, glu, yb, yc, kv_new, s_new = _sample_mix(proj2, layer, conv_bufs, kv_bufs, sc_buf, s0s, conv_w, conv_b,
                                                 ln_g, ln_b, q_norm_g, k_norm_g, dconv_w, a_log, dt_bias, o_norm_g)
    y = _branch_merge(ya, yb, yc, proj2, w_branch, layer)
    out = _proj(y, w_out, layer, resid=x2, tm=B, tn=512).reshape(B, T, D_MODEL)
    conv_new = jnp.concatenate([conv_bufs[layer, :, 1:], glu], axis=1)
    sc_new = jnp.concatenate([sc_buf[:, 1:], proj2[:, None, COL_C_QKV:COL_C_QKV + DN_QKV]], axis=1)
    return out, conv_new, kv_new, sc_new, s_new


def kernel(x_prompt, x_sample, state_conv, cache_kv_w128, cache_kv_w512, cache_kv_w2048, state_short_conv, state_delta, norm_g, w_in, conv_w, conv_b, ln_g, ln_b, q_norm_g, k_norm_g, dconv_w, a_log, dt_bias, o_norm_g, w_branch, w_out):
    bp, t, _ = x_prompt.shape
    assert bp == 1
    xp, xs = x_prompt.reshape(t, D_MODEL), x_sample
    conv_p, conv_s, sc_p, sc_s, d_p, d_s = [], [], [], [], [], []
    kv_p = [[], [], []]
    kv_s = [[], [], []]
    caches = (cache_kv_w128, cache_kv_w512, cache_kv_w2048)
    cache_bufs = None
    w_in_r, w_branch_b, w_out_b = _prep_w_in(w_in), w_branch.astype(BF16), w_out.astype(BF16)
    for l in range(DEPTH):
        wts = (norm_g[l], w_in_r, conv_w[l], conv_b[l], ln_g[l], ln_b[l], q_norm_g[l], k_norm_g[l],
               dconv_w[l], a_log[l], dt_bias[l], o_norm_g[l], w_branch_b, w_out_b)
        xp, c_new, kv_new, sc_new, s_new, cache_bufs = _prompt_layer(xp, l, caches, cache_bufs, *wts)
        conv_p.append(c_new)
        sc_p.append(sc_new)
        d_p.append(s_new)
        for gi in range(N_ATT_GROUPS):
            kv_p[gi].append(kv_new[gi])
        xs, c_new, kv_new, sc_new, s_new = _sample_layer(
            xs, l, state_conv, (cache_kv_w128, cache_kv_w512, cache_kv_w2048),
            state_short_conv[l], state_delta, *wts)
        conv_s.append(c_new)
        sc_s.append(sc_new)
        d_s.append(s_new)
        for gi in range(N_ATT_GROUPS):
            kv_s[gi].append(kv_new[gi])
    kv_s = [[_cache_append(cache_bufs[gi], jnp.stack(kv_s[gi]))] for gi in range(N_ATT_GROUPS)]
    return (xp.reshape(bp, t, D_MODEL), xs,
            jnp.stack(conv_p), jnp.stack(conv_s),
            jnp.stack(kv_p[0]), kv_s[0][0],
            jnp.stack(kv_p[1]), kv_s[1][0],
            jnp.stack(kv_p[2]), kv_s[2][0],
            jnp.stack(sc_p), jnp.stack(sc_s),
            jnp.stack(d_p), jnp.stack(d_s))
```
